```python
import math
import jax
import jax.numpy as jnp
from jax import lax
import numpy as np

D_MODEL = 4096
BATCH = 2
SEQ = 8192
DEPTH = 2

D_MIX = D_MODEL
GROUP = D_MIX // 4
DIFF_HEADS = 8
DIFF_HD = GROUP // DIFF_HEADS
DIFF_QK = DIFF_HD // 2
GLA_HEADS = 4
GLA_DV = GROUP // GLA_HEADS
GLA_DK = GLA_DV // 2
GLA_RANK = 16
GLA_NORMALIZER = 16.0
GLA_CHUNK = 64
RWKV_HD = 64
RWKV_HEADS = GROUP // RWKV_HD
RWKV_W_RANK = 64
RWKV_A_RANK = 64
RWKV_G_RANK = 64
RWKV_V_RANK = 32
RWKV_LN_EPS = 64e-5
SWA_HD = 64
SWA_HEADS = GROUP // SWA_HD
SWA_KV_HEADS = 2
SWA_WINDOW = 128
SWA_BLOCK = 128
REL_BUCKETS = 32
REL_MAX_DIST = 128
REL_HEADS = DIFF_HEADS + SWA_HEADS
ATTN_BLOCK = 128
D_FF = -(-8 * D_MODEL // (3 * 256)) * 256
NORM_EPS = 1e-6
NEG_INF = -1e30
DIFF_COLS = 3 * GROUP
GLA_COLS = 2 * GLA_HEADS * GLA_DK + 2 * GROUP + GLA_RANK
RWKV_COLS = 3 * GROUP + RWKV_W_RANK + RWKV_A_RANK + RWKV_G_RANK
SWA_COLS = GROUP + 2 * SWA_KV_HEADS * SWA_HD
P_IN = DIFF_COLS + GLA_COLS + RWKV_COLS + SWA_COLS

kernel_name = "hybrid_parallel_heads_diff_gla_rwkv7_swa"


def _rms(t, eps=NORM_EPS):
    tf = t.astype(jnp.float32)
    return (tf * lax.rsqrt(jnp.mean(tf * tf, axis=-1, keepdims=True) + eps)).astype(t.dtype)


def _t5_bucket(dist):
    max_exact = REL_BUCKETS // 2
    n = jnp.maximum(dist, 0)
    nf = jnp.maximum(n, 1).astype(jnp.float32)
    large = max_exact + (jnp.log(nf / max_exact) / math.log(REL_MAX_DIST / max_exact)
                         * (REL_BUCKETS - max_exact)).astype(jnp.int32)
    large = jnp.minimum(large, REL_BUCKETS - 1)
    return jnp.where(n < max_exact, n, large)


def _token_shift(t, mu):
    prev = jnp.pad(t, ((0, 0), (1, 0), (0, 0)))[:, :-1]
    return t + (prev - t) * mu


def _diff_attention(cols, layer_idx, q_gain, k_gain, lam, subln, bias_tab):
    bsz, seq, _ = cols.shape
    q, k, v = jnp.split(cols, [GROUP, 2 * GROUP], axis=-1)
    q = _rms(q.reshape(bsz, seq, DIFF_HEADS, 2, DIFF_QK)) * q_gain
    k = _rms(k.reshape(bsz, seq, DIFF_HEADS, 2, DIFF_QK)) * k_gain
    v = v.reshape(bsz, seq, DIFF_HEADS, DIFF_HD)
    lam_init = 0.8 - 0.6 * math.exp(-0.3 * layer_idx)
    lam = lam.astype(jnp.float32)
    lam_full = jnp.exp(jnp.sum(lam[0] * lam[1])) - jnp.exp(jnp.sum(lam[2] * lam[3])) + lam_init
    n_blk = seq // ATTN_BLOCK
    q_blocks = q.reshape(bsz, n_blk, ATTN_BLOCK, DIFF_HEADS, 2, DIFF_QK).transpose(1, 0, 2, 3, 4, 5)
    k_pos = jnp.arange(seq)
    scale = DIFF_QK ** -0.5

    def one_block(args):
        q_i, i = args
        s = jnp.einsum("bqhpd,bkhpd->bphqk", q_i, k).astype(jnp.float32) * scale
        dist = (i * ATTN_BLOCK + jnp.arange(ATTN_BLOCK))[:, None] - k_pos[None, :]
        bias = bias_tab[_t5_bucket(dist)].astype(jnp.float32).transpose(2, 0, 1)
        s = jnp.where(dist >= 0, s + bias, NEG_INF)
        p = jax.nn.softmax(s, axis=-1)
        attn = p[:, 0] - lam_full * p[:, 1]
        return jnp.einsum("bhqk,bkhd->bqhd", attn.astype(v.dtype), v)

    o = lax.map(one_block, (q_blocks, jnp.arange(n_blk)))
    o = o.transpose(1, 0, 2, 3, 4).reshape(bsz, seq, DIFF_HEADS, DIFF_HD)
    o = _rms(o) * subln * (1.0 - lam_init)
    return o.reshape(bsz, seq, GROUP)


def _gla(cols, gate_up, gate_bias, out_gain):
    bsz, seq, _ = cols.shape
    f32 = jnp.float32
    kd = GLA_HEADS * GLA_DK
    q, k, v, g, gd = jnp.split(cols, np.cumsum([kd, kd, GROUP, GROUP]).tolist(), axis=-1)
    log_a = jax.nn.log_sigmoid((gd @ gate_up + gate_bias).astype(f32)) / GLA_NORMALIZER
    n = seq // GLA_CHUNK

    def heads(t, d):
        return (t.astype(f32).reshape(bsz, seq, GLA_HEADS, d).transpose(0, 2, 1, 3)
                .reshape(bsz, GLA_HEADS, n, GLA_CHUNK, d))

    q = heads(q, GLA_DK) * GLA_DK ** -0.5
    k = heads(k, GLA_DK)
    v = heads(v, GLA_DV)
    b = jnp.cumsum(heads(log_a, GLA_DK), axis=3)
    b_last = b[:, :, :, -1:]
    q_dec = q * jnp.exp(b)
    causal = jnp.tril(jnp.ones((GLA_CHUNK, GLA_CHUNK), dtype=bool))
    a_intra = jnp.where(causal, jnp.einsum("bhncd,bhnsd->bhncs", q_dec, k * jnp.exp(-b)), 0.0)
    o = jnp.einsum("bhncs,bhnsv->bhncv", a_intra, v)
    upd = jnp.einsum("bhnsd,bhnsv->bhndv", k * jnp.exp(b_last - b), v)
    chunk_decay = jnp.exp(b_last[:, :, :, 0])

    def step(state, inp):
        dec, u = inp
        return dec[..., None] * state + u, state

    _, s_prev = lax.scan(step, jnp.zeros((bsz, GLA_HEADS, GLA_DK, GLA_DV), f32),
                         (jnp.moveaxis(chunk_decay, 2, 0), jnp.moveaxis(upd, 2, 0)))
    s_prev = jnp.moveaxis(s_prev, 0, 2)
    o = o + jnp.einsum("bhncd,bhndv->bhncv", q_dec, s_prev)
    o = o.reshape(bsz, GLA_HEADS, seq, GLA_DV).transpose(0, 2, 1, 3)
    o = _rms(o) * out_gain
    return (o.reshape(bsz, seq, GROUP) * jax.nn.silu(g.astype(f32))).astype(cols.dtype)


def _rwkv7_scan(r, w, k, v, a, b):
    bsz, seq, nh, hd = r.shape
    xs = tuple(jnp.moveaxis(t, 1, 0) for t in (r, w, k, v, a, b))

    def step(state, inp):
        r_t, w_t, k_t, v_t, a_t, b_t = inp
        sa = jnp.einsum("bhij,bhj->bhi", state, a_t)
        state = (state * w_t[:, :, None, :] + sa[..., None] * b_t[:, :, None, :]
                 + v_t[..., None] * k_t[:, :, None, :])
        return state, jnp.einsum("bhij,bhj->bhi", state, r_t)

    _, y = lax.scan(step, jnp.zeros((bsz, nh, hd, hd), jnp.float32), xs)
    return jnp.moveaxis(y, 0, 1)


def _rwkv7_time_mix(cols, xn, v_first, vres, mu, w_up, w0, a_up, a0, g_up,
                    k_k, k_a, r_k, ln_w, ln_b):
    bsz, seq, _ = cols.shape
    f32 = jnp.float32
    cols = _token_shift(cols, mu)
    r, k, v, wd, ad, gd = jnp.split(
        cols, np.cumsum([GROUP, GROUP, GROUP, RWKV_W_RANK, RWKV_A_RANK]).tolist(), axis=-1)
    if vres is None:
        v_first = v
    else:
        vres_down, vres_mu, vres_up, v0 = vres
        v_gate = jax.nn.sigmoid(v0 + _token_shift(xn @ vres_down, vres_mu) @ vres_up)
        v = v + (v_first - v) * v_gate
    w_log = -jax.nn.softplus(-(w0 + jnp.tanh(wd) @ w_up).astype(f32)) - 0.5
    decay = jnp.exp(-jnp.exp(w_log))
    a = jax.nn.sigmoid((a0 + ad @ a_up).astype(f32))
    g = jax.nn.sigmoid(gd) @ g_up
    k = k.astype(f32)

    def heads(t):
        return t.astype(f32).reshape(bsz, seq, RWKV_HEADS, RWKV_HD)

    kk = heads(k * k_k)
    kk = kk / jnp.maximum(jnp.sqrt(jnp.sum(kk * kk, axis=-1, keepdims=True)), 1e-12)
    k_mod = heads(k * (1.0 + (a - 1.0) * k_a))
    a_h = heads(a)
    r_h = heads(r)
    v_h = heads(v)
    y = _rwkv7_scan(r_h, heads(decay), k_mod, v_h, -kk, kk * a_h)
    mean = jnp.mean(y, axis=-1, keepdims=True)
    var = jnp.mean(jnp.square(y - mean), axis=-1, keepdims=True)
    y = ((y - mean) * lax.rsqrt(var + RWKV_LN_EPS)).reshape(bsz, seq, GROUP) * ln_w + ln_b
    bonus = jnp.sum(r_h * k_mod * r_k, axis=-1, keepdims=True) * v_h
    out = (y + bonus.reshape(bsz, seq, GROUP)) * g
    return out.astype(cols.dtype), v_first


def _swa_attention(cols, q_gain, k_gain, sinks, bias_tab):
    bsz, seq, _ = cols.shape
    f32 = jnp.float32
    grp = SWA_HEADS // SWA_KV_HEADS
    n_blk = seq // SWA_BLOCK
    q, k, v = jnp.split(cols, [GROUP, GROUP + SWA_KV_HEADS * SWA_HD], axis=-1)
    q = _rms(q.reshape(bsz, seq, SWA_HEADS, SWA_HD)) * q_gain
    k = _rms(k.reshape(bsz, seq, SWA_KV_HEADS, SWA_HD)) * k_gain
    v = v.reshape(bsz, seq, SWA_KV_HEADS, SWA_HD)
    q = q.reshape(bsz, n_blk, SWA_BLOCK, SWA_KV_HEADS, grp, SWA_HD)

    def band(t):
        tp = jnp.pad(t, ((0, 0), (SWA_BLOCK, 0), (0, 0), (0, 0)))
        tp = tp.reshape(bsz, n_blk + 1, SWA_BLOCK, SWA_KV_HEADS, SWA_HD)
        return jnp.concatenate([tp[:, :-1], tp[:, 1:]], axis=2)

    kb, vb = band(k), band(v)
    s = jnp.einsum("bnqhgd,bnkhd->bnhgqk", q, kb).astype(f32) * SWA_HD ** -0.5
    qi = jnp.arange(SWA_BLOCK)
    kj = jnp.arange(2 * SWA_BLOCK)
    dist = SWA_BLOCK + qi[:, None] - kj[None, :]
    k_pos = (jnp.arange(n_blk)[:, None] - 1) * SWA_BLOCK + kj[None, :]
    valid = ((dist >= 0) & (dist < SWA_WINDOW))[None] & (k_pos >= 0)[:, None, :]
    bias = (bias_tab[_t5_bucket(dist)].astype(f32)
            .reshape(SWA_BLOCK, 2 * SWA_BLOCK, SWA_KV_HEADS, grp).transpose(2, 3, 0, 1))
    s = jnp.where(valid[None, :, None, None], s + bias, NEG_INF)
    sink = sinks.astype(f32).reshape(SWA_KV_HEADS, grp)[:, :, None, None]
    m = jnp.maximum(jnp.max(s, axis=-1, keepdims=True), sink)
    p = jnp.exp(s - m)
    p = p / (jnp.sum(p, axis=-1, keepdims=True) + jnp.exp(sink - m))
    o = jnp.einsum("bnhgqk,bnkhd->bnqhgd", p.astype(v.dtype), vb)
    return o.reshape(bsz, seq, GROUP)


def _hybrid_layer(x, c, layer_idx, v_first, vres, bias_a, bias_d, p):
    mod = (jax.nn.silu(c) @ p["ada_w"] + p["ada_b"])[:, None, :]
    sh1, sc1, g1, sh2, sc2, g2 = jnp.split(mod, 6, axis=-1)
    h = _rms(x) * (1.0 + sc1) + sh1
    cols_a, cols_b, cols_c, cols_d = jnp.split(
        h @ p["w_in"], np.cumsum([DIFF_COLS, GLA_COLS, RWKV_COLS]).tolist(), axis=-1)
    o_a = _diff_attention(cols_a, layer_idx, p["diff_q_norm"], p["diff_k_norm"],
                          p["diff_lambda"], p["diff_subln"], bias_a)
    o_b = _gla(cols_b, p["gla_gate_up"], p["gla_gate_bias"], p["gla_out_norm"])
    o_c, v_first = _rwkv7_time_mix(cols_c, h, v_first, vres, p["rwkv_mu"], p["rwkv_w_up"],
                                   p["rwkv_w0"], p["rwkv_a_up"], p["rwkv_a0"], p["rwkv_g_up"],
                                   p["rwkv_k_k"], p["rwkv_k_a"], p["rwkv_r_k"],
                                   p["rwkv_ln_w"], p["rwkv_ln_b"])
    o_d = _swa_attention(cols_d, p["swa_q_norm"], p["swa_k_norm"], p["swa_sinks"], bias_d)
    mixed = jnp.concatenate([o_a, o_b, o_c, o_d], axis=-1)
    x = x + g1 * (mixed @ p["w_out"])
    h = _rms(x) * (1.0 + sc2) + sh2
    x = x + g2 * ((jax.nn.silu(h @ p["ffn_w1"]) * (h @ p["ffn_w3"])) @ p["ffn_w2"])
    return x, v_first


def setup_inputs(seed: int = 0) -> dict:
    key = jax.random.key(seed)
    ks = iter(jax.random.split(key, 40))
    L, D = DEPTH, D_MODEL
    f32 = jnp.float32

    def nrm(shape, scale):
        return jax.random.normal(next(ks), shape, f32) * scale

    def gain(shape):
        return 1.0 + nrm(shape, 0.1)

    return {
        "x": nrm((BATCH, SEQ, D), 1.0),
        "c": nrm((BATCH, D), 1.0),
        "rel_bias": nrm((REL_BUCKETS, REL_HEADS), 0.3),
        "ada_w": nrm((L, D, 6 * D), 0.5 * D ** -0.5),
        "ada_b": nrm((L, 6 * D), 0.02),
        "w_in": nrm((L, D, P_IN), D ** -0.5),
        "w_out": nrm((L, D_MIX, D), D_MIX ** -0.5),
        "diff_q_norm": gain((L, DIFF_QK)),
        "diff_k_norm": gain((L, DIFF_QK)),
        "diff_lambda": nrm((L, 4, DIFF_QK), 0.1),
        "diff_subln": gain((L, DIFF_HD)),
        "gla_gate_up": nrm((L, GLA_RANK, GLA_HEADS * GLA_DK), GLA_RANK ** -0.5),
        "gla_gate_bias": nrm((L, GLA_HEADS * GLA_DK), 0.1),
        "gla_out_norm": gain((L, GLA_DV)),
        "rwkv_mu": jax.random.uniform(next(ks), (L, RWKV_COLS), f32),
        "rwkv_w_up": nrm((L, RWKV_W_RANK, GROUP), RWKV_W_RANK ** -0.5),
        "rwkv_w0": nrm((L, GROUP), 0.5),
        "rwkv_a_up": nrm((L, RWKV_A_RANK, GROUP), RWKV_A_RANK ** -0.5),
        "rwkv_a0": nrm((L, GROUP), 0.1),
        "rwkv_g_up": nrm((L, RWKV_G_RANK, GROUP), RWKV_G_RANK ** -0.5),
        "rwkv_k_k": 0.85 + nrm((L, GROUP), 0.05),
        "rwkv_k_a": gain((L, GROUP)),
        "rwkv_r_k": nrm((L, RWKV_HEADS, RWKV_HD), 0.1),
        "rwkv_ln_w": gain((L, GROUP)),
        "rwkv_ln_b": nrm((L, GROUP), 0.02),
        "rwkv_vres_down": nrm((L - 1, D, RWKV_V_RANK), D ** -0.5),
        "rwkv_vres_mu": jax.random.uniform(next(ks), (L - 1, RWKV_V_RANK), f32),
        "rwkv_vres_up": nrm((L - 1, RWKV_V_RANK, GROUP), RWKV_V_RANK ** -0.5),
        "rwkv_v0": nrm((L - 1, GROUP), 0.5),
        "swa_q_norm": gain((L, SWA_HD)),
        "swa_k_norm": gain((L, SWA_HD)),
        "swa_sinks": nrm((L, SWA_HEADS), 0.5),
        "ffn_w1": nrm((L, D, D_FF), D ** -0.5),
        "ffn_w3": nrm((L, D, D_FF), D ** -0.5),
        "ffn_w2": nrm((L, D_FF, D), D_FF ** -0.5),
    }


def reference(x, c, rel_bias, ada_w, ada_b, w_in, w_out, diff_q_norm, diff_k_norm,
              diff_lambda, diff_subln, gla_gate_up, gla_gate_bias, gla_out_norm, rwkv_mu,
              rwkv_w_up, rwkv_w0, rwkv_a_up, rwkv_a0, rwkv_g_up, rwkv_k_k, rwkv_k_a, rwkv_r_k,
              rwkv_ln_w, rwkv_ln_b, rwkv_vres_down, rwkv_vres_mu, rwkv_vres_up, rwkv_v0,
              swa_q_norm, swa_k_norm, swa_sinks, ffn_w1, ffn_w3, ffn_w2):
    bias_a = rel_bias[:, :DIFF_HEADS]
    bias_d = rel_bias[:, DIFF_HEADS:]
    v_first = None
    for l in range(DEPTH):
        p = dict(ada_w=ada_w[l], ada_b=ada_b[l], w_in=w_in[l], w_out=w_out[l],
                 diff_q_norm=diff_q_norm[l], diff_k_norm=diff_k_norm[l],
                 diff_lambda=diff_lambda[l], diff_subln=diff_subln[l],
                 gla_gate_up=gla_gate_up[l], gla_gate_bias=gla_gate_bias[l],
                 gla_out_norm=gla_out_norm[l], rwkv_mu=rwkv_mu[l], rwkv_w_up=rwkv_w_up[l],
                 rwkv_w0=rwkv_w0[l], rwkv_a_up=rwkv_a_up[l], rwkv_a0=rwkv_a0[l],
                 rwkv_g_up=rwkv_g_up[l], rwkv_k_k=rwkv_k_k[l], rwkv_k_a=rwkv_k_a[l],
                 rwkv_r_k=rwkv_r_k[l], rwkv_ln_w=rwkv_ln_w[l], rwkv_ln_b=rwkv_ln_b[l],
                 swa_q_norm=swa_q_norm[l], swa_k_norm=swa_k_norm[l], swa_sinks=swa_sinks[l],
                 ffn_w1=ffn_w1[l], ffn_w3=ffn_w3[l], ffn_w2=ffn_w2[l])
        if l == 0:
            vres = None
        else:
            vres = (rwkv_vres_down[l - 1], rwkv_vres_mu[l - 1], rwkv_vres_up[l - 1], rwkv_v0[l - 1])
        x, v_first = _hybrid_layer(x, c, l, v_first, vres, bias_a, bias_d, p)
    return x
```

```python
import functools
import math

import jax
import jax.numpy as jnp
from jax import lax
from jax.experimental import pallas as pl
from jax.experimental.pallas import tpu as pltpu

F32 = jnp.float32
BF16 = jnp.bfloat16
HI = lax.Precision.HIGHEST

D_MODEL = 4096
GROUP = 1024
D_FF = 11008
NORM_EPS = 1e-6
NEG_INF = -1e30
LANE = 128
HALF = 64

DIFF_HEADS = 8
DIFF_QK = 64
GLA_HEADS = 4
GLA_DK = 128
GLA_DV = 256
GLA_RANK = 16
GLA_NORMALIZER = 16.0
CHUNK = 64
RWKV_LN_EPS = 64e-5
SWA_HEADS = 16
SWA_KV_HEADS = 2
SWA_BLOCK = 128
REL_BUCKETS = 32
REL_MAX_DIST = 128

OFF_A = 0
OFF_B = 3072
OFF_C = 6144
OFF_D = 9216
OFF_S = 10496
SMALL = 256
NCOLS = OFF_S + SMALL
S_GLA, S_WD, S_AD, S_GD, S_VR = 0, 16, 80, 144, 208

VMEM_LIMIT = 56 * 1024 * 1024


def _cparams(sem, vmem=VMEM_LIMIT):
    return pltpu.CompilerParams(dimension_semantics=sem, vmem_limit_bytes=vmem)


def _dot(a, b, prec=None):
    return jnp.dot(a, b, preferred_element_type=F32, precision=prec)


def _dot_nt(a, b, prec=None):
    return lax.dot_general(a, b, (((1,), (1,)), ((), ())), preferred_element_type=F32, precision=prec)


def _dot_tn(a, b, prec=None):
    return lax.dot_general(a, b, (((0,), (0,)), ((), ())), preferred_element_type=F32, precision=prec)


def _sigmoid(z):
    return 1.0 / (1.0 + jnp.exp(-z))


def _softplus(z):
    return jnp.maximum(z, 0.0) + jnp.log(1.0 + jnp.exp(-jnp.abs(z)))


def _seg_ones():
    r = lax.broadcasted_iota(jnp.int32, (LANE, LANE), 0) // HALF
    c = lax.broadcasted_iota(jnp.int32, (LANE, LANE), 1) // HALF
    return (r == c).astype(F32)


def _segsum(x, ones):
    return _dot(x, ones, HI)


def _ada_body(c_ref, w_ref, b_ref, o_ref):
    c = c_ref[...]
    s = (c * _sigmoid(c)).astype(BF16)
    o_ref[...] = _dot(s, w_ref[...].astype(BF16)) + b_ref[...]


def _ada(c8, ada_w, ada_b, layer, tn=512):
    _, d, n = ada_w.shape
    return pl.pallas_call(
        _ada_body,
        out_shape=jax.ShapeDtypeStruct((8, n), F32),
        grid=(n // tn,),
        in_specs=[pl.BlockSpec((8, d), lambda j: (0, 0)),
                  pl.BlockSpec((None, d, tn), lambda j: (layer, 0, j)),
                  pl.BlockSpec((1, tn), lambda j: (0, j))],
        out_specs=pl.BlockSpec((8, tn), lambda j: (0, j)),
        compiler_params=_cparams(("parallel",)),
        name="ada_mod",
    )(c8, ada_w, ada_b[layer].reshape(1, n))


def _norm_body(x_ref, sc_ref, sh_ref, o_ref):
    x = x_ref[0]
    ms = jnp.mean(x * x, axis=-1, keepdims=True)
    h = x * lax.rsqrt(ms + NORM_EPS) * (1.0 + sc_ref[0]) + sh_ref[0]
    o_ref[...] = h.astype(o_ref.dtype)


def _norm_mod(x, mod3, sc_idx, sh_idx, ts=256):
    bsz, seq, d = x.shape
    ts = min(ts, seq)
    ns = seq // ts
    return pl.pallas_call(
        _norm_body,
        out_shape=jax.ShapeDtypeStruct((bsz * seq, d), BF16),
        grid=(bsz, ns),
        in_specs=[pl.BlockSpec((1, ts, d), lambda b, i: (b, i, 0)),
                  pl.BlockSpec((1, 1, d), lambda b, i: (b * 6 + sc_idx, 0, 0)),
                  pl.BlockSpec((1, 1, d), lambda b, i: (b * 6 + sh_idx, 0, 0))],
        out_specs=pl.BlockSpec((ts, d), lambda b, i: (b * ns + i, 0)),
        compiler_params=_cparams(("parallel", "parallel")),
        name="norm_mod",
    )(x, mod3, mod3)


def _mm_body(a_ref, b_ref, o_ref):
    o_ref[...] = _dot(a_ref[...], b_ref[...]).astype(o_ref.dtype)


def _matmul(a, b, out_dtype, tm=1024, tn=512):
    m, k = a.shape
    n = b.shape[1]
    tm = min(tm, m)
    return pl.pallas_call(
        _mm_body,
        out_shape=jax.ShapeDtypeStruct((m, n), out_dtype),
        grid=(m // tm, n // tn),
        in_specs=[pl.BlockSpec((tm, k), lambda i, j: (i, 0)),
                  pl.BlockSpec((k, tn), lambda i, j: (0, j))],
        out_specs=pl.BlockSpec((tm, tn), lambda i, j: (i, j)),
        compiler_params=_cparams(("parallel", "parallel")),
        name="in_proj",
    )(a, b)


def _outproj_body(a0, a1, a2, a3, w_ref, x_ref, g_ref, o_ref):
    acc = _dot(a0[...], w_ref[0 * GROUP:1 * GROUP, :])
    acc += _dot(a1[...], w_ref[1 * GROUP:2 * GROUP, :])
    acc += _dot(a2[...], w_ref[2 * GROUP:3 * GROUP, :])
    acc += _dot(a3[...], w_ref[3 * GROUP:4 * GROUP, :])
    o_ref[0] = x_ref[0] + g_ref[0] * acc


def _out_proj(parts, w, x, mod3, gate_idx, tm=1024, tn=512):
    bsz, seq, d = x.shape
    tm = min(tm, seq)
    ns = seq // tm
    a_spec = pl.BlockSpec((tm, GROUP), lambda b, i, j: (b * ns + i, 0))
    return pl.pallas_call(
        _outproj_body,
        out_shape=jax.ShapeDtypeStruct((bsz, seq, d), F32),
        grid=(bsz, ns, d // tn),
        in_specs=[a_spec, a_spec, a_spec, a_spec,
                  pl.BlockSpec((4 * GROUP, tn), lambda b, i, j: (0, j)),
                  pl.BlockSpec((1, tm, tn), lambda b, i, j: (b, i, j)),
                  pl.BlockSpec((1, 1, tn), lambda b, i, j: (b * 6 + gate_idx, 0, j))],
        out_specs=pl.BlockSpec((1, tm, tn), lambda b, i, j: (b, i, j)),
        compiler_params=_cparams(("parallel", "parallel", "parallel")),
        name="out_proj",
    )(*parts, w, x, mod3)


def _ffn_up_body(a_ref, w1_ref, w3_ref, o_ref):
    a = a_ref[...]
    u = _dot(a, w1_ref[...])
    v = _dot(a, w3_ref[...])
    o_ref[...] = (u * _sigmoid(u) * v).astype(o_ref.dtype)


def _ffn_up(h, w1, w3, tm=1024, tn=256):
    m, k = h.shape
    n = w1.shape[1]
    tm = min(tm, m)
    return pl.pallas_call(
        _ffn_up_body,
        out_shape=jax.ShapeDtypeStruct((m, n), BF16),
        grid=(m // tm, n // tn),
        in_specs=[pl.BlockSpec((tm, k), lambda i, j: (i, 0)),
                  pl.BlockSpec((k, tn), lambda i, j: (0, j)),
                  pl.BlockSpec((k, tn), lambda i, j: (0, j))],
        out_specs=pl.BlockSpec((tm, tn), lambda i, j: (i, j)),
        compiler_params=_cparams(("parallel", "parallel")),
        name="ffn_up",
    )(h, w1, w3)


def _ffn_down_body(a_ref, w_ref, x_ref, g_ref, o_ref):
    o_ref[0] = x_ref[0] + g_ref[0] * _dot(a_ref[...], w_ref[...])


def _ffn_down(a, w, x, mod3, gate_idx, tm=512, tn=256):
    bsz, seq, d = x.shape
    k = a.shape[1]
    tm = min(tm, seq)
    ns = seq // tm
    return pl.pallas_call(
        _ffn_down_body,
        out_shape=jax.ShapeDtypeStruct((bsz, seq, d), F32),
        grid=(bsz, ns, d // tn),
        in_specs=[pl.BlockSpec((tm, k), lambda b, i, j: (b * ns + i, 0)),
                  pl.BlockSpec((k, tn), lambda b, i, j: (0, j)),
                  pl.BlockSpec((1, tm, tn), lambda b, i, j: (b, i, j)),
                  pl.BlockSpec((1, 1, tn), lambda b, i, j: (b * 6 + gate_idx, 0, j))],
        out_specs=pl.BlockSpec((1, tm, tn), lambda b, i, j: (b, i, j)),
        compiler_params=_cparams(("parallel", "parallel", "parallel")),
        name="ffn_down",
    )(a, w, x, mod3)


def _bias_body(tab_ref, o_ref, *, off, window, shift):
    h = pl.program_id(0)
    rows, cols = o_ref.shape[1], o_ref.shape[2]
    qi = lax.broadcasted_iota(jnp.int32, (rows, cols), 0)
    kj = lax.broadcasted_iota(jnp.int32, (rows, cols), 1)
    dist = off + qi - kj
    max_exact = REL_BUCKETS // 2
    n = jnp.maximum(dist, 0)
    nf = jnp.maximum(n, 1).astype(F32)
    large = max_exact + (jnp.log(nf / max_exact) / math.log(REL_MAX_DIST / max_exact)
                         * (REL_BUCKETS - max_exact)).astype(jnp.int32)
    large = jnp.minimum(large, REL_BUCKETS - 1)
    bucket = jnp.where(n < max_exact, n, large)
    last = tab_ref[REL_BUCKETS - 1, h]
    bias = jnp.full((rows, cols), last, F32)
    for b in range(REL_BUCKETS - 1):
        bias = jnp.where(bucket == b, tab_ref[b, h], bias)
    if shift:
        bias = bias - last
    valid = dist >= 0
    if window is not None:
        valid = jnp.logical_and(valid, dist < window)
    o_ref[0] = jnp.where(valid, bias, NEG_INF)


def _bias_tiles(table, rows, cols, off, window, shift):
    nh = table.shape[1]
    return pl.pallas_call(
        functools.partial(_bias_body, off=off, window=window, shift=shift),
        out_shape=jax.ShapeDtypeStruct((nh, rows, cols), F32),
        grid=(nh,),
        in_specs=[pl.BlockSpec(memory_space=pltpu.SMEM)],
        out_specs=pl.BlockSpec((1, rows, cols), lambda h: (h, 0, 0)),
        compiler_params=_cparams(("parallel",)),
        name="rel_bias_tiles",
    )(table)


def _prep_a_body(c_ref, qg_ref, kg_ref, o_ref):
    ones = _seg_ones()
    scale = DIFF_QK ** -0.5
    for j in range(2 * DIFF_HEADS):
        x = c_ref[:, j * LANE:(j + 1) * LANE]
        ms = _segsum(x * x, ones) * (1.0 / DIFF_QK)
        gain = qg_ref[...] * scale if j < DIFF_HEADS else kg_ref[...]
        o_ref[:, j * LANE:(j + 1) * LANE] = (x * lax.rsqrt(ms + NORM_EPS) * gain).astype(o_ref.dtype)
    o_ref[:, 2 * GROUP:] = c_ref[:, 2 * GROUP:].astype(o_ref.dtype)


def _prep_a(cols, q_gain, k_gain, tm=256):
    t = cols.shape[0]
    tm = min(tm, t)
    qg = jnp.tile(q_gain, 2).reshape(1, LANE)
    kg = jnp.tile(k_gain, 2).reshape(1, LANE)
    vec = pl.BlockSpec((1, LANE), lambda i: (0, 0))
    return pl.pallas_call(
        _prep_a_body,
        out_shape=jax.ShapeDtypeStruct((t, 3 * GROUP), BF16),
        grid=(t // tm,),
        in_specs=[pl.BlockSpec((tm, 3 * GROUP), lambda i: (i, OFF_A // (3 * GROUP))), vec, vec],
        out_specs=pl.BlockSpec((tm, 3 * GROUP), lambda i: (i, 0)),
        compiler_params=_cparams(("parallel",)),
        name="diff_prep",
    )(cols, qg, kg)


def _flash_body(q_ref, k_ref, v_ref, bd_ref, bs_ref, lam_ref, sub_ref, o_ref,
                qlo, qhi, m1, l1, a1, m2, l2, a2, *, lam_init):
    i = pl.program_id(2)
    j = pl.program_id(3)

    @pl.when(j == 0)
    def _():
        q = q_ref[...]
        lane = lax.broadcasted_iota(jnp.int32, q.shape, 1)
        zero = jnp.zeros_like(q)
        qlo[...] = jnp.where(lane < HALF, q, zero)
        qhi[...] = jnp.where(lane >= HALF, q, zero)
        for m, l, a in ((m1, l1, a1), (m2, l2, a2)):
            m[...] = jnp.full(m.shape, NEG_INF, F32)
            l[...] = jnp.zeros(l.shape, F32)
            a[...] = jnp.zeros(a.shape, F32)

    def step(bias):
        k = k_ref[...]
        v = v_ref[...]
        for qq, m, l, a in ((qlo, m1, l1, a1), (qhi, m2, l2, a2)):
            s = _dot_nt(qq[...], k)
            if bias is not None:
                s = s + bias
            m_old = m[...]
            m_new = jnp.maximum(m_old, jnp.max(s, axis=-1, keepdims=True))
            alpha = jnp.exp(m_old - m_new)
            p = jnp.exp(s - m_new)
            l[...] = alpha * l[...] + jnp.sum(p, axis=-1, keepdims=True)
            a[...] = alpha * a[...] + _dot(p.astype(BF16), v)
            m[...] = m_new

    @pl.when(j < i - 1)
    def _():
        step(None)

    @pl.when(j == i - 1)
    def _():
        step(bs_ref[0])

    @pl.when(j == i)
    def _():
        step(bd_ref[0])
        lam = lam_ref[...]
        e1 = jnp.exp(jnp.sum(lam[0:1] * lam[1:2], axis=-1, keepdims=True))
        e2 = jnp.exp(jnp.sum(lam[2:3] * lam[3:4], axis=-1, keepdims=True))
        lam_full = e1 - e2 + lam_init
        o = a1[...] / l1[...] - lam_full * (a2[...] / l2[...])
        ms = jnp.mean(o * o, axis=-1, keepdims=True)
        o = o * lax.rsqrt(ms + NORM_EPS) * sub_ref[...] * (1.0 - lam_init)
        o_ref[...] = o.astype(o_ref.dtype)


def _diff_attention(qkv, bias_diag, bias_sub, lam, subln, bsz, seq, layer_idx, tile):
    nq = seq // tile
    lam_init = 0.8 - 0.6 * math.exp(-0.3 * layer_idx)
    kv_row = lambda b, h, i, j: b * nq + jnp.minimum(i, j)
    return pl.pallas_call(
        functools.partial(_flash_body, lam_init=lam_init),
        out_shape=jax.ShapeDtypeStruct((bsz * seq, GROUP), BF16),
        grid=(bsz, DIFF_HEADS, nq, nq),
        in_specs=[pl.BlockSpec((tile, LANE), lambda b, h, i, j: (b * nq + i, h)),
                  pl.BlockSpec((tile, LANE), lambda b, h, i, j: (kv_row(b, h, i, j), DIFF_HEADS + h)),
                  pl.BlockSpec((tile, LANE), lambda b, h, i, j: (kv_row(b, h, i, j), 2 * DIFF_HEADS + h)),
                  pl.BlockSpec((1, tile, tile), lambda b, h, i, j: (h, 0, 0)),
                  pl.BlockSpec((1, tile, tile), lambda b, h, i, j: (h, 0, 0)),
                  pl.BlockSpec((4, DIFF_QK), lambda b, h, i, j: (0, 0)),
                  pl.BlockSpec((1, LANE), lambda b, h, i, j: (0, 0))],
        out_specs=pl.BlockSpec((tile, LANE), lambda b, h, i, j: (b * nq + i, h)),
        scratch_shapes=[pltpu.VMEM((tile, LANE), BF16), pltpu.VMEM((tile, LANE), BF16),
                        pltpu.VMEM((tile, 1), F32), pltpu.VMEM((tile, 1), F32), pltpu.VMEM((tile, LANE), F32),
                        pltpu.VMEM((tile, 1), F32), pltpu.VMEM((tile, 1), F32), pltpu.VMEM((tile, LANE), F32)],
        compiler_params=_cparams(("parallel", "parallel", "parallel", "arbitrary")),
        name="diff_attention",
    )(qkv, qkv, qkv, bias_diag, bias_sub, lam, subln.reshape(1, LANE))


def _swa_body(q_ref, kp_ref, kc_ref, vp_ref, vc_ref, bias_ref, qg_ref, kg_ref, sink_ref, o_ref):
    n = pl.program_id(1)
    g = pl.program_id(2)
    ones = _seg_ones()
    lane = lax.broadcasted_iota(jnp.int32, (SWA_BLOCK, LANE), 1)
    lo = lane < HALF

    def norm(x, gain):
        ms = _segsum(x * x, ones) * (1.0 / HALF)
        return x * lax.rsqrt(ms + NORM_EPS) * gain

    k = jnp.concatenate([norm(kp_ref[...], kg_ref[...]), norm(kc_ref[...], kg_ref[...])], axis=0)
    v = jnp.concatenate([vp_ref[...], vc_ref[...]], axis=0)
    k_sw = pltpu.roll(k, HALF, axis=1)
    v_sw = pltpu.roll(v, HALF, axis=1)
    first_head = g == 0
    lo2 = lax.broadcasted_iota(jnp.int32, (2 * SWA_BLOCK, LANE), 1) < HALF
    k_a = jnp.where(first_head, k, k_sw).astype(BF16)
    k_b = jnp.where(first_head, k_sw, k).astype(BF16)
    v_mine_lo = jnp.where(first_head, v, v_sw)
    v_mine_hi = jnp.where(first_head, v_sw, v)
    v_a = jnp.where(lo2, v_mine_lo, 0.0).astype(BF16)
    v_b = jnp.where(lo2, 0.0, v_mine_hi).astype(BF16)
    kcol = lax.broadcasted_iota(jnp.int32, (SWA_BLOCK, 2 * SWA_BLOCK), 1)
    pad = jnp.logical_and(n == 0, kcol < SWA_BLOCK)
    scale = HALF ** -0.5
    grp = SWA_HEADS // SWA_KV_HEADS
    for pr in range(grp // 2):
        q = norm(q_ref[:, pr * LANE:(pr + 1) * LANE], qg_ref[...] * scale)
        acc = jnp.zeros((SWA_BLOCK, LANE), F32)
        for e, (kk, vv) in enumerate(((k_a, v_a), (k_b, v_b))):
            hh = 2 * pr + e
            qm = jnp.where(lo if e == 0 else jnp.logical_not(lo), q, 0.0).astype(BF16)
            s = _dot_nt(qm, kk) + bias_ref[hh]
            s = jnp.where(pad, NEG_INF, s)
            sink = sink_ref[g * grp + hh]
            m = jnp.maximum(jnp.max(s, axis=-1, keepdims=True), sink)
            p = jnp.exp(s - m)
            p = p / (jnp.sum(p, axis=-1, keepdims=True) + jnp.exp(sink - m))
            acc += _dot(p.astype(BF16), vv)
        o_ref[:, pr * LANE:(pr + 1) * LANE] = acc.astype(o_ref.dtype)


def _swa_attention(cols, bias, q_gain, k_gain, sinks, bsz, seq):
    nb = seq // SWA_BLOCK
    grp = SWA_HEADS // SWA_KV_HEADS
    qw = grp * HALF
    kcol = (OFF_D + GROUP) // LANE
    vcol = kcol + 1
    prev = lambda b, n, g: b * nb + jnp.maximum(n - 1, 0)
    cur = lambda b, n, g: b * nb + n
    vec = pl.BlockSpec((1, LANE), lambda b, n, g: (0, 0))
    return pl.pallas_call(
        _swa_body,
        out_shape=jax.ShapeDtypeStruct((bsz * seq, GROUP), BF16),
        grid=(bsz, nb, SWA_KV_HEADS),
        in_specs=[pl.BlockSpec((SWA_BLOCK, qw), lambda b, n, g: (cur(b, n, g), OFF_D // qw + g)),
                  pl.BlockSpec((SWA_BLOCK, LANE), lambda b, n, g: (prev(b, n, g), kcol)),
                  pl.BlockSpec((SWA_BLOCK, LANE), lambda b, n, g: (cur(b, n, g), kcol)),
                  pl.BlockSpec((SWA_BLOCK, LANE), lambda b, n, g: (prev(b, n, g), vcol)),
                  pl.BlockSpec((SWA_BLOCK, LANE), lambda b, n, g: (cur(b, n, g), vcol)),
                  pl.BlockSpec((grp, SWA_BLOCK, 2 * SWA_BLOCK), lambda b, n, g: (g, 0, 0)),
                  vec, vec,
                  pl.BlockSpec(memory_space=pltpu.SMEM)],
        out_specs=pl.BlockSpec((SWA_BLOCK, qw), lambda b, n, g: (cur(b, n, g), g)),
        compiler_params=_cparams(("parallel", "parallel", "parallel")),
        name="swa_attention",
    )(cols, cols, cols, cols, cols, bias,
      jnp.tile(q_gain, 2).reshape(1, LANE), jnp.tile(k_gain, 2).reshape(1, LANE), sinks)


def _prep_c_body(*refs, seq, with_vres):
    (cm_ref, cs_ref, pm_ref, ps_ref, mum_ref, mus_ref, wup_ref, aup_ref, gup_ref, vup_ref, glaup_ref,
     w0_ref, a0_ref, v0_ref, kk_ref, ka_ref, glab_ref) = refs[:17]
    rest = refs[17:]
    if with_vres:
        vf_ref, rest = rest[0], rest[1:]
    r_ref, lw_ref, k_ref, v_ref, nk_ref, kb_ref, g_ref, la_ref = rest
    tm = cm_ref.shape[0]
    i = pl.program_id(0)
    seq_start = (i * tm) % seq == 0

    def shifted(cur, prev_rows, mu):
        row = lax.broadcasted_iota(jnp.int32, cur.shape, 0)
        before = jnp.where(seq_start, 0.0, prev_rows[7:8, :])
        prev = jnp.where(row == 0, before, pltpu.roll(cur, 1, axis=0))
        return cur + (prev - cur) * mu

    cs = cs_ref[...]
    la_ref[...] = -_softplus(-(_dot(cs, glaup_ref[...], HI) + glab_ref[...])) * (1.0 / GLA_NORMALIZER)
    sm = shifted(cm_ref[...], pm_ref[...], mum_ref[...])
    ss = shifted(cs, ps_ref[...], mus_ref[...])
    r = sm[:, :GROUP]
    k = sm[:, GROUP:2 * GROUP]
    v = sm[:, 2 * GROUP:]
    if with_vres:
        gate = _sigmoid(v0_ref[...] + _dot(ss, vup_ref[...], HI))
        v = v + (vf_ref[...] - v) * gate
    w_log = -_softplus(-(w0_ref[...] + _dot(jnp.tanh(ss), wup_ref[...], HI))) - 0.5
    a = _sigmoid(a0_ref[...] + _dot(ss, aup_ref[...], HI))
    r_ref[...] = r
    lw_ref[...] = -jnp.exp(w_log)
    v_ref[...] = v
    g_ref[...] = _dot(_sigmoid(ss), gup_ref[...], HI)
    k_ref[...] = k * (1.0 + (a - 1.0) * ka_ref[...])
    ones = _seg_ones()
    kk = k * kk_ref[...]
    for j in range(GROUP // LANE):
        sl = slice(j * LANE, (j + 1) * LANE)
        x = kk[:, sl]
        nrm = jnp.maximum(jnp.sqrt(_segsum(x * x, ones)), 1e-12)
        x = x / nrm
        nk_ref[:, sl] = x
        kb_ref[:, sl] = x * a[:, sl]


def _prep_c(cols, seq, mu_main, mu_small, w_up, a_up, g_up, v_up, gla_up, w0, a0, v0, k_k, k_a, gla_bias,
            v_first, tm=256):
    t = cols.shape[0]
    tm = min(tm, seq)
    with_vres = v_first is not None
    cmain = OFF_C // (3 * GROUP)
    csmall = OFF_S // SMALL
    prev8 = lambda i: jnp.maximum(i * (tm // 8) - 1, 0)
    full = lambda r, c: pl.BlockSpec((r, c), lambda i: (0, 0))
    in_specs = [pl.BlockSpec((tm, 3 * GROUP), lambda i: (i, cmain)),
                pl.BlockSpec((tm, SMALL), lambda i: (i, csmall)),
                pl.BlockSpec((8, 3 * GROUP), lambda i: (prev8(i), cmain)),
                pl.BlockSpec((8, SMALL), lambda i: (prev8(i), csmall)),
                full(1, 3 * GROUP), full(1, SMALL),
                full(SMALL, GROUP), full(SMALL, GROUP), full(SMALL, GROUP), full(SMALL, GROUP),
                full(SMALL, GLA_HEADS * GLA_DK),
                full(1, GROUP), full(1, GROUP), full(1, GROUP), full(1, GROUP), full(1, GROUP),
                full(1, GLA_HEADS * GLA_DK)]
    args = [cols, cols, cols, cols, mu_main, mu_small, w_up, a_up, g_up, v_up, gla_up,
            w0, a0, v0, k_k, k_a, gla_bias]
    if with_vres:
        in_specs.append(pl.BlockSpec((tm, GROUP), lambda i: (i, 0)))
        args.append(v_first)
    row = pl.BlockSpec((tm, GROUP), lambda i: (i, 0))
    out = jax.ShapeDtypeStruct((t, GROUP), F32)
    return pl.pallas_call(
        functools.partial(_prep_c_body, seq=seq, with_vres=with_vres),
        out_shape=[out] * 7 + [jax.ShapeDtypeStruct((t, GLA_HEADS * GLA_DK), F32)],
        grid=(t // tm,),
        in_specs=in_specs,
        out_specs=[row] * 7 + [pl.BlockSpec((tm, GLA_HEADS * GLA_DK), lambda i: (i, 0))],
        compiler_params=_cparams(("parallel",)),
        name="rwkv_gla_prep",
    )(*args)


def _gla_body(q_ref, k_ref, v_ref, g_ref, la_ref, gain_ref, o_ref, st_ref):
    @pl.when(pl.program_id(2) == 0)
    def _():
        st_ref[...] = jnp.zeros(st_ref.shape, F32)

    row = lax.broadcasted_iota(jnp.int32, (CHUNK, CHUNK), 0)
    col = lax.broadcasted_iota(jnp.int32, (CHUNK, CHUNK), 1)
    causal = row >= col
    tri = causal.astype(F32)
    nchunk = q_ref.shape[0] // CHUNK

    def chunk(c, carry):
        sl = pl.ds(pl.multiple_of(c * CHUNK, CHUNK), CHUNK)
        q = q_ref[sl, :] * (GLA_DK ** -0.5)
        k = k_ref[sl, :]
        v = v_ref[sl, :]
        b = _dot(tri, la_ref[sl, :], HI)
        b_last = b[CHUNK - 1:CHUNK, :]
        q_dec = q * jnp.exp(b)
        a_intra = jnp.where(causal, _dot_nt(q_dec, k * jnp.exp(-b), HI), 0.0)
        state = st_ref[...]
        o = _dot(a_intra, v, HI) + _dot_nt(q_dec, state, HI)
        st_ref[...] = state * jnp.exp(b_last) + _dot_tn(v, k * jnp.exp(b_last - b), HI)
        ms = jnp.mean(o * o, axis=-1, keepdims=True)
        o = o * lax.rsqrt(ms + NORM_EPS) * gain_ref[...]
        gate = g_ref[sl, :]
        o_ref[sl, :] = (o * (gate * _sigmoid(gate))).astype(o_ref.dtype)
        return carry

    lax.fori_loop(0, nchunk, chunk, 0)


def _gla(cols, log_a, out_gain, bsz, seq, tc=512):
    tc = min(tc, seq)
    ns = seq // tc
    qc = OFF_B // GLA_DK
    kc = qc + GLA_HEADS
    vc = (OFF_B + 2 * GLA_HEADS * GLA_DK) // GLA_DV
    gc = vc + GLA_HEADS
    rows = lambda b, h, i: b * ns + i
    return pl.pallas_call(
        _gla_body,
        out_shape=jax.ShapeDtypeStruct((bsz * seq, GROUP), BF16),
        grid=(bsz, GLA_HEADS, ns),
        in_specs=[pl.BlockSpec((tc, GLA_DK), lambda b, h, i: (rows(b, h, i), qc + h)),
                  pl.BlockSpec((tc, GLA_DK), lambda b, h, i: (rows(b, h, i), kc + h)),
                  pl.BlockSpec((tc, GLA_DV), lambda b, h, i: (rows(b, h, i), vc + h)),
                  pl.BlockSpec((tc, GLA_DV), lambda b, h, i: (rows(b, h, i), gc + h)),
                  pl.BlockSpec((tc, GLA_DK), lambda b, h, i: (rows(b, h, i), h)),
                  pl.BlockSpec((1, GLA_DV), lambda b, h, i: (0, 0))],
        out_specs=pl.BlockSpec((tc, GLA_DV), lambda b, h, i: (rows(b, h, i), h)),
        scratch_shapes=[pltpu.VMEM((GLA_DV, GLA_DK), F32)],
        compiler_params=_cparams(("parallel", "parallel", "arbitrary")),
        name="gla_scan",
    )(cols, cols, cols, cols, log_a, out_gain.reshape(1, GLA_DV))


def _rwkv_body(r_ref, lw_ref, k_ref, v_ref, nk_ref, kb_ref, g_ref, rk_ref, lnw_ref, lnb_ref, o_ref, st_ref):
    @pl.when(pl.program_id(2) == 0)
    def _():
        st_ref[...] = jnp.zeros(st_ref.shape, F32)

    two = 2 * CHUNK
    row = lax.broadcasted_iota(jnp.int32, (two, two), 0)
    col = lax.broadcasted_iota(jnp.int32, (two, two), 1)
    same = (row // CHUNK) == (col // CHUNK)
    strict = jnp.logical_and(same, (row % CHUNK) > (col % CHUNK))
    incl = jnp.logical_and(same, (row % CHUNK) >= (col % CHUNK))
    eye = (row == col).astype(F32)
    crow = lax.broadcasted_iota(jnp.int32, (CHUNK, CHUNK), 0)
    ccol = lax.broadcasted_iota(jnp.int32, (CHUNK, CHUNK), 1)
    tri = (crow >= ccol).astype(F32)
    lo = lax.broadcasted_iota(jnp.int32, (CHUNK, LANE), 1) < HALF
    ones = _seg_ones()
    nchunk = r_ref.shape[0] // CHUNK

    def stack(x):
        return jnp.concatenate([jnp.where(lo, x, 0.0), jnp.where(lo, 0.0, x)], axis=0)

    def chunk(c, carry):
        sl = pl.ds(pl.multiple_of(c * CHUNK, CHUNK), CHUNK)
        r = r_ref[sl, :]
        lw = lw_ref[sl, :]
        k = k_ref[sl, :]
        v = v_ref[sl, :]
        nk = nk_ref[sl, :]
        kb = kb_ref[sl, :]
        cum = _dot(tri, lw, HI)
        last = cum[CHUNK - 1:CHUNK, :]
        e_neg = jnp.exp(-cum)
        e_rem = jnp.exp(last - cum)
        a_st = stack(-nk * jnp.exp(cum - lw))
        b_st = stack(kb * e_neg)
        k_st = stack(k * e_neg)
        r_st = stack(r * jnp.exp(cum))
        bh_st = stack(kb * e_rem)
        kh_st = stack(k * e_rem)
        v_st = stack(v)
        a_ab = jnp.where(strict, _dot_nt(a_st, b_st, HI), 0.0)
        a_ak = jnp.where(strict, _dot_nt(a_st, k_st, HI), 0.0)
        r_b = jnp.where(incl, _dot_nt(r_st, b_st, HI), 0.0)
        r_k = jnp.where(incl, _dot_nt(r_st, k_st, HI), 0.0)
        inv = eye + a_ab
        pw = a_ab
        for _ in range(5):
            pw = _dot(pw, pw, HI)
            inv = inv + _dot(inv, pw, HI)
        state = st_ref[...]
        u = _dot(inv, _dot_nt(a_st, state, HI) + _dot(a_ak, v_st, HI), HI)
        y_st = _dot_nt(r_st, state, HI) + _dot(r_b, u, HI) + _dot(r_k, v_st, HI)
        st_ref[...] = state * jnp.exp(last) + _dot_tn(u, bh_st, HI) + _dot_tn(v_st, kh_st, HI)
        y = y_st[:CHUNK, :] + y_st[CHUNK:, :]
        mean = _segsum(y, ones) * (1.0 / HALF)
        d = y - mean
        var = _segsum(d * d, ones) * (1.0 / HALF)
        y = d * lax.rsqrt(var + RWKV_LN_EPS) * lnw_ref[...] + lnb_ref[...]
        bonus = _segsum(r * k * rk_ref[...], ones) * v
        o_ref[sl, :] = ((y + bonus) * g_ref[sl, :]).astype(o_ref.dtype)
        return carry

    lax.fori_loop(0, nchunk, chunk, 0)


def _rwkv_scan(r, lw, k, v, nk, kb, g, r_k, ln_w, ln_b, bsz, seq, tc=512):
    tc = min(tc, seq)
    ns = seq // tc
    npair = GROUP // LANE
    blk = pl.BlockSpec((tc, LANE), lambda b, p, i: (b * ns + i, p))
    vec = pl.BlockSpec((1, LANE), lambda b, p, i: (0, p))
    return pl.pallas_call(
        _rwkv_body,
        out_shape=jax.ShapeDtypeStruct((bsz * seq, GROUP), BF16),
        grid=(bsz, npair, ns),
        in_specs=[blk] * 7 + [vec] * 3,
        out_specs=blk,
        scratch_shapes=[pltpu.VMEM((LANE, LANE), F32)],
        compiler_params=_cparams(("parallel", "parallel", "arbitrary")),
        name="rwkv_scan",
    )(r, lw, k, v, nk, kb, g, r_k.reshape(1, GROUP), ln_w.reshape(1, GROUP), ln_b.reshape(1, GROUP))


def _pad_rows(w, start):
    return jnp.zeros((SMALL, w.shape[1]), F32).at[start:start + w.shape[0]].set(w)


def _layer(x, c8, layer_idx, v_first, vres, bias_ad, bias_as, bias_d, p, attn_tile):
    bsz, seq, d = x.shape
    t = bsz * seq
    mod = _ada(c8, p["ada_w"], p["ada_b"], layer_idx)
    mod3 = mod[:bsz].reshape(bsz * 6, 1, d)
    h = _norm_mod(x, mod3, 1, 0)

    w_in = p["w_in"]
    o_b, o_c, o_d = 3072, 3072 + 3088, 3072 + 3088 + 3264
    vres_cols = vres[0] if vres is not None else jnp.zeros((d, 32), F32)
    w_r = jnp.concatenate([
        w_in[:, :o_b], w_in[:, o_b:o_b + 3072], w_in[:, o_c:o_c + 3072], w_in[:, o_d:],
        w_in[:, o_b + 3072:o_c], w_in[:, o_c + 3072:o_d], vres_cols,
        jnp.zeros((d, SMALL - S_VR - 32), F32)], axis=1).astype(BF16)
    cols = _matmul(h, w_r, F32)

    qkv = _prep_a(cols, p["diff_q_norm"], p["diff_k_norm"])
    o_a = _diff_attention(qkv, bias_ad, bias_as, p["diff_lambda"], p["diff_subln"], bsz, seq, layer_idx, attn_tile)
    o_dd = _swa_attention(cols, bias_d, p["swa_q_norm"], p["swa_k_norm"], p["swa_sinks"], bsz, seq)
    mu = p["rwkv_mu"]
    vres_mu = vres[1] if vres is not None else jnp.zeros((32,), F32)
    mu_small = jnp.concatenate([jnp.zeros((S_WD,), F32), mu[3 * GROUP:], vres_mu,
                                jnp.zeros((SMALL - S_VR - 32,), F32)]).reshape(1, SMALL)
    v_up = _pad_rows(vres[2], S_VR) if vres is not None else jnp.zeros((SMALL, GROUP), F32)
    v0 = vres[3] if vres is not None else jnp.zeros((GROUP,), F32)
    r_, lw_, k_, v_, nk_, kb_, g_, la_ = _prep_c(
        cols, seq, mu[:3 * GROUP].reshape(1, -1), mu_small,
        _pad_rows(p["rwkv_w_up"], S_WD), _pad_rows(p["rwkv_a_up"], S_AD), _pad_rows(p["rwkv_g_up"], S_GD),
        v_up, _pad_rows(p["gla_gate_up"], S_GLA),
        p["rwkv_w0"].reshape(1, -1), p["rwkv_a0"].reshape(1, -1), v0.reshape(1, -1),
        p["rwkv_k_k"].reshape(1, -1), p["rwkv_k_a"].reshape(1, -1), p["gla_gate_bias"].reshape(1, -1),
        v_first if vres is not None else None)
    if vres is None:
        v_first = v_
    o_bb = _gla(cols, la_, p["gla_out_norm"], bsz, seq)
    o_c = _rwkv_scan(r_, lw_, k_, v_, nk_, kb_, g_, p["rwkv_r_k"].reshape(-1), p["rwkv_ln_w"], p["rwkv_ln_b"],
                     bsz, seq)

    x = _out_proj([o_a, o_bb, o_c, o_dd], p["w_out"].astype(BF16), x, mod3, 2)
    h2 = _norm_mod(x, mod3, 4, 3)
    act = _ffn_up(h2, p["ffn_w1"].astype(BF16), p["ffn_w3"].astype(BF16))
    x = _ffn_down(act, p["ffn_w2"].astype(BF16), x, mod3, 5)
    return x, v_first


def kernel(x, c, rel_bias, ada_w, ada_b, w_in, w_out, diff_q_norm, diff_k_norm, diff_lambda, diff_subln,
           gla_gate_up, gla_gate_bias, gla_out_norm, rwkv_mu, rwkv_w_up, rwkv_w0, rwkv_a_up, rwkv_a0,
           rwkv_g_up, rwkv_k_k, rwkv_k_a, rwkv_r_k, rwkv_ln_w, rwkv_ln_b, rwkv_vres_down, rwkv_vres_mu,
           rwkv_vres_up, rwkv_v0, swa_q_norm, swa_k_norm, swa_sinks, ffn_w1, ffn_w3, ffn_w2):
    bsz, seq, _ = x.shape
    depth = ada_w.shape[0]
    attn_tile = min(512, seq)
    bias_ad = _bias_tiles(rel_bias[:, :DIFF_HEADS], attn_tile, attn_tile, 0, None, True)
    bias_as = _bias_tiles(rel_bias[:, :DIFF_HEADS], attn_tile, attn_tile, attn_tile, None, True)
    bias_d = _bias_tiles(rel_bias[:, DIFF_HEADS:], SWA_BLOCK, 2 * SWA_BLOCK, SWA_BLOCK, SWA_BLOCK, False)
    c8 = jnp.zeros((8, c.shape[1]), F32).at[:bsz].set(c)
    v_first = None
    for l in range(depth):
        p = dict(ada_w=ada_w, ada_b=ada_b, w_in=w_in[l], w_out=w_out[l],
                 diff_q_norm=diff_q_norm[l], diff_k_norm=diff_k_norm[l], diff_lambda=diff_lambda[l],
                 diff_subln=diff_subln[l], gla_gate_up=gla_gate_up[l], gla_gate_bias=gla_gate_bias[l],
                 gla_out_norm=gla_out_norm[l], rwkv_mu=rwkv_mu[l], rwkv_w_up=rwkv_w_up[l],
                 rwkv_w0=rwkv_w0[l], rwkv_a_up=rwkv_a_up[l], rwkv_a0=rwkv_a0[l], rwkv_g_up=rwkv_g_up[l],
                 rwkv_k_k=rwkv_k_k[l], rwkv_k_a=rwkv_k_a[l], rwkv_r_k=rwkv_r_k[l], rwkv_ln_w=rwkv_ln_w[l],
                 rwkv_ln_b=rwkv_ln_b[l], swa_q_norm=swa_q_norm[l], swa_k_norm=swa_k_norm[l],
                 swa_sinks=swa_sinks[l], ffn_w1=ffn_w1[l], ffn_w3=ffn_w3[l], ffn_w2=ffn_w2[l])
        vres = None if l == 0 else (rwkv_vres_down[l - 1], rwkv_vres_mu[l - 1], rwkv_vres_up[l - 1],
                                    rwkv_v0[l - 1])
        x, v_first = _layer(x, c8, l, v_first, vres, bias_ad, bias_as, bias_d, p, attn_tile)
    return x
```

```python
import functools
import math

import jax
import jax.numpy as jnp
from jax import lax
from jax.experimental import pallas as pl
from jax.experimental.pallas import tpu as pltpu

F32 = jnp.float32
BF16 = jnp.bfloat16
HI = lax.Precision.HIGHEST

D_MODEL = 4096
GROUP = 1024
D_FF = 11008
NORM_EPS = 1e-6
NEG_INF = -1e30
LANE = 128
HALF = 64

DIFF_HEADS = 8
DIFF_QK = 64
GLA_HEADS = 4
GLA_DK = 128
GLA_DV = 256
GLA_RANK = 16
GLA_NORMALIZER = 16.0
CHUNK = 64
SCAN_GROUP = 8
RWKV_LN_EPS = 64e-5
SWA_HEADS = 16
SWA_KV_HEADS = 2
SWA_BLOCK = 128
REL_BUCKETS = 32
REL_MAX_DIST = 128

OFF_A = 0
OFF_B = 3072
OFF_C = 6144
OFF_D = 9216
OFF_S = 10496
SMALL = 256
NCOLS = OFF_S + SMALL
S_GLA, S_WD, S_AD, S_GD, S_VR = 0, 16, 80, 144, 208

VMEM_LIMIT = 56 * 1024 * 1024


def _cparams(sem, vmem=VMEM_LIMIT):
    return pltpu.CompilerParams(dimension_semantics=sem, vmem_limit_bytes=vmem)


def _dot(a, b, prec=None):
    return jnp.dot(a, b, preferred_element_type=F32, precision=prec)


def _dot_nt(a, b, prec=None):
    return lax.dot_general(a, b, (((1,), (1,)), ((), ())), preferred_element_type=F32, precision=prec)


def _dot_tn(a, b, prec=None):
    return lax.dot_general(a, b, (((0,), (0,)), ((), ())), preferred_element_type=F32, precision=prec)


def _sigmoid(z):
    return 1.0 / (1.0 + jnp.exp(-z))


def _softplus(z):
    return jnp.maximum(z, 0.0) + jnp.log(1.0 + jnp.exp(-jnp.abs(z)))


def _seg_ones():
    r = lax.broadcasted_iota(jnp.int32, (LANE, LANE), 0) // HALF
    c = lax.broadcasted_iota(jnp.int32, (LANE, LANE), 1) // HALF
    return (r == c).astype(BF16)


def _split(x):
    hi = x.astype(BF16)
    return hi, (x - hi.astype(F32)).astype(BF16)


def _dot_left2(x, w):
    hi, lo = _split(x)
    return _dot(hi, w) + _dot(lo, w)


def _tri_cumsum(tri, x):
    hi, lo = _split(x)
    return _dot(tri, hi) + _dot(tri, lo)


def _segsum(x, ones):
    return _dot_left2(x, ones)


def _ada_body(c_ref, w_ref, b_ref, o_ref):
    c = c_ref[...]
    s = (c * _sigmoid(c)).astype(BF16)
    o_ref[...] = _dot(s, w_ref[...].astype(BF16)) + b_ref[...]


def _ada(c8, ada_w, ada_b, layer, tn=512):
    _, d, n = ada_w.shape
    return pl.pallas_call(
        _ada_body,
        out_shape=jax.ShapeDtypeStruct((8, n), F32),
        grid=(n // tn,),
        in_specs=[pl.BlockSpec((8, d), lambda j: (0, 0)),
                  pl.BlockSpec((None, d, tn), lambda j: (layer, 0, j)),
                  pl.BlockSpec((1, tn), lambda j: (0, j))],
        out_specs=pl.BlockSpec((8, tn), lambda j: (0, j)),
        compiler_params=_cparams(("parallel",)),
        name="ada_mod",
    )(c8, ada_w, ada_b[layer].reshape(1, n))


def _norm_body(x_ref, sc_ref, sh_ref, o_ref):
    x = x_ref[0]
    ms = jnp.mean(x * x, axis=-1, keepdims=True)
    h = x * lax.rsqrt(ms + NORM_EPS) * (1.0 + sc_ref[0]) + sh_ref[0]
    o_ref[...] = h.astype(o_ref.dtype)


def _norm_mod(x, mod3, sc_idx, sh_idx, ts=256):
    bsz, seq, d = x.shape
    ts = min(ts, seq)
    ns = seq // ts
    return pl.pallas_call(
        _norm_body,
        out_shape=jax.ShapeDtypeStruct((bsz * seq, d), BF16),
        grid=(bsz, ns),
        in_specs=[pl.BlockSpec((1, ts, d), lambda b, i: (b, i, 0)),
                  pl.BlockSpec((1, 1, d), lambda b, i: (b * 6 + sc_idx, 0, 0)),
                  pl.BlockSpec((1, 1, d), lambda b, i: (b * 6 + sh_idx, 0, 0))],
        out_specs=pl.BlockSpec((ts, d), lambda b, i: (b * ns + i, 0)),
        compiler_params=_cparams(("parallel", "parallel")),
        name="norm_mod",
    )(x, mod3, mod3)


def _mm_body(a_ref, b_ref, o_ref):
    o_ref[...] = _dot(a_ref[...], b_ref[...]).astype(o_ref.dtype)


def _matmul(a, b, out_dtype, tm=1024, tn=512):
    m, k = a.shape
    n = b.shape[1]
    tm = min(tm, m)
    return pl.pallas_call(
        _mm_body,
        out_shape=jax.ShapeDtypeStruct((m, n), out_dtype),
        grid=(m // tm, n // tn),
        in_specs=[pl.BlockSpec((tm, k), lambda i, j: (i, 0)),
                  pl.BlockSpec((k, tn), lambda i, j: (0, j))],
        out_specs=pl.BlockSpec((tm, tn), lambda i, j: (i, j)),
        compiler_params=_cparams(("parallel", "parallel")),
        name="in_proj",
    )(a, b)


def _outproj_body(a0, a1, a2, a3, w_ref, x_ref, g_ref, o_ref):
    acc = _dot(a0[...], w_ref[0 * GROUP:1 * GROUP, :])
    acc += _dot(a1[...], w_ref[1 * GROUP:2 * GROUP, :])
    acc += _dot(a2[...], w_ref[2 * GROUP:3 * GROUP, :])
    acc += _dot(a3[...], w_ref[3 * GROUP:4 * GROUP, :])
    o_ref[0] = x_ref[0] + g_ref[0] * acc


def _out_proj(parts, w, x, mod3, gate_idx, tm=1024, tn=512):
    bsz, seq, d = x.shape
    tm = min(tm, seq)
    ns = seq // tm
    a_spec = pl.BlockSpec((tm, GROUP), lambda b, i, j: (b * ns + i, 0))
    return pl.pallas_call(
        _outproj_body,
        out_shape=jax.ShapeDtypeStruct((bsz, seq, d), F32),
        grid=(bsz, ns, d // tn),
        in_specs=[a_spec, a_spec, a_spec, a_spec,
                  pl.BlockSpec((4 * GROUP, tn), lambda b, i, j: (0, j)),
                  pl.BlockSpec((1, tm, tn), lambda b, i, j: (b, i, j)),
                  pl.BlockSpec((1, 1, tn), lambda b, i, j: (b * 6 + gate_idx, 0, j))],
        out_specs=pl.BlockSpec((1, tm, tn), lambda b, i, j: (b, i, j)),
        compiler_params=_cparams(("parallel", "parallel", "parallel")),
        name="out_proj",
    )(*parts, w, x, mod3)


def _ffn_up_body(a_ref, w1_ref, w3_ref, o_ref):
    a = a_ref[...]
    u = _dot(a, w1_ref[...].astype(BF16))
    v = _dot(a, w3_ref[...].astype(BF16))
    o_ref[...] = (u * _sigmoid(u) * v).astype(o_ref.dtype)


def _ffn_up(h, w1, w3, layer, tm=1024, tn=256):
    m, k = h.shape
    n = w1.shape[2]
    tm = min(tm, m)
    return pl.pallas_call(
        _ffn_up_body,
        out_shape=jax.ShapeDtypeStruct((m, n), BF16),
        grid=(m // tm, n // tn),
        in_specs=[pl.BlockSpec((tm, k), lambda i, j: (i, 0)),
                  pl.BlockSpec((None, k, tn), lambda i, j: (layer, 0, j)),
                  pl.BlockSpec((None, k, tn), lambda i, j: (layer, 0, j))],
        out_specs=pl.BlockSpec((tm, tn), lambda i, j: (i, j)),
        compiler_params=_cparams(("parallel", "parallel")),
        name="ffn_up",
    )(h, w1, w3)


def _ffn_down_body(a_ref, w_ref, x_ref, g_ref, o_ref):
    o_ref[0] = x_ref[0] + g_ref[0] * _dot(a_ref[...], w_ref[...])


def _ffn_down(a, w, x, mod3, gate_idx, tm=512, tn=256):
    bsz, seq, d = x.shape
    k = a.shape[1]
    tm = min(tm, seq)
    ns = seq // tm
    return pl.pallas_call(
        _ffn_down_body,
        out_shape=jax.ShapeDtypeStruct((bsz, seq, d), F32),
        grid=(bsz, ns, d // tn),
        in_specs=[pl.BlockSpec((tm, k), lambda b, i, j: (b * ns + i, 0)),
                  pl.BlockSpec((k, tn), lambda b, i, j: (0, j)),
                  pl.BlockSpec((1, tm, tn), lambda b, i, j: (b, i, j)),
                  pl.BlockSpec((1, 1, tn), lambda b, i, j: (b * 6 + gate_idx, 0, j))],
        out_specs=pl.BlockSpec((1, tm, tn), lambda b, i, j: (b, i, j)),
        compiler_params=_cparams(("parallel", "parallel", "parallel")),
        name="ffn_down",
    )(a, w, x, mod3)


def _bias_body(tab_ref, o_ref, *, off, window, shift, key_major, scale):
    h = pl.program_id(0)
    rows, cols = o_ref.shape[1], o_ref.shape[2]
    qi = lax.broadcasted_iota(jnp.int32, (rows, cols), 1 if key_major else 0)
    kj = lax.broadcasted_iota(jnp.int32, (rows, cols), 0 if key_major else 1)
    dist = off + qi - kj
    max_exact = REL_BUCKETS // 2
    n = jnp.maximum(dist, 0)
    nf = jnp.maximum(n, 1).astype(F32)
    large = max_exact + (jnp.log(nf / max_exact) / math.log(REL_MAX_DIST / max_exact)
                         * (REL_BUCKETS - max_exact)).astype(jnp.int32)
    large = jnp.minimum(large, REL_BUCKETS - 1)
    bucket = jnp.where(n < max_exact, n, large)
    last = tab_ref[REL_BUCKETS - 1, h]
    bias = jnp.full((rows, cols), last, F32)
    for b in range(REL_BUCKETS - 1):
        bias = jnp.where(bucket == b, tab_ref[b, h], bias)
    if shift:
        bias = bias - last
    if scale != 1.0:
        bias = bias * scale
    valid = dist >= 0
    if window is not None:
        valid = jnp.logical_and(valid, dist < window)
    o_ref[0] = jnp.where(valid, bias, NEG_INF)


def _bias_tiles(table, rows, cols, off, window, shift, key_major=False, scale=1.0):
    nh = table.shape[1]
    return pl.pallas_call(
        functools.partial(_bias_body, off=off, window=window, shift=shift, key_major=key_major, scale=scale),
        out_shape=jax.ShapeDtypeStruct((nh, rows, cols), F32),
        grid=(nh,),
        in_specs=[pl.BlockSpec(memory_space=pltpu.SMEM)],
        out_specs=pl.BlockSpec((1, rows, cols), lambda h: (h, 0, 0)),
        compiler_params=_cparams(("parallel",)),
        name="rel_bias_tiles",
    )(table)


LOG2E = 1.4426950408889634
VT_ROWS = LANE + 16
Q_CHUNK = 256


def _prep_a_body(c_ref, qg_ref, kg_ref, qk_ref, vt_ref):
    ones = _seg_ones()
    scale = DIFF_QK ** -0.5 * LOG2E
    tm = c_ref.shape[0]
    for j in range(2 * DIFF_HEADS):
        x = c_ref[:, j * LANE:(j + 1) * LANE]
        ms = _segsum(x * x, ones) * (1.0 / DIFF_QK)
        gain = qg_ref[...] * scale if j < DIFF_HEADS else kg_ref[...]
        qk_ref[:, j * LANE:(j + 1) * LANE] = (x * lax.rsqrt(ms + NORM_EPS) * gain).astype(qk_ref.dtype)
    for h in range(DIFF_HEADS):
        v = c_ref[:, 2 * GROUP + h * LANE:2 * GROUP + (h + 1) * LANE]
        vt_ref[h * VT_ROWS:h * VT_ROWS + LANE, :] = v.T.astype(vt_ref.dtype)
        vt_ref[h * VT_ROWS + LANE:(h + 1) * VT_ROWS, :] = jnp.ones((VT_ROWS - LANE, tm), vt_ref.dtype)


def _prep_a(cols, q_gain, k_gain, tm=256):
    t = cols.shape[0]
    tm = min(tm, t)
    qg = jnp.tile(q_gain, 2).reshape(1, LANE)
    kg = jnp.tile(k_gain, 2).reshape(1, LANE)
    vec = pl.BlockSpec((1, LANE), lambda i: (0, 0))
    return pl.pallas_call(
        _prep_a_body,
        out_shape=[jax.ShapeDtypeStruct((t, 2 * GROUP), BF16),
                   jax.ShapeDtypeStruct((DIFF_HEADS * VT_ROWS, t), BF16)],
        grid=(t // tm,),
        in_specs=[pl.BlockSpec((tm, 3 * GROUP), lambda i: (i, OFF_A // (3 * GROUP))), vec, vec],
        out_specs=[pl.BlockSpec((tm, 2 * GROUP), lambda i: (i, 0)),
                   pl.BlockSpec((DIFF_HEADS * VT_ROWS, tm), lambda i: (0, i))],
        compiler_params=_cparams(("parallel",)),
        name="diff_prep",
    )(cols, qg, kg)


def _flash_body(it_ref, jt_ref, q_ref, k_ref, vt_ref, bd_ref, bs_ref, lam_ref, sub_ref, o_ref,
                qlo, qhi, m1, a1, m2, a2, *, lam_init):
    t = pl.program_id(2)
    i = it_ref[t]
    j = jt_ref[t]

    @pl.when(j == 0)
    def _():
        q = q_ref[...]
        lane = lax.broadcasted_iota(jnp.int32, q.shape, 1)
        zero = jnp.zeros_like(q)
        qlo[...] = jnp.where(lane < HALF, q, zero)
        qhi[...] = jnp.where(lane >= HALF, q, zero)
        for m, a in ((m1, a1), (m2, a2)):
            m[...] = jnp.full(m.shape, NEG_INF, F32)
            a[...] = jnp.zeros(a.shape, F32)

    def step(bias):
        k = k_ref[...]
        vt = vt_ref[...]
        scores = [_dot_nt(k, qq[...]) for qq in (qlo, qhi)]
        for s, m, a in ((scores[0], m1, a1), (scores[1], m2, a2)):
            if bias is not None:
                s = s + bias
            m_old = m[...]
            m_new = jnp.maximum(m_old, jnp.max(s, axis=0, keepdims=True))
            p = jnp.exp2(s - m_new).astype(BF16)
            a[...] = jnp.exp2(m_old - m_new) * a[...] + _dot(vt, p)
            m[...] = m_new

    @pl.when(j < i - 1)
    def _():
        step(None)

    @pl.when(j == i - 1)
    def _():
        step(bs_ref[0])

    @pl.when(j == i)
    def _():
        step(bd_ref[0])
        lam = lam_ref[...]
        e1 = jnp.exp(jnp.sum(lam[0:1] * lam[1:2], axis=-1, keepdims=True))
        e2 = jnp.exp(jnp.sum(lam[2:3] * lam[3:4], axis=-1, keepdims=True))
        lam_full = e1 - e2 + lam_init
        o = a1[:LANE, :] / a1[LANE:LANE + 1, :] - lam_full * (a2[:LANE, :] / a2[LANE:LANE + 1, :])
        ms = jnp.mean(o * o, axis=0, keepdims=True)
        o = o * lax.rsqrt(ms + NORM_EPS) * sub_ref[...] * (1.0 - lam_init)
        o_ref[...] = o.T.astype(o_ref.dtype)


def _diff_attention(qk, vt, bias_diag, bias_sub, lam, subln, bsz, seq, layer_idx, tile):
    nq = seq // tile
    lam_init = 0.8 - 0.6 * math.exp(-0.3 * layer_idx)
    pairs = [(i, j) for i in range(nq) for j in range(i + 1)]
    i_tab = jnp.array([ij[0] for ij in pairs], jnp.int32)
    j_tab = jnp.array([ij[1] for ij in pairs], jnp.int32)
    grid_spec = pltpu.PrefetchScalarGridSpec(
        num_scalar_prefetch=2,
        grid=(bsz, DIFF_HEADS, len(pairs)),
        in_specs=[pl.BlockSpec((tile, LANE), lambda b, h, t, it, jt: (b * nq + it[t], h)),
                  pl.BlockSpec((tile, LANE), lambda b, h, t, it, jt: (b * nq + jt[t], DIFF_HEADS + h)),
                  pl.BlockSpec((VT_ROWS, tile), lambda b, h, t, it, jt: (h, b * nq + jt[t])),
                  pl.BlockSpec((1, tile, tile), lambda b, h, t, it, jt: (h, 0, 0)),
                  pl.BlockSpec((1, tile, tile), lambda b, h, t, it, jt: (h, 0, 0)),
                  pl.BlockSpec((4, DIFF_QK), lambda b, h, t, it, jt: (0, 0)),
                  pl.BlockSpec((LANE, 1), lambda b, h, t, it, jt: (0, 0))],
        out_specs=pl.BlockSpec((tile, LANE), lambda b, h, t, it, jt: (b * nq + it[t], h)),
        scratch_shapes=[pltpu.VMEM((tile, LANE), BF16), pltpu.VMEM((tile, LANE), BF16),
                        pltpu.VMEM((1, tile), F32), pltpu.VMEM((VT_ROWS, tile), F32),
                        pltpu.VMEM((1, tile), F32), pltpu.VMEM((VT_ROWS, tile), F32)])
    return pl.pallas_call(
        functools.partial(_flash_body, lam_init=lam_init),
        out_shape=jax.ShapeDtypeStruct((bsz * seq, GROUP), BF16),
        grid_spec=grid_spec,
        compiler_params=_cparams(("parallel", "parallel", "arbitrary")),
        name="diff_attention",
    )(i_tab, j_tab, qk, qk, vt, bias_diag, bias_sub, lam, subln.reshape(LANE, 1))


def _swa_body(q_ref, kp_ref, kc_ref, vp_ref, vc_ref, bias_ref, qg_ref, kg_ref, sink_ref, o_ref):
    n = pl.program_id(1)
    g = pl.program_id(2)
    ones = _seg_ones()
    lane = lax.broadcasted_iota(jnp.int32, (SWA_BLOCK, LANE), 1)
    lo = lane < HALF

    def norm(x, gain):
        ms = _segsum(x * x, ones) * (1.0 / HALF)
        return x * lax.rsqrt(ms + NORM_EPS) * gain

    k = jnp.concatenate([norm(kp_ref[...], kg_ref[...]), norm(kc_ref[...], kg_ref[...])], axis=0)
    v = jnp.concatenate([vp_ref[...], vc_ref[...]], axis=0)
    k_sw = pltpu.roll(k, HALF, axis=1)
    v_sw = pltpu.roll(v, HALF, axis=1)
    first_head = g == 0
    lo2 = lax.broadcasted_iota(jnp.int32, (2 * SWA_BLOCK, LANE), 1) < HALF
    k_a = jnp.where(first_head, k, k_sw).astype(BF16)
    k_b = jnp.where(first_head, k_sw, k).astype(BF16)
    v_mine_lo = jnp.where(first_head, v, v_sw)
    v_mine_hi = jnp.where(first_head, v_sw, v)
    v_a = jnp.where(lo2, v_mine_lo, 0.0).astype(BF16)
    v_b = jnp.where(lo2, 0.0, v_mine_hi).astype(BF16)
    kcol = lax.broadcasted_iota(jnp.int32, (SWA_BLOCK, 2 * SWA_BLOCK), 1)
    pad = jnp.logical_and(n == 0, kcol < SWA_BLOCK)
    scale = HALF ** -0.5
    grp = SWA_HEADS // SWA_KV_HEADS
    heads = range(grp)
    q = [norm(q_ref[:, pr * LANE:(pr + 1) * LANE], qg_ref[...] * scale) for pr in range(grp // 2)]
    qm = [jnp.where(lo if hh % 2 == 0 else jnp.logical_not(lo), q[hh // 2], 0.0).astype(BF16) for hh in heads]
    s = [_dot_nt(qm[hh], k_a if hh % 2 == 0 else k_b) for hh in heads]
    s = [jnp.where(pad, NEG_INF, s[hh] + bias_ref[hh]) for hh in heads]
    sink = [sink_ref[g * grp + hh] for hh in heads]
    m = [jnp.maximum(jnp.max(s[hh], axis=-1, keepdims=True), sink[hh]) for hh in heads]
    p = [jnp.exp(s[hh] - m[hh]) for hh in heads]
    den = [jnp.sum(p[hh], axis=-1, keepdims=True) + jnp.exp(sink[hh] - m[hh]) for hh in heads]
    pv = [_dot(p[hh].astype(BF16), v_a if hh % 2 == 0 else v_b) for hh in heads]
    for pr in range(grp // 2):
        o = pv[2 * pr] / den[2 * pr] + pv[2 * pr + 1] / den[2 * pr + 1]
        o_ref[:, pr * LANE:(pr + 1) * LANE] = o.astype(o_ref.dtype)


def _swa_attention(cols, bias, q_gain, k_gain, sinks, bsz, seq):
    nb = seq // SWA_BLOCK
    grp = SWA_HEADS // SWA_KV_HEADS
    qw = grp * HALF
    kcol = (OFF_D + GROUP) // LANE
    vcol = kcol + 1
    prev = lambda b, n, g: b * nb + jnp.maximum(n - 1, 0)
    cur = lambda b, n, g: b * nb + n
    vec = pl.BlockSpec((1, LANE), lambda b, n, g: (0, 0))
    return pl.pallas_call(
        _swa_body,
        out_shape=jax.ShapeDtypeStruct((bsz * seq, GROUP), BF16),
        grid=(bsz, nb, SWA_KV_HEADS),
        in_specs=[pl.BlockSpec((SWA_BLOCK, qw), lambda b, n, g: (cur(b, n, g), OFF_D // qw + g)),
                  pl.BlockSpec((SWA_BLOCK, LANE), lambda b, n, g: (prev(b, n, g), kcol)),
                  pl.BlockSpec((SWA_BLOCK, LANE), lambda b, n, g: (cur(b, n, g), kcol)),
                  pl.BlockSpec((SWA_BLOCK, LANE), lambda b, n, g: (prev(b, n, g), vcol)),
                  pl.BlockSpec((SWA_BLOCK, LANE), lambda b, n, g: (cur(b, n, g), vcol)),
                  pl.BlockSpec((grp, SWA_BLOCK, 2 * SWA_BLOCK), lambda b, n, g: (g, 0, 0)),
                  vec, vec,
                  pl.BlockSpec(memory_space=pltpu.SMEM)],
        out_specs=pl.BlockSpec((SWA_BLOCK, qw), lambda b, n, g: (cur(b, n, g), g)),
        compiler_params=_cparams(("parallel", "parallel", "parallel")),
        name="swa_attention",
    )(cols, cols, cols, cols, cols, bias,
      jnp.tile(q_gain, 2).reshape(1, LANE), jnp.tile(k_gain, 2).reshape(1, LANE), sinks)


def _prep_c_body(*refs, seq, with_vres):
    (cm_ref, cs_ref, pm_ref, ps_ref, mum_ref, mus_ref, wup_ref, aup_ref, gup_ref, vup_ref, glaup_ref,
     w0_ref, a0_ref, v0_ref, kk_ref, ka_ref, glab_ref) = refs[:17]
    rest = refs[17:]
    if with_vres:
        vf_ref, rest = rest[0], rest[1:]
    r_ref, lw_ref, k_ref, v_ref, nk_ref, kb_ref, g_ref, la_ref = rest
    tm = cm_ref.shape[0]
    i = pl.program_id(0)
    seq_start = (i * tm) % seq == 0

    def shifted(cur, prev_rows, mu):
        row = lax.broadcasted_iota(jnp.int32, cur.shape, 0)
        before = jnp.where(seq_start, 0.0, prev_rows[7:8, :])
        prev = jnp.where(row == 0, before, pltpu.roll(cur, 1, axis=0))
        return cur + (prev - cur) * mu

    def low_rank(x, w2_ref):
        hi, lo = _split(x)
        return _dot(hi, w2_ref[0]) + _dot(lo, w2_ref[0]) + _dot(hi, w2_ref[1])

    cs = cs_ref[...]
    la_ref[...] = -_softplus(-(low_rank(cs, glaup_ref) + glab_ref[...])) * (1.0 / GLA_NORMALIZER)
    sm = shifted(cm_ref[...], pm_ref[...], mum_ref[...])
    ss = shifted(cs, ps_ref[...], mus_ref[...])
    r = sm[:, :GROUP]
    k = sm[:, GROUP:2 * GROUP]
    v = sm[:, 2 * GROUP:]
    if with_vres:
        gate = _sigmoid(v0_ref[...] + low_rank(ss, vup_ref))
        v = v + (vf_ref[...] - v) * gate
    w_log = -_softplus(-(w0_ref[...] + low_rank(jnp.tanh(ss), wup_ref))) - 0.5
    a = _sigmoid(a0_ref[...] + low_rank(ss, aup_ref))
    r_ref[...] = r.astype(r_ref.dtype)
    lw_ref[...] = -jnp.exp(w_log)
    v_ref[...] = v
    g_ref[...] = low_rank(_sigmoid(ss), gup_ref).astype(g_ref.dtype)
    k_ref[...] = (k * (1.0 + (a - 1.0) * ka_ref[...])).astype(k_ref.dtype)
    ones = _seg_ones()
    kk = k * kk_ref[...]
    for j in range(GROUP // LANE):
        sl = slice(j * LANE, (j + 1) * LANE)
        x = kk[:, sl]
        nrm = jnp.maximum(jnp.sqrt(_segsum(x * x, ones)), 1e-12)
        x = x / nrm
        nk_ref[:, sl] = x.astype(nk_ref.dtype)
        kb_ref[:, sl] = (x * a[:, sl]).astype(kb_ref.dtype)


def _prep_c(cols, seq, mu_main, mu_small, w_up, a_up, g_up, v_up, gla_up, w0, a0, v0, k_k, k_a, gla_bias,
            v_first, tm=256):
    t = cols.shape[0]
    tm = min(tm, seq)
    with_vres = v_first is not None
    cmain = OFF_C // (3 * GROUP)
    csmall = OFF_S // SMALL
    prev8 = lambda i: jnp.maximum(i * (tm // 8) - 1, 0)
    full = lambda r, c: pl.BlockSpec((r, c), lambda i: (0, 0))
    hilo = lambda c: pl.BlockSpec((2, SMALL, c), lambda i: (0, 0, 0))
    in_specs = [pl.BlockSpec((tm, 3 * GROUP), lambda i: (i, cmain)),
                pl.BlockSpec((tm, SMALL), lambda i: (i, csmall)),
                pl.BlockSpec((8, 3 * GROUP), lambda i: (prev8(i), cmain)),
                pl.BlockSpec((8, SMALL), lambda i: (prev8(i), csmall)),
                full(1, 3 * GROUP), full(1, SMALL),
                hilo(GROUP), hilo(GROUP), hilo(GROUP), hilo(GROUP), hilo(GLA_HEADS * GLA_DK),
                full(1, GROUP), full(1, GROUP), full(1, GROUP), full(1, GROUP), full(1, GROUP),
                full(1, GLA_HEADS * GLA_DK)]
    args = [cols, cols, cols, cols, mu_main, mu_small, w_up, a_up, g_up, v_up, gla_up,
            w0, a0, v0, k_k, k_a, gla_bias]
    if with_vres:
        in_specs.append(pl.BlockSpec((tm, GROUP), lambda i: (i, 0)))
        args.append(v_first)
    row = pl.BlockSpec((tm, GROUP), lambda i: (i, 0))
    out = [jax.ShapeDtypeStruct((t, GROUP), dt) for dt in (BF16, F32, BF16, F32, BF16, BF16, BF16)]
    return pl.pallas_call(
        functools.partial(_prep_c_body, seq=seq, with_vres=with_vres),
        out_shape=out + [jax.ShapeDtypeStruct((t, GLA_HEADS * GLA_DK), F32)],
        grid=(t // tm,),
        in_specs=in_specs,
        out_specs=[row] * 7 + [pl.BlockSpec((tm, GLA_HEADS * GLA_DK), lambda i: (i, 0))],
        compiler_params=_cparams(("parallel",)),
        name="rwkv_gla_prep",
    )(*args)


def _gla_body(q_ref, k_ref, v_ref, g_ref, la_ref, gain_ref, o_ref, st_ref):
    @pl.when(pl.program_id(2) == 0)
    def _():
        st_ref[...] = jnp.zeros(st_ref.shape, F32)

    row = lax.broadcasted_iota(jnp.int32, (CHUNK, CHUNK), 0)
    col = lax.broadcasted_iota(jnp.int32, (CHUNK, CHUNK), 1)
    causal = row >= col
    tri = causal.astype(BF16)
    nchunk = q_ref.shape[0] // CHUNK

    def local(sls):
        n = range(len(sls))
        k = [k_ref[sl, :] for sl in sls]
        v = [v_ref[sl, :].astype(BF16) for sl in sls]
        b = [_tri_cumsum(tri, la_ref[sl, :]) for sl in sls]
        b_last = [b[i][CHUNK - 1:CHUNK, :] for i in n]
        q_dec = [(q_ref[sl, :] * (GLA_DK ** -0.5) * jnp.exp(b[i])).astype(BF16) for i, sl in enumerate(sls)]
        a_intra = [jnp.where(causal, _dot_nt(q_dec[i], (k[i] * jnp.exp(-b[i])).astype(BF16)), 0.0) for i in n]
        upd = [_dot_tn(v[i], (k[i] * jnp.exp(b_last[i] - b[i])).astype(BF16)) for i in n]
        o_intra = [_dot(a_intra[i].astype(BF16), v[i]) for i in n]
        return [(q_dec[i], o_intra[i], upd[i], jnp.exp(b_last[i])) for i in n]

    def advance(sl, q_dec, o_intra, upd, dec):
        state = st_ref[...]
        s_hi, s_lo = _split(state)
        o = o_intra + _dot_nt(q_dec, s_hi) + _dot_nt(q_dec, s_lo)
        st_ref[...] = state * dec + upd
        ms = jnp.mean(o * o, axis=-1, keepdims=True)
        o = o * lax.rsqrt(ms + NORM_EPS) * gain_ref[...]
        gate = g_ref[sl, :]
        o_ref[sl, :] = (o * (gate * _sigmoid(gate))).astype(o_ref.dtype)

    def group(gi, carry):
        sls = [pl.ds(pl.multiple_of((gi * SCAN_GROUP + g) * CHUNK, CHUNK), CHUNK) for g in range(SCAN_GROUP)]
        for sl, part in zip(sls, local(sls)):
            advance(sl, *part)
        return carry

    lax.fori_loop(0, nchunk // SCAN_GROUP, group, 0)


def _gla(cols, log_a, out_gain, bsz, seq, tc=512):
    tc = min(tc, seq)
    ns = seq // tc
    qc = OFF_B // GLA_DK
    kc = qc + GLA_HEADS
    vc = (OFF_B + 2 * GLA_HEADS * GLA_DK) // GLA_DV
    gc = vc + GLA_HEADS
    rows = lambda b, h, i: b * ns + i
    return pl.pallas_call(
        _gla_body,
        out_shape=jax.ShapeDtypeStruct((bsz * seq, GROUP), BF16),
        grid=(bsz, GLA_HEADS, ns),
        in_specs=[pl.BlockSpec((tc, GLA_DK), lambda b, h, i: (rows(b, h, i), qc + h)),
                  pl.BlockSpec((tc, GLA_DK), lambda b, h, i: (rows(b, h, i), kc + h)),
                  pl.BlockSpec((tc, GLA_DV), lambda b, h, i: (rows(b, h, i), vc + h)),
                  pl.BlockSpec((tc, GLA_DV), lambda b, h, i: (rows(b, h, i), gc + h)),
                  pl.BlockSpec((tc, GLA_DK), lambda b, h, i: (rows(b, h, i), h)),
                  pl.BlockSpec((1, GLA_DV), lambda b, h, i: (0, 0))],
        out_specs=pl.BlockSpec((tc, GLA_DV), lambda b, h, i: (rows(b, h, i), h)),
        scratch_shapes=[pltpu.VMEM((GLA_DV, GLA_DK), F32)],
        compiler_params=_cparams(("parallel", "parallel", "arbitrary")),
        name="gla_scan",
    )(cols, cols, cols, cols, log_a, out_gain.reshape(1, GLA_DV))


def _rwkv_body(r_ref, lw_ref, k_ref, v_ref, nk_ref, kb_ref, g_ref, rk_ref, lnw_ref, lnb_ref, o_ref, st_ref):
    @pl.when(pl.program_id(2) == 0)
    def _():
        st_ref[...] = jnp.zeros(st_ref.shape, F32)

    two = 2 * CHUNK
    row = lax.broadcasted_iota(jnp.int32, (two, two), 0)
    col = lax.broadcasted_iota(jnp.int32, (two, two), 1)
    same = (row // CHUNK) == (col // CHUNK)
    strict = jnp.logical_and(same, (row % CHUNK) > (col % CHUNK))
    incl = jnp.logical_and(same, (row % CHUNK) >= (col % CHUNK))
    eye = (row == col).astype(F32)
    crow = lax.broadcasted_iota(jnp.int32, (CHUNK, CHUNK), 0)
    ccol = lax.broadcasted_iota(jnp.int32, (CHUNK, CHUNK), 1)
    tri = (crow >= ccol).astype(BF16)
    lo = lax.broadcasted_iota(jnp.int32, (CHUNK, LANE), 1) < HALF
    ones = _seg_ones()
    nchunk = r_ref.shape[0] // CHUNK

    def stack(x):
        return jnp.concatenate([jnp.where(lo, x, 0.0), jnp.where(lo, 0.0, x)], axis=0)

    def fold(x):
        return x[:CHUNK, :] + x[CHUNK:, :]

    def transitions(sls):
        n = range(len(sls))
        r = [r_ref[sl, :].astype(F32) for sl in sls]
        lw = [lw_ref[sl, :] for sl in sls]
        k = [k_ref[sl, :].astype(F32) for sl in sls]
        v = [v_ref[sl, :] for sl in sls]
        cum = [_tri_cumsum(tri, lw[i]) for i in n]
        last = [cum[i][CHUNK - 1:CHUNK, :] for i in n]
        a_bf, r_st, bk_st, bkh_st, v_st = [], [], [], [], []
        for i, sl in enumerate(sls):
            kb = kb_ref[sl, :].astype(F32)
            e_neg = jnp.exp(-cum[i])
            e_rem = jnp.exp(last[i] - cum[i])
            a_bf.append(stack(-nk_ref[sl, :].astype(F32) * jnp.exp(cum[i] - lw[i])).astype(BF16))
            r_st.append(stack(r[i] * jnp.exp(cum[i])))
            bk_st.append(jnp.concatenate([stack(kb * e_neg), stack(k[i] * e_neg)], axis=0).astype(BF16))
            bkh_st.append(jnp.concatenate([stack(kb * e_rem), stack(k[i] * e_rem)], axis=0).astype(BF16))
            v_st.append(stack(v[i]).astype(BF16))
        sc = [_dot_nt(jnp.concatenate([a_bf[i], r_st[i].astype(BF16)], axis=0), bk_st[i]) for i in n]
        a_ab = [jnp.where(strict, sc[i][:two, :two], 0.0) for i in n]
        a_ak = [jnp.where(strict, sc[i][:two, two:], 0.0).astype(BF16) for i in n]
        r_b = [jnp.where(incl, sc[i][two:, :two], 0.0).astype(BF16) for i in n]
        r_k = [jnp.where(incl, sc[i][two:, two:], 0.0).astype(BF16) for i in n]
        akv = [_dot(a_ak[i], v_st[i]).astype(BF16) for i in n]
        rkv = [_dot(r_k[i], v_st[i]) for i in n]
        inv = [eye + a_ab[i] for i in n]
        pw = [a_ab[i].astype(BF16) for i in n]
        for _ in range(5):
            pw = [_dot(pw[i], pw[i]).astype(BF16) for i in n]
            inv = [inv[i] + _dot(inv[i].astype(BF16), pw[i]) for i in n]
        pu = [_dot(inv[i].astype(BF16), jnp.concatenate([a_bf[i], akv[i]], axis=1)).astype(BF16) for i in n]
        rb_pu = [_dot(r_b[i], pu[i]) for i in n]
        m = [_dot_tn(pu[i][:, :LANE], bkh_st[i][:two, :]).astype(BF16) for i in n]
        n0 = [_dot_tn(jnp.concatenate([pu[i][:, LANE:], v_st[i]], axis=0), bkh_st[i]) for i in n]
        rkr = [_segsum(r[i] * k[i] * rk_ref[...], ones) for i in n]
        return [(fold(r_st[i] + rb_pu[i][:, :LANE]).astype(BF16), fold(rb_pu[i][:, LANE:] + rkv[i]),
                 m[i], n0[i], jnp.exp(last[i]), rkr[i] * v[i]) for i in n]

    def advance(sl, p2, y0, m, n0, dec, bonus):
        state = st_ref[...]
        s_hi, s_lo = _split(state)
        y = _dot_nt(p2, s_hi) + y0
        st_ref[...] = state * dec + _dot(s_hi, m) + _dot(s_lo, m) + n0
        d = y - _segsum(y, ones) * (1.0 / HALF)
        var = _segsum(d * d, ones) * (1.0 / HALF)
        y = d * lax.rsqrt(var + RWKV_LN_EPS) * lnw_ref[...] + lnb_ref[...]
        o_ref[sl, :] = ((y + bonus) * g_ref[sl, :]).astype(o_ref.dtype)

    def group(gi, carry):
        sls = [pl.ds(pl.multiple_of((gi * SCAN_GROUP + g) * CHUNK, CHUNK), CHUNK) for g in range(SCAN_GROUP)]
        for sl, part in zip(sls, transitions(sls)):
            advance(sl, *part)
        return carry

    lax.fori_loop(0, nchunk // SCAN_GROUP, group, 0)


def _rwkv_scan(r, lw, k, v, nk, kb, g, r_k, ln_w, ln_b, bsz, seq, tc=512):
    tc = min(tc, seq)
    ns = seq // tc
    npair = GROUP // LANE
    blk = pl.BlockSpec((tc, LANE), lambda b, p, i: (b * ns + i, p))
    vec = pl.BlockSpec((1, LANE), lambda b, p, i: (0, p))
    return pl.pallas_call(
        _rwkv_body,
        out_shape=jax.ShapeDtypeStruct((bsz * seq, GROUP), BF16),
        grid=(bsz, npair, ns),
        in_specs=[blk] * 7 + [vec] * 3,
        out_specs=blk,
        scratch_shapes=[pltpu.VMEM((LANE, LANE), F32)],
        compiler_params=_cparams(("parallel", "parallel", "arbitrary")),
        name="rwkv_scan",
    )(r, lw, k, v, nk, kb, g, r_k.reshape(1, GROUP), ln_w.reshape(1, GROUP), ln_b.reshape(1, GROUP))


def _pad_rows(w, start):
    full = jnp.zeros((SMALL, w.shape[1]), F32).at[start:start + w.shape[0]].set(w)
    hi = full.astype(BF16)
    return jnp.stack([hi, (full - hi.astype(F32)).astype(BF16)])


def _layer(x, c8, layer_idx, v_first, vres, bias_ad, bias_as, bias_d, p, attn_tile):
    bsz, seq, d = x.shape
    t = bsz * seq
    mod = _ada(c8, p["ada_w"], p["ada_b"], layer_idx)
    mod3 = mod[:bsz].reshape(bsz * 6, 1, d)
    h = _norm_mod(x, mod3, 1, 0)

    w_in = p["w_in"]
    o_b, o_c, o_d = 3072, 3072 + 3088, 3072 + 3088 + 3264
    vres_cols = vres[0] if vres is not None else jnp.zeros((d, 32), F32)
    w_r = jnp.concatenate([
        w_in[:, :o_b], w_in[:, o_b:o_b + 3072], w_in[:, o_c:o_c + 3072], w_in[:, o_d:],
        w_in[:, o_b + 3072:o_c], w_in[:, o_c + 3072:o_d], vres_cols,
        jnp.zeros((d, SMALL - S_VR - 32), F32)], axis=1).astype(BF16)
    cols = _matmul(h, w_r, F32)

    qk, vt = _prep_a(cols, p["diff_q_norm"], p["diff_k_norm"])
    o_a = _diff_attention(qk, vt, bias_ad, bias_as, p["diff_lambda"], p["diff_subln"], bsz, seq, layer_idx,
                          attn_tile)
    o_dd = _swa_attention(cols, bias_d, p["swa_q_norm"], p["swa_k_norm"], p["swa_sinks"], bsz, seq)
    mu = p["rwkv_mu"]
    vres_mu = vres[1] if vres is not None else jnp.zeros((32,), F32)
    mu_small = jnp.concatenate([jnp.zeros((S_WD,), F32), mu[3 * GROUP:], vres_mu,
                                jnp.zeros((SMALL - S_VR - 32,), F32)]).reshape(1, SMALL)
    v_up = _pad_rows(vres[2], S_VR) if vres is not None else jnp.zeros((2, SMALL, GROUP), BF16)
    v0 = vres[3] if vres is not None else jnp.zeros((GROUP,), F32)
    r_, lw_, k_, v_, nk_, kb_, g_, la_ = _prep_c(
        cols, seq, mu[:3 * GROUP].reshape(1, -1), mu_small,
        _pad_rows(p["rwkv_w_up"], S_WD), _pad_rows(p["rwkv_a_up"], S_AD), _pad_rows(p["rwkv_g_up"], S_GD),
        v_up, _pad_rows(p["gla_gate_up"], S_GLA),
        p["rwkv_w0"].reshape(1, -1), p["rwkv_a0"].reshape(1, -1), v0.reshape(1, -1),
        p["rwkv_k_k"].reshape(1, -1), p["rwkv_k_a"].reshape(1, -1), p["gla_gate_bias"].reshape(1, -1),
        v_first if vres is not None else None)
    if vres is None:
        v_first = v_
    o_bb = _gla(cols, la_, p["gla_out_norm"], bsz, seq)
    o_c = _rwkv_scan(r_, lw_, k_, v_, nk_, kb_, g_, p["rwkv_r_k"].reshape(-1), p["rwkv_ln_w"], p["rwkv_ln_b"],
                     bsz, seq)

    x = _out_proj([o_a, o_bb, o_c, o_dd], p["w_out"].astype(BF16), x, mod3, 2)
    h2 = _norm_mod(x, mod3, 4, 3)
    act = _ffn_up(h2, p["ffn_w1"], p["ffn_w3"], layer_idx)
    x = _ffn_down(act, p["ffn_w2"].astype(BF16), x, mod3, 5)
    return x, v_first


def kernel(x, c, rel_bias, ada_w, ada_b, w_in, w_out, diff_q_norm, diff_k_norm, diff_lambda, diff_subln,
           gla_gate_up, gla_gate_bias, gla_out_norm, rwkv_mu, rwkv_w_up, rwkv_w0, rwkv_a_up, rwkv_a0,
           rwkv_g_up, rwkv_k_k, rwkv_k_a, rwkv_r_k, rwkv_ln_w, rwkv_ln_b, rwkv_vres_down, rwkv_vres_mu,
           rwkv_vres_up, rwkv_v0, swa_q_norm, swa_k_norm, swa_sinks, ffn_w1, ffn_w3, ffn_w2):
    bsz, seq, _ = x.shape
    depth = ada_w.shape[0]
    attn_tile = min(512, seq)
    bias_ad = _bias_tiles(rel_bias[:, :DIFF_HEADS], attn_tile, attn_tile, 0, None, True, True, LOG2E)
    bias_as = _bias_tiles(rel_bias[:, :DIFF_HEADS], attn_tile, attn_tile, attn_tile, None, True, True, LOG2E)
    bias_d = _bias_tiles(rel_bias[:, DIFF_HEADS:], SWA_BLOCK, 2 * SWA_BLOCK, SWA_BLOCK, SWA_BLOCK, False)
    c8 = jnp.zeros((8, c.shape[1]), F32).at[:bsz].set(c)
    v_first = None
    for l in range(depth):
        p = dict(ada_w=ada_w, ada_b=ada_b, w_in=w_in[l], w_out=w_out[l],
                 diff_q_norm=diff_q_norm[l], diff_k_norm=diff_k_norm[l], diff_lambda=diff_lambda[l],
                 diff_subln=diff_subln[l], gla_gate_up=gla_gate_up[l], gla_gate_bias=gla_gate_bias[l],
                 gla_out_norm=gla_out_norm[l], rwkv_mu=rwkv_mu[l], rwkv_w_up=rwkv_w_up[l],
                 rwkv_w0=rwkv_w0[l], rwkv_a_up=rwkv_a_up[l], rwkv_a0=rwkv_a0[l], rwkv_g_up=rwkv_g_up[l],
                 rwkv_k_k=rwkv_k_k[l], rwkv_k_a=rwkv_k_a[l], rwkv_r_k=rwkv_r_k[l], rwkv_ln_w=rwkv_ln_w[l],
                 rwkv_ln_b=rwkv_ln_b[l], swa_q_norm=swa_q_norm[l], swa_k_norm=swa_k_norm[l],
                 swa_sinks=swa_sinks[l], ffn_w1=ffn_w1, ffn_w3=ffn_w3, ffn_w2=ffn_w2[l])
        vres = None if l == 0 else (rwkv_vres_down[l - 1], rwkv_vres_mu[l - 1], rwkv_vres_up[l - 1],
                                    rwkv_v0[l - 1])
        x, v_first = _layer(x, c8, l, v_first, vres, bias_ad, bias_as, bias_d, p, attn_tile)
    return x
```

```python
import functools
import math

import jax
import jax.numpy as jnp
from jax import lax
from jax.experimental import pallas as pl
from jax.experimental.pallas import tpu as pltpu

F32 = jnp.float32
BF16 = jnp.bfloat16
HI = lax.Precision.HIGHEST

D_MODEL = 4096
GROUP = 1024
D_FF = 11008
NORM_EPS = 1e-6
NEG_INF = -1e30
LANE = 128
HALF = 64

DIFF_HEADS = 8
DIFF_QK = 64
GLA_HEADS = 4
GLA_DK = 128
GLA_DV = 256
GLA_RANK = 16
GLA_NORMALIZER = 16.0
CHUNK = 64
SCAN_GROUP = 8
RWKV_LN_EPS = 64e-5
SWA_HEADS = 16
SWA_KV_HEADS = 2
SWA_BLOCK = 128
REL_BUCKETS = 32
REL_MAX_DIST = 128

OFF_A = 0
OFF_B = 3072
OFF_C = 6144
OFF_D = 9216
OFF_S = 10496
SMALL = 256
NCOLS = OFF_S + SMALL
S_GLA, S_WD, S_AD, S_GD, S_VR = 0, 16, 80, 144, 208
TAIL = 512
OFF_T = NCOLS - TAIL

VMEM_LIMIT = 56 * 1024 * 1024


def _cparams(sem, vmem=VMEM_LIMIT):
    return pltpu.CompilerParams(dimension_semantics=sem, vmem_limit_bytes=vmem)


def _dot(a, b, prec=None):
    return jnp.dot(a, b, preferred_element_type=F32, precision=prec)


def _dot_nt(a, b, prec=None):
    return lax.dot_general(a, b, (((1,), (1,)), ((), ())), preferred_element_type=F32, precision=prec)


def _dot_tn(a, b, prec=None):
    return lax.dot_general(a, b, (((0,), (0,)), ((), ())), preferred_element_type=F32, precision=prec)


def _sigmoid(z):
    return 1.0 / (1.0 + jnp.exp(-z))


def _softplus(z):
    return jnp.maximum(z, 0.0) + jnp.log(1.0 + jnp.exp(-jnp.abs(z)))


def _seg_ones():
    r = lax.broadcasted_iota(jnp.int32, (LANE, LANE), 0) // HALF
    c = lax.broadcasted_iota(jnp.int32, (LANE, LANE), 1) // HALF
    return (r == c).astype(BF16)


def _split(x):
    hi = x.astype(BF16)
    return hi, (x - hi.astype(F32)).astype(BF16)


def _dot_left2(x, w):
    hi, lo = _split(x)
    return _dot(hi, w) + _dot(lo, w)


def _tri_cumsum(tri, x):
    hi, lo = _split(x)
    return _dot(tri, hi) + _dot(tri, lo)


def _segsum(x, ones):
    return _dot_left2(x, ones)


def _ada_body(c_ref, w_ref, b_ref, o_ref):
    c = c_ref[...]
    s = (c * _sigmoid(c)).astype(BF16)
    o_ref[...] = _dot(s, w_ref[...].astype(BF16)) + b_ref[...]


def _ada(c8, ada_w, ada_b, layer, tn=512):
    _, d, n = ada_w.shape
    return pl.pallas_call(
        _ada_body,
        out_shape=jax.ShapeDtypeStruct((8, n), F32),
        grid=(n // tn,),
        in_specs=[pl.BlockSpec((8, d), lambda j: (0, 0)),
                  pl.BlockSpec((None, d, tn), lambda j: (layer, 0, j)),
                  pl.BlockSpec((1, tn), lambda j: (0, j))],
        out_specs=pl.BlockSpec((8, tn), lambda j: (0, j)),
        compiler_params=_cparams(("parallel",)),
        name="ada_mod",
    )(c8, ada_w, ada_b[layer].reshape(1, n))


def _cast_body(w_ref, o_ref):
    o_ref[...] = w_ref[...].astype(o_ref.dtype)


def _cast_bf16(w, layer, tr=256):
    _, rows, cols = w.shape
    return pl.pallas_call(
        _cast_body,
        out_shape=jax.ShapeDtypeStruct((rows, cols), BF16),
        grid=(rows // tr,),
        in_specs=[pl.BlockSpec((None, tr, cols), lambda i: (layer, i, 0))],
        out_specs=pl.BlockSpec((tr, cols), lambda i: (i, 0)),
        compiler_params=_cparams(("parallel",)),
        name="weight_cast",
    )(w)


def _norm_body(x_ref, sc_ref, sh_ref, o_ref):
    x = x_ref[0]
    ms = jnp.mean(x * x, axis=-1, keepdims=True)
    h = x * lax.rsqrt(ms + NORM_EPS) * (1.0 + sc_ref[0]) + sh_ref[0]
    o_ref[...] = h.astype(o_ref.dtype)


def _norm_mod(x, mod3, sc_idx, sh_idx, ts=256):
    bsz, seq, d = x.shape
    ts = min(ts, seq)
    ns = seq // ts
    return pl.pallas_call(
        _norm_body,
        out_shape=jax.ShapeDtypeStruct((bsz * seq, d), BF16),
        grid=(bsz, ns),
        in_specs=[pl.BlockSpec((1, ts, d), lambda b, i: (b, i, 0)),
                  pl.BlockSpec((1, 1, d), lambda b, i: (b * 6 + sc_idx, 0, 0)),
                  pl.BlockSpec((1, 1, d), lambda b, i: (b * 6 + sh_idx, 0, 0))],
        out_specs=pl.BlockSpec((ts, d), lambda b, i: (b * ns + i, 0)),
        compiler_params=_cparams(("parallel", "parallel")),
        name="norm_mod",
    )(x, mod3, mod3)


def _in_proj_body(a_ref, b_ref, main_ref, tail_ref):
    j = pl.program_id(1)
    acc = _dot(a_ref[...], b_ref[...])

    @pl.when(j < pl.num_programs(1) - 1)
    def _():
        main_ref[...] = acc.astype(main_ref.dtype)

    @pl.when(j == pl.num_programs(1) - 1)
    def _():
        tail_ref[...] = acc


def _in_proj(a, b, tm=1024):
    m, k = a.shape
    tm = min(tm, m)
    n_main = OFF_T // TAIL
    return pl.pallas_call(
        _in_proj_body,
        out_shape=[jax.ShapeDtypeStruct((m, OFF_T), BF16), jax.ShapeDtypeStruct((m, TAIL), F32)],
        grid=(m // tm, n_main + 1),
        in_specs=[pl.BlockSpec((tm, k), lambda i, j: (i, 0)),
                  pl.BlockSpec((k, TAIL), lambda i, j: (0, j))],
        out_specs=[pl.BlockSpec((tm, TAIL), lambda i, j: (i, jnp.minimum(j, n_main - 1))),
                   pl.BlockSpec((tm, TAIL), lambda i, j: (i, 0))],
        compiler_params=_cparams(("parallel", "arbitrary")),
        name="in_proj",
    )(a, b)


def _outproj_body(a0, a1, a2, a3, w_ref, x_ref, g_ref, o_ref):
    acc = _dot(a0[...], w_ref[0 * GROUP:1 * GROUP, :])
    acc += _dot(a1[...], w_ref[1 * GROUP:2 * GROUP, :])
    acc += _dot(a2[...], w_ref[2 * GROUP:3 * GROUP, :])
    acc += _dot(a3[...], w_ref[3 * GROUP:4 * GROUP, :])
    o_ref[0] = x_ref[0] + g_ref[0] * acc


def _out_proj(parts, w, x, mod3, gate_idx, tm=1024, tn=512):
    bsz, seq, d = x.shape
    tm = min(tm, seq)
    ns = seq // tm
    a_spec = pl.BlockSpec((tm, GROUP), lambda b, i, j: (b * ns + i, 0))
    return pl.pallas_call(
        _outproj_body,
        out_shape=jax.ShapeDtypeStruct((bsz, seq, d), F32),
        grid=(bsz, ns, d // tn),
        in_specs=[a_spec, a_spec, a_spec, a_spec,
                  pl.BlockSpec((4 * GROUP, tn), lambda b, i, j: (0, j)),
                  pl.BlockSpec((1, tm, tn), lambda b, i, j: (b, i, j)),
                  pl.BlockSpec((1, 1, tn), lambda b, i, j: (b * 6 + gate_idx, 0, j))],
        out_specs=pl.BlockSpec((1, tm, tn), lambda b, i, j: (b, i, j)),
        compiler_params=_cparams(("parallel", "parallel", "parallel")),
        name="out_proj",
    )(*parts, w, x, mod3)


def _ffn_up_body(a_ref, w1_ref, w3_ref, o_ref):
    a = a_ref[...]
    u = _dot(a, w1_ref[...].astype(BF16))
    v = _dot(a, w3_ref[...].astype(BF16))
    o_ref[...] = (u * _sigmoid(u) * v).astype(o_ref.dtype)


def _ffn_up(h, w1, w3, layer, tm=1024, tn=256):
    m, k = h.shape
    n = w1.shape[2]
    tm = min(tm, m)
    return pl.pallas_call(
        _ffn_up_body,
        out_shape=jax.ShapeDtypeStruct((m, n), BF16),
        grid=(m // tm, n // tn),
        in_specs=[pl.BlockSpec((tm, k), lambda i, j: (i, 0)),
                  pl.BlockSpec((None, k, tn), lambda i, j: (layer, 0, j)),
                  pl.BlockSpec((None, k, tn), lambda i, j: (layer, 0, j))],
        out_specs=pl.BlockSpec((tm, tn), lambda i, j: (i, j)),
        compiler_params=_cparams(("parallel", "parallel")),
        name="ffn_up",
    )(h, w1, w3)


def _ffn_down_body(a_ref, w_ref, x_ref, g_ref, o_ref):
    o_ref[0] = x_ref[0] + g_ref[0] * _dot(a_ref[...], w_ref[...])


def _ffn_down(a, w, x, mod3, gate_idx, tm=512, tn=512):
    bsz, seq, d = x.shape
    k = a.shape[1]
    tm = min(tm, seq)
    ns = seq // tm
    return pl.pallas_call(
        _ffn_down_body,
        out_shape=jax.ShapeDtypeStruct((bsz, seq, d), F32),
        grid=(bsz, ns, d // tn),
        in_specs=[pl.BlockSpec((tm, k), lambda b, i, j: (b * ns + i, 0)),
                  pl.BlockSpec((k, tn), lambda b, i, j: (0, j)),
                  pl.BlockSpec((1, tm, tn), lambda b, i, j: (b, i, j)),
                  pl.BlockSpec((1, 1, tn), lambda b, i, j: (b * 6 + gate_idx, 0, j))],
        out_specs=pl.BlockSpec((1, tm, tn), lambda b, i, j: (b, i, j)),
        compiler_params=_cparams(("parallel", "parallel", "parallel")),
        name="ffn_down",
    )(a, w, x, mod3)


def _bias_body(tab_ref, o_ref, *, off, window, shift, key_major, scale):
    h = pl.program_id(0)
    rows, cols = o_ref.shape[1], o_ref.shape[2]
    qi = lax.broadcasted_iota(jnp.int32, (rows, cols), 1 if key_major else 0)
    kj = lax.broadcasted_iota(jnp.int32, (rows, cols), 0 if key_major else 1)
    dist = off + qi - kj
    max_exact = REL_BUCKETS // 2
    n = jnp.maximum(dist, 0)
    nf = jnp.maximum(n, 1).astype(F32)
    large = max_exact + (jnp.log(nf / max_exact) / math.log(REL_MAX_DIST / max_exact)
                         * (REL_BUCKETS - max_exact)).astype(jnp.int32)
    large = jnp.minimum(large, REL_BUCKETS - 1)
    bucket = jnp.where(n < max_exact, n, large)
    last = tab_ref[REL_BUCKETS - 1, h]
    bias = jnp.full((rows, cols), last, F32)
    for b in range(REL_BUCKETS - 1):
        bias = jnp.where(bucket == b, tab_ref[b, h], bias)
    if shift:
        bias = bias - last
    if scale != 1.0:
        bias = bias * scale
    valid = dist >= 0
    if window is not None:
        valid = jnp.logical_and(valid, dist < window)
    o_ref[0] = jnp.where(valid, bias, NEG_INF)


def _bias_tiles(table, rows, cols, off, window, shift, key_major=False, scale=1.0):
    nh = table.shape[1]
    return pl.pallas_call(
        functools.partial(_bias_body, off=off, window=window, shift=shift, key_major=key_major, scale=scale),
        out_shape=jax.ShapeDtypeStruct((nh, rows, cols), F32),
        grid=(nh,),
        in_specs=[pl.BlockSpec(memory_space=pltpu.SMEM)],
        out_specs=pl.BlockSpec((1, rows, cols), lambda h: (h, 0, 0)),
        compiler_params=_cparams(("parallel",)),
        name="rel_bias_tiles",
    )(table)


LOG2E = 1.4426950408889634
VT_ROWS = LANE + 16


def _prep_a_body(c_ref, qg_ref, kg_ref, qk_ref, vt_ref):
    ones = _seg_ones()
    scale = DIFF_QK ** -0.5 * LOG2E
    tm = c_ref.shape[0]
    for j in range(2 * DIFF_HEADS):
        x = c_ref[:, j * LANE:(j + 1) * LANE].astype(F32)
        ms = _segsum(x * x, ones) * (1.0 / DIFF_QK)
        gain = qg_ref[...] * scale if j < DIFF_HEADS else kg_ref[...]
        qk_ref[:, j * LANE:(j + 1) * LANE] = (x * lax.rsqrt(ms + NORM_EPS) * gain).astype(qk_ref.dtype)
    for h in range(DIFF_HEADS):
        v = c_ref[:, 2 * GROUP + h * LANE:2 * GROUP + (h + 1) * LANE].astype(F32)
        vt_ref[h * VT_ROWS:h * VT_ROWS + LANE, :] = v.T.astype(vt_ref.dtype)
        vt_ref[h * VT_ROWS + LANE:(h + 1) * VT_ROWS, :] = jnp.ones((VT_ROWS - LANE, tm), vt_ref.dtype)


def _prep_a(cols, q_gain, k_gain, tm=256):
    t = cols.shape[0]
    tm = min(tm, t)
    qg = jnp.tile(q_gain, 2).reshape(1, LANE)
    kg = jnp.tile(k_gain, 2).reshape(1, LANE)
    vec = pl.BlockSpec((1, LANE), lambda i: (0, 0))
    return pl.pallas_call(
        _prep_a_body,
        out_shape=[jax.ShapeDtypeStruct((t, 2 * GROUP), BF16),
                   jax.ShapeDtypeStruct((DIFF_HEADS * VT_ROWS, t), BF16)],
        grid=(t // tm,),
        in_specs=[pl.BlockSpec((tm, 3 * GROUP), lambda i: (i, OFF_A // (3 * GROUP))), vec, vec],
        out_specs=[pl.BlockSpec((tm, 2 * GROUP), lambda i: (i, 0)),
                   pl.BlockSpec((DIFF_HEADS * VT_ROWS, tm), lambda i: (0, i))],
        compiler_params=_cparams(("parallel",)),
        name="diff_prep",
    )(cols, qg, kg)


def _flash_body(it_ref, jt_ref, q_ref, k_ref, vt_ref, bd_ref, bs_ref, lam_ref, sub_ref, o_ref,
                qlo, qhi, m1, a1, m2, a2, *, lam_init):
    t = pl.program_id(2)
    i = it_ref[t]
    j = jt_ref[t]

    @pl.when(j == 0)
    def _():
        q = q_ref[...]
        lane = lax.broadcasted_iota(jnp.int32, q.shape, 1)
        zero = jnp.zeros_like(q)
        qlo[...] = jnp.where(lane < HALF, q, zero)
        qhi[...] = jnp.where(lane >= HALF, q, zero)
        for m, a in ((m1, a1), (m2, a2)):
            m[...] = jnp.full(m.shape, NEG_INF, F32)
            a[...] = jnp.zeros(a.shape, F32)

    def step(bias):
        k = k_ref[...]
        vt = vt_ref[...]
        scores = [_dot_nt(k, qq[...]) for qq in (qlo, qhi)]
        for s, m, a in ((scores[0], m1, a1), (scores[1], m2, a2)):
            if bias is not None:
                s = s + bias
            m_old = m[...]
            m_new = jnp.maximum(m_old, jnp.max(s, axis=0, keepdims=True))
            p = jnp.exp2(s - m_new).astype(BF16)
            a[...] = jnp.exp2(m_old - m_new) * a[...] + _dot(vt, p)
            m[...] = m_new

    @pl.when(j < i - 1)
    def _():
        step(None)

    @pl.when(j == i - 1)
    def _():
        step(bs_ref[0])

    @pl.when(j == i)
    def _():
        step(bd_ref[0])
        lam = lam_ref[...]
        e1 = jnp.exp(jnp.sum(lam[0:1] * lam[1:2], axis=-1, keepdims=True))
        e2 = jnp.exp(jnp.sum(lam[2:3] * lam[3:4], axis=-1, keepdims=True))
        lam_full = e1 - e2 + lam_init
        o = a1[:LANE, :] / a1[LANE:LANE + 1, :] - lam_full * (a2[:LANE, :] / a2[LANE:LANE + 1, :])
        ms = jnp.mean(o * o, axis=0, keepdims=True)
        o = o * lax.rsqrt(ms + NORM_EPS) * sub_ref[...] * (1.0 - lam_init)
        o_ref[...] = o.T.astype(o_ref.dtype)


def _diff_attention(qk, vt, bias_diag, bias_sub, lam, subln, bsz, seq, layer_idx, tile):
    nq = seq // tile
    lam_init = 0.8 - 0.6 * math.exp(-0.3 * layer_idx)
    pairs = [(i, j) for i in range(nq) for j in range(i + 1)]
    i_tab = jnp.array([ij[0] for ij in pairs], jnp.int32)
    j_tab = jnp.array([ij[1] for ij in pairs], jnp.int32)
    grid_spec = pltpu.PrefetchScalarGridSpec(
        num_scalar_prefetch=2,
        grid=(bsz, DIFF_HEADS, len(pairs)),
        in_specs=[pl.BlockSpec((tile, LANE), lambda b, h, t, it, jt: (b * nq + it[t], h)),
                  pl.BlockSpec((tile, LANE), lambda b, h, t, it, jt: (b * nq + jt[t], DIFF_HEADS + h)),
                  pl.BlockSpec((VT_ROWS, tile), lambda b, h, t, it, jt: (h, b * nq + jt[t])),
                  pl.BlockSpec((1, tile, tile), lambda b, h, t, it, jt: (h, 0, 0)),
                  pl.BlockSpec((1, tile, tile), lambda b, h, t, it, jt: (h, 0, 0)),
                  pl.BlockSpec((4, DIFF_QK), lambda b, h, t, it, jt: (0, 0)),
                  pl.BlockSpec((LANE, 1), lambda b, h, t, it, jt: (0, 0))],
        out_specs=pl.BlockSpec((tile, LANE), lambda b, h, t, it, jt: (b * nq + it[t], h)),
        scratch_shapes=[pltpu.VMEM((tile, LANE), BF16), pltpu.VMEM((tile, LANE), BF16),
                        pltpu.VMEM((1, tile), F32), pltpu.VMEM((VT_ROWS, tile), F32),
                        pltpu.VMEM((1, tile), F32), pltpu.VMEM((VT_ROWS, tile), F32)])
    return pl.pallas_call(
        functools.partial(_flash_body, lam_init=lam_init),
        out_shape=jax.ShapeDtypeStruct((bsz * seq, GROUP), BF16),
        grid_spec=grid_spec,
        compiler_params=_cparams(("parallel", "parallel", "arbitrary")),
        name="diff_attention",
    )(i_tab, j_tab, qk, qk, vt, bias_diag, bias_sub, lam, subln.reshape(LANE, 1))


def _swa_body(q_ref, kp_ref, kc_ref, vp_ref, vc_ref, bias_ref, qg_ref, kg_ref, sink_ref, o_ref):
    n = pl.program_id(1)
    g = pl.program_id(2)
    ones = _seg_ones()
    lane = lax.broadcasted_iota(jnp.int32, (SWA_BLOCK, LANE), 1)
    lo = lane < HALF

    def norm(x, gain):
        ms = _segsum(x * x, ones) * (1.0 / HALF)
        return x * lax.rsqrt(ms + NORM_EPS) * gain

    k = jnp.concatenate([norm(kp_ref[...], kg_ref[...]), norm(kc_ref[...], kg_ref[...])], axis=0)
    v = jnp.concatenate([vp_ref[...], vc_ref[...]], axis=0)
    k_sw = pltpu.roll(k, HALF, axis=1)
    v_sw = pltpu.roll(v, HALF, axis=1)
    first_head = g == 0
    lo2 = lax.broadcasted_iota(jnp.int32, (2 * SWA_BLOCK, LANE), 1) < HALF
    k_a = jnp.where(first_head, k, k_sw).astype(BF16)
    k_b = jnp.where(first_head, k_sw, k).astype(BF16)
    v_mine_lo = jnp.where(first_head, v, v_sw)
    v_mine_hi = jnp.where(first_head, v_sw, v)
    v_a = jnp.where(lo2, v_mine_lo, 0.0).astype(BF16)
    v_b = jnp.where(lo2, 0.0, v_mine_hi).astype(BF16)
    kcol = lax.broadcasted_iota(jnp.int32, (SWA_BLOCK, 2 * SWA_BLOCK), 1)
    pad = jnp.logical_and(n == 0, kcol < SWA_BLOCK)
    scale = HALF ** -0.5
    grp = SWA_HEADS // SWA_KV_HEADS
    heads = range(grp)
    q = [norm(q_ref[:, pr * LANE:(pr + 1) * LANE].astype(F32), qg_ref[...] * scale) for pr in range(grp // 2)]
    qm = [jnp.where(lo if hh % 2 == 0 else jnp.logical_not(lo), q[hh // 2], 0.0).astype(BF16) for hh in heads]
    s = [_dot_nt(qm[hh], k_a if hh % 2 == 0 else k_b) for hh in heads]
    s = [jnp.where(pad, NEG_INF, s[hh] + bias_ref[hh]) for hh in heads]
    sink = [sink_ref[g * grp + hh] for hh in heads]
    m = [jnp.maximum(jnp.max(s[hh], axis=-1, keepdims=True), sink[hh]) for hh in heads]
    p = [jnp.exp(s[hh] - m[hh]) for hh in heads]
    den = [jnp.sum(p[hh], axis=-1, keepdims=True) + jnp.exp(sink[hh] - m[hh]) for hh in heads]
    pv = [_dot(p[hh].astype(BF16), v_a if hh % 2 == 0 else v_b) for hh in heads]
    for pr in range(grp // 2):
        o = pv[2 * pr] / den[2 * pr] + pv[2 * pr + 1] / den[2 * pr + 1]
        o_ref[:, pr * LANE:(pr + 1) * LANE] = o.astype(o_ref.dtype)


def _swa_attention(cols, tail, bias, q_gain, k_gain, sinks, bsz, seq):
    nb = seq // SWA_BLOCK
    grp = SWA_HEADS // SWA_KV_HEADS
    qw = grp * HALF
    kcol = (OFF_D + GROUP - OFF_T) // LANE
    vcol = kcol + 1
    prev = lambda b, n, g: b * nb + jnp.maximum(n - 1, 0)
    cur = lambda b, n, g: b * nb + n
    vec = pl.BlockSpec((1, LANE), lambda b, n, g: (0, 0))
    return pl.pallas_call(
        _swa_body,
        out_shape=jax.ShapeDtypeStruct((bsz * seq, GROUP), BF16),
        grid=(bsz, nb, SWA_KV_HEADS),
        in_specs=[pl.BlockSpec((SWA_BLOCK, qw), lambda b, n, g: (cur(b, n, g), OFF_D // qw + g)),
                  pl.BlockSpec((SWA_BLOCK, LANE), lambda b, n, g: (prev(b, n, g), kcol)),
                  pl.BlockSpec((SWA_BLOCK, LANE), lambda b, n, g: (cur(b, n, g), kcol)),
                  pl.BlockSpec((SWA_BLOCK, LANE), lambda b, n, g: (prev(b, n, g), vcol)),
                  pl.BlockSpec((SWA_BLOCK, LANE), lambda b, n, g: (cur(b, n, g), vcol)),
                  pl.BlockSpec((grp, SWA_BLOCK, 2 * SWA_BLOCK), lambda b, n, g: (g, 0, 0)),
                  vec, vec,
                  pl.BlockSpec(memory_space=pltpu.SMEM)],
        out_specs=pl.BlockSpec((SWA_BLOCK, qw), lambda b, n, g: (cur(b, n, g), g)),
        compiler_params=_cparams(("parallel", "parallel", "parallel")),
        name="swa_attention",
    )(cols, tail, tail, tail, tail, bias,
      jnp.tile(q_gain, 2).reshape(1, LANE), jnp.tile(k_gain, 2).reshape(1, LANE), sinks)


def _prep_c_body(*refs, seq, with_vres):
    (cm_ref, cs_ref, pm_ref, ps_ref, mum_ref, mus_ref, wup_ref, aup_ref, gup_ref, vup_ref, glaup_ref,
     w0_ref, a0_ref, v0_ref, kk_ref, ka_ref, glab_ref) = refs[:17]
    rest = refs[17:]
    if with_vres:
        vf_ref, rest = rest[0], rest[1:]
    r_ref, lw_ref, k_ref, v_ref, nk_ref, kb_ref, g_ref, la_ref = rest
    tm = cm_ref.shape[0]
    i = pl.program_id(0)
    seq_start = (i * tm) % seq == 0

    def shifted(cur, prev_rows, mu):
        row = lax.broadcasted_iota(jnp.int32, cur.shape, 0)
        before = jnp.where(seq_start, 0.0, prev_rows[PREV_ROWS - 1:PREV_ROWS, :].astype(F32))
        prev = jnp.where(row == 0, before, pltpu.roll(cur, 1, axis=0))
        return cur + (prev - cur) * mu

    def low_rank(x, w2_ref):
        hi, lo = _split(x)
        return _dot(hi, w2_ref[0]) + _dot(lo, w2_ref[0]) + _dot(hi, w2_ref[1])

    cs = cs_ref[...]
    la_ref[...] = -_softplus(-(low_rank(cs, glaup_ref) + glab_ref[...])) * (1.0 / GLA_NORMALIZER)
    sm = shifted(cm_ref[...].astype(F32), pm_ref, mum_ref[...])
    ss = shifted(cs, ps_ref, mus_ref[...])
    r = sm[:, :GROUP]
    k = sm[:, GROUP:2 * GROUP]
    v = sm[:, 2 * GROUP:]
    if with_vres:
        gate = _sigmoid(v0_ref[...] + low_rank(ss, vup_ref))
        v = v + (vf_ref[...] - v) * gate
    w_log = -_softplus(-(w0_ref[...] + low_rank(jnp.tanh(ss), wup_ref))) - 0.5
    a = _sigmoid(a0_ref[...] + low_rank(ss, aup_ref))
    r_ref[...] = r.astype(r_ref.dtype)
    lw_ref[...] = -jnp.exp(w_log)
    v_ref[...] = v
    g_ref[...] = low_rank(_sigmoid(ss), gup_ref).astype(g_ref.dtype)
    k_ref[...] = (k * (1.0 + (a - 1.0) * ka_ref[...])).astype(k_ref.dtype)
    ones = _seg_ones()
    kk = k * kk_ref[...]
    for j in range(GROUP // LANE):
        sl = slice(j * LANE, (j + 1) * LANE)
        x = kk[:, sl]
        nrm = jnp.maximum(jnp.sqrt(_segsum(x * x, ones)), 1e-12)
        x = x / nrm
        nk_ref[:, sl] = x.astype(nk_ref.dtype)
        kb_ref[:, sl] = (x * a[:, sl]).astype(kb_ref.dtype)


PREV_ROWS = 16


def _prep_c(cols, tail, seq, mu_main, mu_small, w_up, a_up, g_up, v_up, gla_up, w0, a0, v0, k_k, k_a, gla_bias,
            v_first, tm=256):
    t = cols.shape[0]
    tm = min(tm, seq)
    with_vres = v_first is not None
    cmain = OFF_C // (3 * GROUP)
    csmall = (OFF_S - OFF_T) // SMALL
    prev = lambda i: jnp.maximum(i * (tm // PREV_ROWS) - 1, 0)
    full = lambda r, c: pl.BlockSpec((r, c), lambda i: (0, 0))
    hilo = lambda c: pl.BlockSpec((2, SMALL, c), lambda i: (0, 0, 0))
    in_specs = [pl.BlockSpec((tm, 3 * GROUP), lambda i: (i, cmain)),
                pl.BlockSpec((tm, SMALL), lambda i: (i, csmall)),
                pl.BlockSpec((PREV_ROWS, 3 * GROUP), lambda i: (prev(i), cmain)),
                pl.BlockSpec((PREV_ROWS, SMALL), lambda i: (prev(i), csmall)),
                full(1, 3 * GROUP), full(1, SMALL),
                hilo(GROUP), hilo(GROUP), hilo(GROUP), hilo(GROUP), hilo(GLA_HEADS * GLA_DK),
                full(1, GROUP), full(1, GROUP), full(1, GROUP), full(1, GROUP), full(1, GROUP),
                full(1, GLA_HEADS * GLA_DK)]
    args = [cols, tail, cols, tail, mu_main, mu_small, w_up, a_up, g_up, v_up, gla_up,
            w0, a0, v0, k_k, k_a, gla_bias]
    if with_vres:
        in_specs.append(pl.BlockSpec((tm, GROUP), lambda i: (i, 0)))
        args.append(v_first)
    row = pl.BlockSpec((tm, GROUP), lambda i: (i, 0))
    out = [jax.ShapeDtypeStruct((t, GROUP), dt) for dt in (BF16, F32, BF16, F32, BF16, BF16, BF16)]
    return pl.pallas_call(
        functools.partial(_prep_c_body, seq=seq, with_vres=with_vres),
        out_shape=out + [jax.ShapeDtypeStruct((t, GLA_HEADS * GLA_DK), F32)],
        grid=(t // tm,),
        in_specs=in_specs,
        out_specs=[row] * 7 + [pl.BlockSpec((tm, GLA_HEADS * GLA_DK), lambda i: (i, 0))],
        compiler_params=_cparams(("parallel",)),
        name="rwkv_gla_prep",
    )(*args)


def _gla_body(q_ref, k_ref, v_ref, g_ref, la_ref, gain_ref, o_ref, st_ref):
    @pl.when(pl.program_id(2) == 0)
    def _():
        st_ref[...] = jnp.zeros(st_ref.shape, F32)

    row = lax.broadcasted_iota(jnp.int32, (CHUNK, CHUNK), 0)
    col = lax.broadcasted_iota(jnp.int32, (CHUNK, CHUNK), 1)
    causal = row >= col
    tri = causal.astype(BF16)
    nchunk = q_ref.shape[0] // CHUNK

    def local(sls):
        n = range(len(sls))
        k = [k_ref[sl, :].astype(F32) for sl in sls]
        v = [v_ref[sl, :].astype(BF16) for sl in sls]
        b = [_tri_cumsum(tri, la_ref[sl, :]) for sl in sls]
        b_last = [b[i][CHUNK - 1:CHUNK, :] for i in n]
        q_dec = [(q_ref[sl, :].astype(F32) * (GLA_DK ** -0.5) * jnp.exp(b[i])).astype(BF16)
                 for i, sl in enumerate(sls)]
        a_intra = [jnp.where(causal, _dot_nt(q_dec[i], (k[i] * jnp.exp(-b[i])).astype(BF16)), 0.0) for i in n]
        upd = [_dot_tn(v[i], (k[i] * jnp.exp(b_last[i] - b[i])).astype(BF16)) for i in n]
        o_intra = [_dot(a_intra[i].astype(BF16), v[i]) for i in n]
        return [(q_dec[i], o_intra[i], upd[i], jnp.exp(b_last[i])) for i in n]

    def advance(sl, q_dec, o_intra, upd, dec):
        state = st_ref[...]
        s_hi, s_lo = _split(state)
        o = o_intra + _dot_nt(q_dec, s_hi) + _dot_nt(q_dec, s_lo)
        st_ref[...] = state * dec + upd
        ms = jnp.mean(o * o, axis=-1, keepdims=True)
        o = o * lax.rsqrt(ms + NORM_EPS) * gain_ref[...]
        gate = g_ref[sl, :].astype(F32)
        o_ref[sl, :] = (o * (gate * _sigmoid(gate))).astype(o_ref.dtype)

    def group(gi, carry):
        sls = [pl.ds(pl.multiple_of((gi * SCAN_GROUP + g) * CHUNK, CHUNK), CHUNK) for g in range(SCAN_GROUP)]
        for sl, part in zip(sls, local(sls)):
            advance(sl, *part)
        return carry

    lax.fori_loop(0, nchunk // SCAN_GROUP, group, 0)


def _gla(cols, log_a, out_gain, bsz, seq, tc=512):
    tc = min(tc, seq)
    ns = seq // tc
    qc = OFF_B // GLA_DK
    kc = qc + GLA_HEADS
    vc = (OFF_B + 2 * GLA_HEADS * GLA_DK) // GLA_DV
    gc = vc + GLA_HEADS
    rows = lambda b, h, i: b * ns + i
    return pl.pallas_call(
        _gla_body,
        out_shape=jax.ShapeDtypeStruct((bsz * seq, GROUP), BF16),
        grid=(bsz, GLA_HEADS, ns),
        in_specs=[pl.BlockSpec((tc, GLA_DK), lambda b, h, i: (rows(b, h, i), qc + h)),
                  pl.BlockSpec((tc, GLA_DK), lambda b, h, i: (rows(b, h, i), kc + h)),
                  pl.BlockSpec((tc, GLA_DV), lambda b, h, i: (rows(b, h, i), vc + h)),
                  pl.BlockSpec((tc, GLA_DV), lambda b, h, i: (rows(b, h, i), gc + h)),
                  pl.BlockSpec((tc, GLA_DK), lambda b, h, i: (rows(b, h, i), h)),
                  pl.BlockSpec((1, GLA_DV), lambda b, h, i: (0, 0))],
        out_specs=pl.BlockSpec((tc, GLA_DV), lambda b, h, i: (rows(b, h, i), h)),
        scratch_shapes=[pltpu.VMEM((GLA_DV, GLA_DK), F32)],
        compiler_params=_cparams(("parallel", "parallel", "arbitrary")),
        name="gla_scan",
    )(cols, cols, cols, cols, log_a, out_gain.reshape(1, GLA_DV))


def _rwkv_body(r_ref, lw_ref, k_ref, v_ref, nk_ref, kb_ref, g_ref, rk_ref, lnw_ref, lnb_ref, o_ref, st_ref):
    @pl.when(pl.program_id(2) == 0)
    def _():
        st_ref[...] = jnp.zeros(st_ref.shape, F32)

    two = 2 * CHUNK
    row = lax.broadcasted_iota(jnp.int32, (two, two), 0)
    col = lax.broadcasted_iota(jnp.int32, (two, two), 1)
    same = (row // CHUNK) == (col // CHUNK)
    strict = jnp.logical_and(same, (row % CHUNK) > (col % CHUNK))
    incl = jnp.logical_and(same, (row % CHUNK) >= (col % CHUNK))
    eye = (row == col).astype(F32)
    crow = lax.broadcasted_iota(jnp.int32, (CHUNK, CHUNK), 0)
    ccol = lax.broadcasted_iota(jnp.int32, (CHUNK, CHUNK), 1)
    tri = (crow >= ccol).astype(BF16)
    lo = lax.broadcasted_iota(jnp.int32, (CHUNK, LANE), 1) < HALF
    ones = _seg_ones()
    nchunk = r_ref.shape[0] // CHUNK

    def stack(x):
        return jnp.concatenate([jnp.where(lo, x, 0.0), jnp.where(lo, 0.0, x)], axis=0)

    def fold(x):
        return x[:CHUNK, :] + x[CHUNK:, :]

    def transitions(sls):
        n = range(len(sls))
        r = [r_ref[sl, :].astype(F32) for sl in sls]
        lw = [lw_ref[sl, :] for sl in sls]
        k = [k_ref[sl, :].astype(F32) for sl in sls]
        v = [v_ref[sl, :] for sl in sls]
        cum = [_tri_cumsum(tri, lw[i]) for i in n]
        last = [cum[i][CHUNK - 1:CHUNK, :] for i in n]
        a_bf, r_st, bk_st, bkh_st, v_st = [], [], [], [], []
        for i, sl in enumerate(sls):
            kb = kb_ref[sl, :].astype(F32)
            e_neg = jnp.exp(-cum[i])
            e_rem = jnp.exp(last[i] - cum[i])
            a_bf.append(stack(-nk_ref[sl, :].astype(F32) * jnp.exp(cum[i] - lw[i])).astype(BF16))
            r_st.append(stack(r[i] * jnp.exp(cum[i])))
            bk_st.append(jnp.concatenate([stack(kb * e_neg), stack(k[i] * e_neg)], axis=0).astype(BF16))
            bkh_st.append(jnp.concatenate([stack(kb * e_rem), stack(k[i] * e_rem)], axis=0).astype(BF16))
            v_st.append(stack(v[i]).astype(BF16))
        sc = [_dot_nt(jnp.concatenate([a_bf[i], r_st[i].astype(BF16)], axis=0), bk_st[i]) for i in n]
        a_ab = [jnp.where(strict, sc[i][:two, :two], 0.0) for i in n]
        a_ak = [jnp.where(strict, sc[i][:two, two:], 0.0).astype(BF16) for i in n]
        r_b = [jnp.where(incl, sc[i][two:, :two], 0.0).astype(BF16) for i in n]
        r_k = [jnp.where(incl, sc[i][two:, two:], 0.0).astype(BF16) for i in n]
        akv = [_dot(a_ak[i], v_st[i]).astype(BF16) for i in n]
        rkv = [_dot(r_k[i], v_st[i]) for i in n]
        inv = [eye + a_ab[i] for i in n]
        pw = [a_ab[i].astype(BF16) for i in n]
        for _ in range(5):
            pw = [_dot(pw[i], pw[i]).astype(BF16) for i in n]
            inv = [inv[i] + _dot(inv[i].astype(BF16), pw[i]) for i in n]
        pu = [_dot(inv[i].astype(BF16), jnp.concatenate([a_bf[i], akv[i]], axis=1)).astype(BF16) for i in n]
        rb_pu = [_dot(r_b[i], pu[i]) for i in n]
        m = [_dot_tn(pu[i][:, :LANE], bkh_st[i][:two, :]).astype(BF16) for i in n]
        n0 = [_dot_tn(jnp.concatenate([pu[i][:, LANE:], v_st[i]], axis=0), bkh_st[i]) for i in n]
        rkr = [_segsum(r[i] * k[i] * rk_ref[...], ones) for i in n]
        return [(fold(r_st[i] + rb_pu[i][:, :LANE]).astype(BF16), fold(rb_pu[i][:, LANE:] + rkv[i]),
                 m[i], n0[i], jnp.exp(last[i]), rkr[i] * v[i]) for i in n]

    def advance(sl, p2, y0, m, n0, dec, bonus):
        state = st_ref[...]
        s_hi, s_lo = _split(state)
        y = _dot_nt(p2, s_hi) + y0
        st_ref[...] = state * dec + _dot(s_hi, m) + _dot(s_lo, m) + n0
        d = y - _segsum(y, ones) * (1.0 / HALF)
        var = _segsum(d * d, ones) * (1.0 / HALF)
        y = d * lax.rsqrt(var + RWKV_LN_EPS) * lnw_ref[...] + lnb_ref[...]
        o_ref[sl, :] = ((y + bonus) * g_ref[sl, :]).astype(o_ref.dtype)

    def group(gi, carry):
        sls = [pl.ds(pl.multiple_of((gi * SCAN_GROUP + g) * CHUNK, CHUNK), CHUNK) for g in range(SCAN_GROUP)]
        for sl, part in zip(sls, transitions(sls)):
            advance(sl, *part)
        return carry

    lax.fori_loop(0, nchunk // SCAN_GROUP, group, 0)


def _rwkv_scan(r, lw, k, v, nk, kb, g, r_k, ln_w, ln_b, bsz, seq, tc=512):
    tc = min(tc, seq)
    ns = seq // tc
    npair = GROUP // LANE
    blk = pl.BlockSpec((tc, LANE), lambda b, p, i: (b * ns + i, p))
    vec = pl.BlockSpec((1, LANE), lambda b, p, i: (0, p))
    return pl.pallas_call(
        _rwkv_body,
        out_shape=jax.ShapeDtypeStruct((bsz * seq, GROUP), BF16),
        grid=(bsz, npair, ns),
        in_specs=[blk] * 7 + [vec] * 3,
        out_specs=blk,
        scratch_shapes=[pltpu.VMEM((LANE, LANE), F32)],
        compiler_params=_cparams(("parallel", "parallel", "arbitrary")),
        name="rwkv_scan",
    )(r, lw, k, v, nk, kb, g, r_k.reshape(1, GROUP), ln_w.reshape(1, GROUP), ln_b.reshape(1, GROUP))


def _pad_rows(w, start):
    full = jnp.zeros((SMALL, w.shape[1]), F32).at[start:start + w.shape[0]].set(w)
    hi = full.astype(BF16)
    return jnp.stack([hi, (full - hi.astype(F32)).astype(BF16)])


def _layer(x, c8, layer_idx, v_first, vres, bias_ad, bias_as, bias_d, p, attn_tile):
    bsz, seq, d = x.shape
    t = bsz * seq
    mod = _ada(c8, p["ada_w"], p["ada_b"], layer_idx)
    mod3 = mod[:bsz].reshape(bsz * 6, 1, d)
    h = _norm_mod(x, mod3, 1, 0)

    w_in = p["w_in"]
    o_b, o_c, o_d = 3072, 3072 + 3088, 3072 + 3088 + 3264
    vres_cols = vres[0] if vres is not None else jnp.zeros((d, 32), F32)
    w_r = jnp.concatenate([
        w_in[:, :o_b], w_in[:, o_b:o_b + 3072], w_in[:, o_c:o_c + 3072], w_in[:, o_d:],
        w_in[:, o_b + 3072:o_c], w_in[:, o_c + 3072:o_d], vres_cols,
        jnp.zeros((d, SMALL - S_VR - 32), F32)], axis=1).astype(BF16)
    cols, tail = _in_proj(h, w_r)

    qk, vt = _prep_a(cols, p["diff_q_norm"], p["diff_k_norm"])
    o_a = _diff_attention(qk, vt, bias_ad, bias_as, p["diff_lambda"], p["diff_subln"], bsz, seq, layer_idx,
                          attn_tile)
    o_dd = _swa_attention(cols, tail, bias_d, p["swa_q_norm"], p["swa_k_norm"], p["swa_sinks"], bsz, seq)
    mu = p["rwkv_mu"]
    vres_mu = vres[1] if vres is not None else jnp.zeros((32,), F32)
    mu_small = jnp.concatenate([jnp.zeros((S_WD,), F32), mu[3 * GROUP:], vres_mu,
                                jnp.zeros((SMALL - S_VR - 32,), F32)]).reshape(1, SMALL)
    v_up = _pad_rows(vres[2], S_VR) if vres is not None else jnp.zeros((2, SMALL, GROUP), BF16)
    v0 = vres[3] if vres is not None else jnp.zeros((GROUP,), F32)
    r_, lw_, k_, v_, nk_, kb_, g_, la_ = _prep_c(
        cols, tail, seq, mu[:3 * GROUP].reshape(1, -1), mu_small,
        _pad_rows(p["rwkv_w_up"], S_WD), _pad_rows(p["rwkv_a_up"], S_AD), _pad_rows(p["rwkv_g_up"], S_GD),
        v_up, _pad_rows(p["gla_gate_up"], S_GLA),
        p["rwkv_w0"].reshape(1, -1), p["rwkv_a0"].reshape(1, -1), v0.reshape(1, -1),
        p["rwkv_k_k"].reshape(1, -1), p["rwkv_k_a"].reshape(1, -1), p["gla_gate_bias"].reshape(1, -1),
        v_first if vres is not None else None)
    if vres is None:
        v_first = v_
    o_bb = _gla(cols, la_, p["gla_out_norm"], bsz, seq)
    o_c = _rwkv_scan(r_, lw_, k_, v_, nk_, kb_, g_, p["rwkv_r_k"].reshape(-1), p["rwkv_ln_w"], p["rwkv_ln_b"],
                     bsz, seq)

    x = _out_proj([o_a, o_bb, o_c, o_dd], p["w_out"], x, mod3, 2)
    h2 = _norm_mod(x, mod3, 4, 3)
    act = _ffn_up(h2, p["ffn_w1"], p["ffn_w3"], layer_idx)
    x = _ffn_down(act, p["ffn_w2"], x, mod3, 5)
    return x, v_first


def kernel(x, c, rel_bias, ada_w, ada_b, w_in, w_out, diff_q_norm, diff_k_norm, diff_lambda, diff_subln,
           gla_gate_up, gla_gate_bias, gla_out_norm, rwkv_mu, rwkv_w_up, rwkv_w0, rwkv_a_up, rwkv_a0,
           rwkv_g_up, rwkv_k_k, rwkv_k_a, rwkv_r_k, rwkv_ln_w, rwkv_ln_b, rwkv_vres_down, rwkv_vres_mu,
           rwkv_vres_up, rwkv_v0, swa_q_norm, swa_k_norm, swa_sinks, ffn_w1, ffn_w3, ffn_w2):
    bsz, seq, _ = x.shape
    depth = ada_w.shape[0]
    attn_tile = min(1024, seq)
    bias_ad = _bias_tiles(rel_bias[:, :DIFF_HEADS], attn_tile, attn_tile, 0, None, True, True, LOG2E)
    bias_as = _bias_tiles(rel_bias[:, :DIFF_HEADS], attn_tile, attn_tile, attn_tile, None, True, True, LOG2E)
    bias_d = _bias_tiles(rel_bias[:, DIFF_HEADS:], SWA_BLOCK, 2 * SWA_BLOCK, SWA_BLOCK, SWA_BLOCK, False)
    c8 = jnp.zeros((8, c.shape[1]), F32).at[:bsz].set(c)
    v_first = None
    for l in range(depth):
        p = dict(ada_w=ada_w, ada_b=ada_b, w_in=w_in[l], w_out=_cast_bf16(w_out, l),
                 diff_q_norm=diff_q_norm[l], diff_k_norm=diff_k_norm[l], diff_lambda=diff_lambda[l],
                 diff_subln=diff_subln[l], gla_gate_up=gla_gate_up[l], gla_gate_bias=gla_gate_bias[l],
                 gla_out_norm=gla_out_norm[l], rwkv_mu=rwkv_mu[l], rwkv_w_up=rwkv_w_up[l],
                 rwkv_w0=rwkv_w0[l], rwkv_a_up=rwkv_a_up[l], rwkv_a0=rwkv_a0[l], rwkv_g_up=rwkv_g_up[l],
                 rwkv_k_k=rwkv_k_k[l], rwkv_k_a=rwkv_k_a[l], rwkv_r_k=rwkv_r_k[l], rwkv_ln_w=rwkv_ln_w[l],
                 rwkv_ln_b=rwkv_ln_b[l], swa_q_norm=swa_q_norm[l], swa_k_norm=swa_k_norm[l],
                 swa_sinks=swa_sinks[l], ffn_w1=ffn_w1, ffn_w3=ffn_w3, ffn_w2=_cast_bf16(ffn_w2, l))
        vres = None if l == 0 else (rwkv_vres_down[l - 1], rwkv_vres_mu[l - 1], rwkv_vres_up[l - 1],
                                    rwkv_v0[l - 1])
        x, v_first = _layer(x, c8, l, v_first, vres, bias_ad, bias_as, bias_d, p, attn_tile)
    return x
```

```python
import functools
import math

import jax
import jax.numpy as jnp
from jax import lax
from jax.experimental import pallas as pl
from jax.experimental.pallas import tpu as pltpu

F32 = jnp.float32
BF16 = jnp.bfloat16
HI = lax.Precision.HIGHEST

D_MODEL = 4096
GROUP = 1024
D_FF = 11008
NORM_EPS = 1e-6
NEG_INF = -1e30
LANE = 128
HALF = 64

DIFF_HEADS = 8
DIFF_QK = 64
GLA_HEADS = 4
GLA_DK = 128
GLA_DV = 256
GLA_RANK = 16
GLA_NORMALIZER = 16.0
CHUNK = 64
SCAN_GROUP = 8
RWKV_GROUP = 16
RWKV_LN_EPS = 64e-5
SWA_HEADS = 16
SWA_KV_HEADS = 2
SWA_BLOCK = 128
REL_BUCKETS = 32
REL_MAX_DIST = 128

OFF_A = 0
OFF_B = 3072
OFF_C = 6144
OFF_D = 9216
OFF_S = 10496
SMALL = 256
NCOLS = OFF_S + SMALL
S_GLA, S_WD, S_AD, S_GD, S_VR = 0, 16, 80, 144, 208
TAIL = 512
OFF_T = NCOLS - TAIL

VMEM_LIMIT = 56 * 1024 * 1024


def _cparams(sem, vmem=VMEM_LIMIT):
    return pltpu.CompilerParams(dimension_semantics=sem, vmem_limit_bytes=vmem)


def _dot(a, b, prec=None):
    return jnp.dot(a, b, preferred_element_type=F32, precision=prec)


def _dot_nt(a, b, prec=None):
    return lax.dot_general(a, b, (((1,), (1,)), ((), ())), preferred_element_type=F32, precision=prec)


def _dot_tn(a, b, prec=None):
    return lax.dot_general(a, b, (((0,), (0,)), ((), ())), preferred_element_type=F32, precision=prec)


def _sigmoid(z):
    return 1.0 / (1.0 + jnp.exp(-z))


def _softplus(z):
    return jnp.maximum(z, 0.0) + jnp.log(1.0 + jnp.exp(-jnp.abs(z)))


def _seg_ones():
    r = lax.broadcasted_iota(jnp.int32, (LANE, LANE), 0) // HALF
    c = lax.broadcasted_iota(jnp.int32, (LANE, LANE), 1) // HALF
    return (r == c).astype(BF16)


def _split(x):
    hi = x.astype(BF16)
    return hi, (x - hi.astype(F32)).astype(BF16)


def _dot_left2(x, w):
    hi, lo = _split(x)
    return _dot(hi, w) + _dot(lo, w)


def _tri_cumsum(tri, x):
    hi, lo = _split(x)
    return _dot(tri, hi) + _dot(tri, lo)


def _segsum(x, ones):
    return _dot_left2(x, ones)


def _ada_body(c_ref, w_ref, b_ref, o_ref):
    c = c_ref[...]
    s = (c * _sigmoid(c)).astype(BF16)
    o_ref[...] = _dot(s, w_ref[...].astype(BF16)) + b_ref[...]


def _ada(c8, ada_w, ada_b, layer, tn=512):
    _, d, n = ada_w.shape
    return pl.pallas_call(
        _ada_body,
        out_shape=jax.ShapeDtypeStruct((8, n), F32),
        grid=(n // tn,),
        in_specs=[pl.BlockSpec((8, d), lambda j: (0, 0)),
                  pl.BlockSpec((None, d, tn), lambda j: (layer, 0, j)),
                  pl.BlockSpec((1, tn), lambda j: (0, j))],
        out_specs=pl.BlockSpec((8, tn), lambda j: (0, j)),
        compiler_params=_cparams(("parallel",)),
        name="ada_mod",
    )(c8, ada_w, ada_b[layer].reshape(1, n))


def _cast_body(w_ref, o_ref):
    o_ref[...] = w_ref[...].astype(o_ref.dtype)


def _cast_bf16(w, layer, tr=256):
    _, rows, cols = w.shape
    return pl.pallas_call(
        _cast_body,
        out_shape=jax.ShapeDtypeStruct((rows, cols), BF16),
        grid=(rows // tr,),
        in_specs=[pl.BlockSpec((None, tr, cols), lambda i: (layer, i, 0))],
        out_specs=pl.BlockSpec((tr, cols), lambda i: (i, 0)),
        compiler_params=_cparams(("parallel",)),
        name="weight_cast",
    )(w)


P_IN = 10704
SRC_B, SRC_C, SRC_D = 3072, 6160, 9424
SRC_BS, SRC_CS = SRC_B + 3072, SRC_C + 3072


def _regroup_body(w_ref, vr_ref, o_ref):
    def take(c0, c1):
        base = (c0 // LANE) * LANE
        tile = w_ref[:, base:min(base + LANE, P_IN)]
        return tile[:, c0 - base:c1 - base]

    def window(c0, width):
        parts, c = [], c0
        while c < c0 + width:
            nxt = min((c // LANE + 1) * LANE, c0 + width)
            parts.append(take(c, nxt))
            c = nxt
        return parts[0] if len(parts) == 1 else jnp.concatenate(parts, axis=1)

    o_ref[:, :OFF_C] = w_ref[:, :OFF_C].astype(o_ref.dtype)
    for dst, src, width in ((OFF_C, SRC_C, 3 * GROUP), (OFF_D, SRC_D, GROUP + 2 * LANE)):
        for t in range(width // LANE):
            o_ref[:, dst + t * LANE:dst + (t + 1) * LANE] = window(src + t * LANE, LANE).astype(o_ref.dtype)
    rows = w_ref.shape[0]
    small = jnp.concatenate([window(SRC_BS, GLA_RANK), window(SRC_CS, 192), vr_ref[...],
                             jnp.zeros((rows, SMALL - S_VR - 32), F32)], axis=1)
    o_ref[:, OFF_S:] = small.astype(o_ref.dtype)


def _regroup_w_in(w_in, layer, vres_down, tr=128):
    _, d, _ = w_in.shape
    return pl.pallas_call(
        _regroup_body,
        out_shape=jax.ShapeDtypeStruct((d, NCOLS), BF16),
        grid=(d // tr,),
        in_specs=[pl.BlockSpec((None, tr, P_IN), lambda i: (layer, i, 0)),
                  pl.BlockSpec((tr, 32), lambda i: (i, 0))],
        out_specs=pl.BlockSpec((tr, NCOLS), lambda i: (i, 0)),
        compiler_params=_cparams(("parallel",)),
        name="w_in_regroup",
    )(w_in, vres_down)


def _norm_body(x_ref, sc_ref, sh_ref, o_ref):
    x = x_ref[0]
    ms = jnp.mean(x * x, axis=-1, keepdims=True)
    h = x * lax.rsqrt(ms + NORM_EPS) * (1.0 + sc_ref[0]) + sh_ref[0]
    o_ref[...] = h.astype(o_ref.dtype)


def _norm_mod(x, mod3, sc_idx, sh_idx, ts=256):
    bsz, seq, d = x.shape
    ts = min(ts, seq)
    ns = seq // ts
    return pl.pallas_call(
        _norm_body,
        out_shape=jax.ShapeDtypeStruct((bsz * seq, d), BF16),
        grid=(bsz, ns),
        in_specs=[pl.BlockSpec((1, ts, d), lambda b, i: (b, i, 0)),
                  pl.BlockSpec((1, 1, d), lambda b, i: (b * 6 + sc_idx, 0, 0)),
                  pl.BlockSpec((1, 1, d), lambda b, i: (b * 6 + sh_idx, 0, 0))],
        out_specs=pl.BlockSpec((ts, d), lambda b, i: (b * ns + i, 0)),
        compiler_params=_cparams(("parallel", "parallel")),
        name="norm_mod",
    )(x, mod3, mod3)


def _in_proj_body(a_ref, b_ref, main_ref, tail_ref):
    j = pl.program_id(1)
    acc = _dot(a_ref[...], b_ref[...])

    @pl.when(j < pl.num_programs(1) - 1)
    def _():
        main_ref[...] = acc.astype(main_ref.dtype)

    @pl.when(j == pl.num_programs(1) - 1)
    def _():
        tail_ref[...] = acc


def _in_proj(a, b, tm=1024):
    m, k = a.shape
    tm = min(tm, m)
    n_main = OFF_T // TAIL
    return pl.pallas_call(
        _in_proj_body,
        out_shape=[jax.ShapeDtypeStruct((m, OFF_T), BF16), jax.ShapeDtypeStruct((m, TAIL), F32)],
        grid=(m // tm, n_main + 1),
        in_specs=[pl.BlockSpec((tm, k), lambda i, j: (i, 0)),
                  pl.BlockSpec((k, TAIL), lambda i, j: (0, j))],
        out_specs=[pl.BlockSpec((tm, TAIL), lambda i, j: (i, jnp.minimum(j, n_main - 1))),
                   pl.BlockSpec((tm, TAIL), lambda i, j: (i, 0))],
        compiler_params=_cparams(("parallel", "arbitrary")),
        name="in_proj",
    )(a, b)


def _outproj_body(a0, a1, a2, a3, w_ref, x_ref, g_ref, o_ref):
    acc = _dot(a0[...], w_ref[0 * GROUP:1 * GROUP, :])
    acc += _dot(a1[...], w_ref[1 * GROUP:2 * GROUP, :])
    acc += _dot(a2[...], w_ref[2 * GROUP:3 * GROUP, :])
    acc += _dot(a3[...], w_ref[3 * GROUP:4 * GROUP, :])
    o_ref[0] = x_ref[0] + g_ref[0] * acc


def _out_proj(parts, w, x, mod3, gate_idx, tm=1024, tn=512):
    bsz, seq, d = x.shape
    tm = min(tm, seq)
    ns = seq // tm
    a_spec = pl.BlockSpec((tm, GROUP), lambda b, i, j: (b * ns + i, 0))
    return pl.pallas_call(
        _outproj_body,
        out_shape=jax.ShapeDtypeStruct((bsz, seq, d), F32),
        grid=(bsz, ns, d // tn),
        in_specs=[a_spec, a_spec, a_spec, a_spec,
                  pl.BlockSpec((4 * GROUP, tn), lambda b, i, j: (0, j)),
                  pl.BlockSpec((1, tm, tn), lambda b, i, j: (b, i, j)),
                  pl.BlockSpec((1, 1, tn), lambda b, i, j: (b * 6 + gate_idx, 0, j))],
        out_specs=pl.BlockSpec((1, tm, tn), lambda b, i, j: (b, i, j)),
        compiler_params=_cparams(("parallel", "parallel", "parallel")),
        name="out_proj",
    )(*parts, w, x, mod3)


def _ffn_up_body(a_ref, w1_ref, w3_ref, o_ref):
    a = a_ref[...]
    u = _dot(a, w1_ref[...].astype(BF16))
    v = _dot(a, w3_ref[...].astype(BF16))
    o_ref[...] = (u * _sigmoid(u) * v).astype(o_ref.dtype)


def _ffn_up(h, w1, w3, layer, tm=1024, tn=256):
    m, k = h.shape
    n = w1.shape[2]
    tm = min(tm, m)
    return pl.pallas_call(
        _ffn_up_body,
        out_shape=jax.ShapeDtypeStruct((m, n), BF16),
        grid=(m // tm, n // tn),
        in_specs=[pl.BlockSpec((tm, k), lambda i, j: (i, 0)),
                  pl.BlockSpec((None, k, tn), lambda i, j: (layer, 0, j)),
                  pl.BlockSpec((None, k, tn), lambda i, j: (layer, 0, j))],
        out_specs=pl.BlockSpec((tm, tn), lambda i, j: (i, j)),
        compiler_params=_cparams(("parallel", "parallel")),
        name="ffn_up",
    )(h, w1, w3)


def _ffn_down_body(a_ref, w_ref, x_ref, g_ref, o_ref):
    o_ref[0] = x_ref[0] + g_ref[0] * _dot(a_ref[...], w_ref[...])


def _ffn_down(a, w, x, mod3, gate_idx, tm=512, tn=512):
    bsz, seq, d = x.shape
    k = a.shape[1]
    tm = min(tm, seq)
    ns = seq // tm
    return pl.pallas_call(
        _ffn_down_body,
        out_shape=jax.ShapeDtypeStruct((bsz, seq, d), F32),
        grid=(bsz, ns, d // tn),
        in_specs=[pl.BlockSpec((tm, k), lambda b, i, j: (b * ns + i, 0)),
                  pl.BlockSpec((k, tn), lambda b, i, j: (0, j)),
                  pl.BlockSpec((1, tm, tn), lambda b, i, j: (b, i, j)),
                  pl.BlockSpec((1, 1, tn), lambda b, i, j: (b * 6 + gate_idx, 0, j))],
        out_specs=pl.BlockSpec((1, tm, tn), lambda b, i, j: (b, i, j)),
        compiler_params=_cparams(("parallel", "parallel", "parallel")),
        name="ffn_down",
    )(a, w, x, mod3)


def _bias_body(tab_ref, o_ref, *, off, window, shift, key_major, scale):
    h = pl.program_id(0)
    rows, cols = o_ref.shape[1], o_ref.shape[2]
    qi = lax.broadcasted_iota(jnp.int32, (rows, cols), 1 if key_major else 0)
    kj = lax.broadcasted_iota(jnp.int32, (rows, cols), 0 if key_major else 1)
    dist = off + qi - kj
    max_exact = REL_BUCKETS // 2
    n = jnp.maximum(dist, 0)
    nf = jnp.maximum(n, 1).astype(F32)
    large = max_exact + (jnp.log(nf / max_exact) / math.log(REL_MAX_DIST / max_exact)
                         * (REL_BUCKETS - max_exact)).astype(jnp.int32)
    large = jnp.minimum(large, REL_BUCKETS - 1)
    bucket = jnp.where(n < max_exact, n, large)
    last = tab_ref[REL_BUCKETS - 1, h]
    bias = jnp.full((rows, cols), last, F32)
    for b in range(REL_BUCKETS - 1):
        bias = jnp.where(bucket == b, tab_ref[b, h], bias)
    if shift:
        bias = bias - last
    if scale != 1.0:
        bias = bias * scale
    valid = dist >= 0
    if window is not None:
        valid = jnp.logical_and(valid, dist < window)
    o_ref[0] = jnp.where(valid, bias, NEG_INF)


def _bias_tiles(table, rows, cols, off, window, shift, key_major=False, scale=1.0):
    nh = table.shape[1]
    return pl.pallas_call(
        functools.partial(_bias_body, off=off, window=window, shift=shift, key_major=key_major, scale=scale),
        out_shape=jax.ShapeDtypeStruct((nh, rows, cols), F32),
        grid=(nh,),
        in_specs=[pl.BlockSpec(memory_space=pltpu.SMEM)],
        out_specs=pl.BlockSpec((1, rows, cols), lambda h: (h, 0, 0)),
        compiler_params=_cparams(("parallel",)),
        name="rel_bias_tiles",
    )(table)


LOG2E = 1.4426950408889634
VT_ROWS = LANE + 16
Q_CHUNK = 512


def _prep_a_body(c_ref, qg_ref, kg_ref, qk_ref, vt_ref):
    ones = _seg_ones()
    scale = DIFF_QK ** -0.5 * LOG2E
    tm = c_ref.shape[0]
    for j in range(2 * DIFF_HEADS):
        x = c_ref[:, j * LANE:(j + 1) * LANE].astype(F32)
        ms = _segsum(x * x, ones) * (1.0 / DIFF_QK)
        gain = qg_ref[...] * scale if j < DIFF_HEADS else kg_ref[...]
        qk_ref[:, j * LANE:(j + 1) * LANE] = (x * lax.rsqrt(ms + NORM_EPS) * gain).astype(qk_ref.dtype)
    for h in range(DIFF_HEADS):
        v = c_ref[:, 2 * GROUP + h * LANE:2 * GROUP + (h + 1) * LANE].astype(F32)
        vt_ref[h * VT_ROWS:h * VT_ROWS + LANE, :] = v.T.astype(vt_ref.dtype)
        vt_ref[h * VT_ROWS + LANE:(h + 1) * VT_ROWS, :] = jnp.ones((VT_ROWS - LANE, tm), vt_ref.dtype)


def _prep_a(cols, q_gain, k_gain, tm=256):
    t = cols.shape[0]
    tm = min(tm, t)
    qg = jnp.tile(q_gain, 2).reshape(1, LANE)
    kg = jnp.tile(k_gain, 2).reshape(1, LANE)
    vec = pl.BlockSpec((1, LANE), lambda i: (0, 0))
    return pl.pallas_call(
        _prep_a_body,
        out_shape=[jax.ShapeDtypeStruct((t, 2 * GROUP), BF16),
                   jax.ShapeDtypeStruct((DIFF_HEADS * VT_ROWS, t), BF16)],
        grid=(t // tm,),
        in_specs=[pl.BlockSpec((tm, 3 * GROUP), lambda i: (i, OFF_A // (3 * GROUP))), vec, vec],
        out_specs=[pl.BlockSpec((tm, 2 * GROUP), lambda i: (i, 0)),
                   pl.BlockSpec((DIFF_HEADS * VT_ROWS, tm), lambda i: (0, i))],
        compiler_params=_cparams(("parallel",)),
        name="diff_prep",
    )(cols, qg, kg)


def _flash_body(it_ref, jt_ref, q_ref, k_ref, vt_ref, bd_ref, bs_ref, lam_ref, sub_ref, o_ref,
                qlo, qhi, m1, a1, m2, a2, *, lam_init):
    t = pl.program_id(2)
    i = it_ref[t]
    j = jt_ref[t]

    @pl.when(j == 0)
    def _():
        q = q_ref[...]
        lane = lax.broadcasted_iota(jnp.int32, q.shape, 1)
        zero = jnp.zeros_like(q)
        qlo[...] = jnp.where(lane < HALF, q, zero)
        qhi[...] = jnp.where(lane >= HALF, q, zero)
        for m, a in ((m1, a1), (m2, a2)):
            m[...] = jnp.full(m.shape, NEG_INF, F32)
            a[...] = jnp.zeros(a.shape, F32)

    def step(bias):
        k = k_ref[...]
        vt = vt_ref[...]
        nq = q_ref.shape[0]
        qc = min(Q_CHUNK, nq)
        chains = [(qq, m, a, slice(c * qc, (c + 1) * qc))
                  for qq, m, a in ((qlo, m1, a1), (qhi, m2, a2)) for c in range(nq // qc)]
        ahead = 2
        if bias is None:
            scores = [_dot_nt(k, qq[qs, :]) for qq, _, _, qs in chains[:ahead]]
        else:
            full = [_dot_nt(k, qq[...]) for qq in (qlo, qhi)]
            scores = [full[i // (nq // qc)][:, chains[i][3]] + bias[:, chains[i][3]] for i in range(len(chains))]
        for idx, (_, m, a, qs) in enumerate(chains):
            if bias is None and idx + ahead < len(chains):
                qq_n, _, _, qs_n = chains[idx + ahead]
                scores.append(_dot_nt(k, qq_n[qs_n, :]))
            s = scores[idx]
            m_old = m[:, qs]
            m_new = jnp.maximum(m_old, jnp.max(s, axis=0, keepdims=True))
            p = jnp.exp2(s - m_new).astype(BF16)
            a[:, qs] = jnp.exp2(m_old - m_new) * a[:, qs] + _dot(vt, p)
            m[:, qs] = m_new

    @pl.when(j < i - 1)
    def _():
        step(None)

    @pl.when(j == i - 1)
    def _():
        step(bs_ref[0])

    @pl.when(j == i)
    def _():
        step(bd_ref[0])
        lam = lam_ref[...]
        e1 = jnp.exp(jnp.sum(lam[0:1] * lam[1:2], axis=-1, keepdims=True))
        e2 = jnp.exp(jnp.sum(lam[2:3] * lam[3:4], axis=-1, keepdims=True))
        lam_full = e1 - e2 + lam_init
        o = a1[:LANE, :] / a1[LANE:LANE + 1, :] - lam_full * (a2[:LANE, :] / a2[LANE:LANE + 1, :])
        ms = jnp.mean(o * o, axis=0, keepdims=True)
        o = o * lax.rsqrt(ms + NORM_EPS) * sub_ref[...] * (1.0 - lam_init)
        o_ref[...] = o.T.astype(o_ref.dtype)


def _diff_attention(qk, vt, bias_diag, bias_sub, lam, subln, bsz, seq, layer_idx, tile):
    nq = seq // tile
    lam_init = 0.8 - 0.6 * math.exp(-0.3 * layer_idx)
    pairs = [(i, j) for i in range(nq) for j in range(i + 1)]
    i_tab = jnp.array([ij[0] for ij in pairs], jnp.int32)
    j_tab = jnp.array([ij[1] for ij in pairs], jnp.int32)
    grid_spec = pltpu.PrefetchScalarGridSpec(
        num_scalar_prefetch=2,
        grid=(bsz, DIFF_HEADS, len(pairs)),
        in_specs=[pl.BlockSpec((tile, LANE), lambda b, h, t, it, jt: (b * nq + it[t], h)),
                  pl.BlockSpec((tile, LANE), lambda b, h, t, it, jt: (b * nq + jt[t], DIFF_HEADS + h)),
                  pl.BlockSpec((VT_ROWS, tile), lambda b, h, t, it, jt: (h, b * nq + jt[t])),
                  pl.BlockSpec((1, tile, tile), lambda b, h, t, it, jt: (h, 0, 0)),
                  pl.BlockSpec((1, tile, tile), lambda b, h, t, it, jt: (h, 0, 0)),
                  pl.BlockSpec((4, DIFF_QK), lambda b, h, t, it, jt: (0, 0)),
                  pl.BlockSpec((LANE, 1), lambda b, h, t, it, jt: (0, 0))],
        out_specs=pl.BlockSpec((tile, LANE), lambda b, h, t, it, jt: (b * nq + it[t], h)),
        scratch_shapes=[pltpu.VMEM((tile, LANE), BF16), pltpu.VMEM((tile, LANE), BF16),
                        pltpu.VMEM((1, tile), F32), pltpu.VMEM((VT_ROWS, tile), F32),
                        pltpu.VMEM((1, tile), F32), pltpu.VMEM((VT_ROWS, tile), F32)])
    return pl.pallas_call(
        functools.partial(_flash_body, lam_init=lam_init),
        out_shape=jax.ShapeDtypeStruct((bsz * seq, GROUP), BF16),
        grid_spec=grid_spec,
        compiler_params=_cparams(("parallel", "parallel", "arbitrary")),
        name="diff_attention",
    )(i_tab, j_tab, qk, qk, vt, bias_diag, bias_sub, lam, subln.reshape(LANE, 1))


def _swa_body(q_ref, kp_ref, kc_ref, vp_ref, vc_ref, bias_ref, qg_ref, kg_ref, sink_ref, o_ref):
    n = pl.program_id(1)
    g = pl.program_id(2)
    ones = _seg_ones()
    lane = lax.broadcasted_iota(jnp.int32, (SWA_BLOCK, LANE), 1)
    lo = lane < HALF

    def norm(x, gain):
        ms = _segsum(x * x, ones) * (1.0 / HALF)
        return x * lax.rsqrt(ms + NORM_EPS) * gain

    k = jnp.concatenate([norm(kp_ref[...], kg_ref[...]), norm(kc_ref[...], kg_ref[...])], axis=0)
    v = jnp.concatenate([vp_ref[...], vc_ref[...]], axis=0)
    k_sw = pltpu.roll(k, HALF, axis=1)
    v_sw = pltpu.roll(v, HALF, axis=1)
    first_head = g == 0
    lo2 = lax.broadcasted_iota(jnp.int32, (2 * SWA_BLOCK, LANE), 1) < HALF
    k_a = jnp.where(first_head, k, k_sw).astype(BF16)
    k_b = jnp.where(first_head, k_sw, k).astype(BF16)
    v_mine_lo = jnp.where(first_head, v, v_sw)
    v_mine_hi = jnp.where(first_head, v_sw, v)
    v_a = jnp.where(lo2, v_mine_lo, 0.0).astype(BF16)
    v_b = jnp.where(lo2, 0.0, v_mine_hi).astype(BF16)
    kcol = lax.broadcasted_iota(jnp.int32, (SWA_BLOCK, 2 * SWA_BLOCK), 1)
    pad = jnp.logical_and(n == 0, kcol < SWA_BLOCK)
    scale = HALF ** -0.5
    grp = SWA_HEADS // SWA_KV_HEADS
    heads = range(grp)
    q = [norm(q_ref[:, pr * LANE:(pr + 1) * LANE].astype(F32), qg_ref[...] * scale) for pr in range(grp // 2)]
    qm = [jnp.where(lo if hh % 2 == 0 else jnp.logical_not(lo), q[hh // 2], 0.0).astype(BF16) for hh in heads]
    s = [_dot_nt(qm[hh], k_a if hh % 2 == 0 else k_b) for hh in heads]
    s = [jnp.where(pad, NEG_INF, s[hh] + bias_ref[hh]) for hh in heads]
    sink = [sink_ref[g * grp + hh] for hh in heads]
    m = [jnp.maximum(jnp.max(s[hh], axis=-1, keepdims=True), sink[hh]) for hh in heads]
    p = [jnp.exp(s[hh] - m[hh]) for hh in heads]
    den = [jnp.sum(p[hh], axis=-1, keepdims=True) + jnp.exp(sink[hh] - m[hh]) for hh in heads]
    pv = [_dot(p[hh].astype(BF16), v_a if hh % 2 == 0 else v_b) for hh in heads]
    for pr in range(grp // 2):
        o = pv[2 * pr] / den[2 * pr] + pv[2 * pr + 1] / den[2 * pr + 1]
        o_ref[:, pr * LANE:(pr + 1) * LANE] = o.astype(o_ref.dtype)


def _swa_attention(cols, tail, bias, q_gain, k_gain, sinks, bsz, seq):
    nb = seq // SWA_BLOCK
    grp = SWA_HEADS // SWA_KV_HEADS
    qw = grp * HALF
    kcol = (OFF_D + GROUP - OFF_T) // LANE
    vcol = kcol + 1
    prev = lambda b, n, g: b * nb + jnp.maximum(n - 1, 0)
    cur = lambda b, n, g: b * nb + n
    vec = pl.BlockSpec((1, LANE), lambda b, n, g: (0, 0))
    return pl.pallas_call(
        _swa_body,
        out_shape=jax.ShapeDtypeStruct((bsz * seq, GROUP), BF16),
        grid=(bsz, nb, SWA_KV_HEADS),
        in_specs=[pl.BlockSpec((SWA_BLOCK, qw), lambda b, n, g: (cur(b, n, g), OFF_D // qw + g)),
                  pl.BlockSpec((SWA_BLOCK, LANE), lambda b, n, g: (prev(b, n, g), kcol)),
                  pl.BlockSpec((SWA_BLOCK, LANE), lambda b, n, g: (cur(b, n, g), kcol)),
                  pl.BlockSpec((SWA_BLOCK, LANE), lambda b, n, g: (prev(b, n, g), vcol)),
                  pl.BlockSpec((SWA_BLOCK, LANE), lambda b, n, g: (cur(b, n, g), vcol)),
                  pl.BlockSpec((grp, SWA_BLOCK, 2 * SWA_BLOCK), lambda b, n, g: (g, 0, 0)),
                  vec, vec,
                  pl.BlockSpec(memory_space=pltpu.SMEM)],
        out_specs=pl.BlockSpec((SWA_BLOCK, qw), lambda b, n, g: (cur(b, n, g), g)),
        compiler_params=_cparams(("parallel", "parallel", "parallel")),
        name="swa_attention",
    )(cols, tail, tail, tail, tail, bias,
      jnp.tile(q_gain, 2).reshape(1, LANE), jnp.tile(k_gain, 2).reshape(1, LANE), sinks)


def _prep_c_body(*refs, seq, with_vres):
    (cm_ref, cs_ref, pm_ref, ps_ref, mum_ref, mus_ref, wup_ref, aup_ref, gup_ref, vup_ref, glaup_ref,
     w0_ref, a0_ref, v0_ref, kk_ref, ka_ref, glab_ref) = refs[:17]
    rest = refs[17:]
    if with_vres:
        vf_ref, rest = rest[0], rest[1:]
    r_ref, lw_ref, k_ref, v_ref, nk_ref, kb_ref, g_ref, la_ref = rest
    tm = cm_ref.shape[0]
    i = pl.program_id(0)
    seq_start = (i * tm) % seq == 0

    def shifted(cur, prev_rows, mu):
        row = lax.broadcasted_iota(jnp.int32, cur.shape, 0)
        before = jnp.where(seq_start, 0.0, prev_rows[PREV_ROWS - 1:PREV_ROWS, :].astype(F32))
        prev = jnp.where(row == 0, before, pltpu.roll(cur, 1, axis=0))
        return cur + (prev - cur) * mu

    def low_rank(x, w2_ref):
        hi, lo = _split(x)
        return _dot(hi, w2_ref[0]) + _dot(lo, w2_ref[0]) + _dot(hi, w2_ref[1])

    cs = cs_ref[...]
    la_ref[...] = -_softplus(-(low_rank(cs, glaup_ref) + glab_ref[...])) * (1.0 / GLA_NORMALIZER)
    sm = shifted(cm_ref[...].astype(F32), pm_ref, mum_ref[...])
    ss = shifted(cs, ps_ref, mus_ref[...])
    r = sm[:, :GROUP]
    k = sm[:, GROUP:2 * GROUP]
    v = sm[:, 2 * GROUP:]
    if with_vres:
        gate = _sigmoid(v0_ref[...] + low_rank(ss, vup_ref))
        v = v + (vf_ref[...] - v) * gate
    w_log = -_softplus(-(w0_ref[...] + low_rank(jnp.tanh(ss), wup_ref))) - 0.5
    a = _sigmoid(a0_ref[...] + low_rank(ss, aup_ref))
    r_ref[...] = r.astype(r_ref.dtype)
    lw_ref[...] = -jnp.exp(w_log)
    v_ref[...] = v
    g_ref[...] = low_rank(_sigmoid(ss), gup_ref).astype(g_ref.dtype)
    k_ref[...] = (k * (1.0 + (a - 1.0) * ka_ref[...])).astype(k_ref.dtype)
    ones = _seg_ones()
    kk = k * kk_ref[...]
    for j in range(GROUP // LANE):
        sl = slice(j * LANE, (j + 1) * LANE)
        x = kk[:, sl]
        nrm = jnp.maximum(jnp.sqrt(_segsum(x * x, ones)), 1e-12)
        x = x / nrm
        nk_ref[:, sl] = x.astype(nk_ref.dtype)
        kb_ref[:, sl] = (x * a[:, sl]).astype(kb_ref.dtype)


PREV_ROWS = 16


def _prep_c(cols, tail, seq, mu_main, mu_small, w_up, a_up, g_up, v_up, gla_up, w0, a0, v0, k_k, k_a, gla_bias,
            v_first, tm=256):
    t = cols.shape[0]
    tm = min(tm, seq)
    with_vres = v_first is not None
    cmain = OFF_C // (3 * GROUP)
    csmall = (OFF_S - OFF_T) // SMALL
    prev = lambda i: jnp.maximum(i * (tm // PREV_ROWS) - 1, 0)
    full = lambda r, c: pl.BlockSpec((r, c), lambda i: (0, 0))
    hilo = lambda c: pl.BlockSpec((2, SMALL, c), lambda i: (0, 0, 0))
    in_specs = [pl.BlockSpec((tm, 3 * GROUP), lambda i: (i, cmain)),
                pl.BlockSpec((tm, SMALL), lambda i: (i, csmall)),
                pl.BlockSpec((PREV_ROWS, 3 * GROUP), lambda i: (prev(i), cmain)),
                pl.BlockSpec((PREV_ROWS, SMALL), lambda i: (prev(i), csmall)),
                full(1, 3 * GROUP), full(1, SMALL),
                hilo(GROUP), hilo(GROUP), hilo(GROUP), hilo(GROUP), hilo(GLA_HEADS * GLA_DK),
                full(1, GROUP), full(1, GROUP), full(1, GROUP), full(1, GROUP), full(1, GROUP),
                full(1, GLA_HEADS * GLA_DK)]
    args = [cols, tail, cols, tail, mu_main, mu_small, w_up, a_up, g_up, v_up, gla_up,
            w0, a0, v0, k_k, k_a, gla_bias]
    if with_vres:
        in_specs.append(pl.BlockSpec((tm, GROUP), lambda i: (i, 0)))
        args.append(v_first)
    row = pl.BlockSpec((tm, GROUP), lambda i: (i, 0))
    out = [jax.ShapeDtypeStruct((t, GROUP), dt) for dt in (BF16, F32, BF16, F32, BF16, BF16, BF16)]
    return pl.pallas_call(
        functools.partial(_prep_c_body, seq=seq, with_vres=with_vres),
        out_shape=out + [jax.ShapeDtypeStruct((t, GLA_HEADS * GLA_DK), F32)],
        grid=(t // tm,),
        in_specs=in_specs,
        out_specs=[row] * 7 + [pl.BlockSpec((tm, GLA_HEADS * GLA_DK), lambda i: (i, 0))],
        compiler_params=_cparams(("parallel",)),
        name="rwkv_gla_prep",
    )(*args)


def _gla_body(q_ref, k_ref, v_ref, g_ref, la_ref, gain_ref, o_ref, st_ref):
    @pl.when(pl.program_id(2) == 0)
    def _():
        st_ref[...] = jnp.zeros(st_ref.shape, F32)

    row = lax.broadcasted_iota(jnp.int32, (CHUNK, CHUNK), 0)
    col = lax.broadcasted_iota(jnp.int32, (CHUNK, CHUNK), 1)
    causal = row >= col
    tri = causal.astype(BF16)
    nchunk = q_ref.shape[0] // CHUNK

    def local(sls):
        n = range(len(sls))
        k = [k_ref[sl, :].astype(F32) for sl in sls]
        v = [v_ref[sl, :].astype(BF16) for sl in sls]
        b = [_tri_cumsum(tri, la_ref[sl, :]) for sl in sls]
        b_last = [b[i][CHUNK - 1:CHUNK, :] for i in n]
        q_dec = [(q_ref[sl, :].astype(F32) * (GLA_DK ** -0.5) * jnp.exp(b[i])).astype(BF16)
                 for i, sl in enumerate(sls)]
        a_intra = [jnp.where(causal, _dot_nt(q_dec[i], (k[i] * jnp.exp(-b[i])).astype(BF16)), 0.0) for i in n]
        upd = [_dot_tn(v[i], (k[i] * jnp.exp(b_last[i] - b[i])).astype(BF16)) for i in n]
        o_intra = [_dot(a_intra[i].astype(BF16), v[i]) for i in n]
        return [(q_dec[i], o_intra[i], upd[i], jnp.exp(b_last[i])) for i in n]

    def advance(sl, q_dec, o_intra, upd, dec):
        state = st_ref[...]
        s_hi, s_lo = _split(state)
        o = o_intra + _dot_nt(q_dec, s_hi) + _dot_nt(q_dec, s_lo)
        st_ref[...] = state * dec + upd
        ms = jnp.mean(o * o, axis=-1, keepdims=True)
        o = o * lax.rsqrt(ms + NORM_EPS) * gain_ref[...]
        gate = g_ref[sl, :].astype(F32)
        o_ref[sl, :] = (o * (gate * _sigmoid(gate))).astype(o_ref.dtype)

    def group(gi, carry):
        sls = [pl.ds(pl.multiple_of((gi * SCAN_GROUP + g) * CHUNK, CHUNK), CHUNK) for g in range(SCAN_GROUP)]
        for sl, part in zip(sls, local(sls)):
            advance(sl, *part)
        return carry

    lax.fori_loop(0, nchunk // SCAN_GROUP, group, 0)


def _gla(cols, log_a, out_gain, bsz, seq, tc=512):
    tc = min(tc, seq)
    ns = seq // tc
    qc = OFF_B // GLA_DK
    kc = qc + GLA_HEADS
    vc = (OFF_B + 2 * GLA_HEADS * GLA_DK) // GLA_DV
    gc = vc + GLA_HEADS
    rows = lambda b, h, i: b * ns + i
    return pl.pallas_call(
        _gla_body,
        out_shape=jax.ShapeDtypeStruct((bsz * seq, GROUP), BF16),
        grid=(bsz, GLA_HEADS, ns),
        in_specs=[pl.BlockSpec((tc, GLA_DK), lambda b, h, i: (rows(b, h, i), qc + h)),
                  pl.BlockSpec((tc, GLA_DK), lambda b, h, i: (rows(b, h, i), kc + h)),
                  pl.BlockSpec((tc, GLA_DV), lambda b, h, i: (rows(b, h, i), vc + h)),
                  pl.BlockSpec((tc, GLA_DV), lambda b, h, i: (rows(b, h, i), gc + h)),
                  pl.BlockSpec((tc, GLA_DK), lambda b, h, i: (rows(b, h, i), h)),
                  pl.BlockSpec((1, GLA_DV), lambda b, h, i: (0, 0))],
        out_specs=pl.BlockSpec((tc, GLA_DV), lambda b, h, i: (rows(b, h, i), h)),
        scratch_shapes=[pltpu.VMEM((GLA_DV, GLA_DK), F32)],
        compiler_params=_cparams(("parallel", "parallel", "arbitrary")),
        name="gla_scan",
    )(cols, cols, cols, cols, log_a, out_gain.reshape(1, GLA_DV))


def _rwkv_body(r_ref, lw_ref, k_ref, v_ref, nk_ref, kb_ref, g_ref, rk_ref, lnw_ref, lnb_ref, o_ref, st_ref):
    @pl.when(pl.program_id(2) == 0)
    def _():
        st_ref[...] = jnp.zeros(st_ref.shape, F32)

    two = 2 * CHUNK
    row = lax.broadcasted_iota(jnp.int32, (two, two), 0)
    col = lax.broadcasted_iota(jnp.int32, (two, two), 1)
    same = (row // CHUNK) == (col // CHUNK)
    strict = jnp.logical_and(same, (row % CHUNK) > (col % CHUNK))
    incl = jnp.logical_and(same, (row % CHUNK) >= (col % CHUNK))
    eye = (row == col).astype(F32)
    crow = lax.broadcasted_iota(jnp.int32, (CHUNK, CHUNK), 0)
    ccol = lax.broadcasted_iota(jnp.int32, (CHUNK, CHUNK), 1)
    tri = (crow >= ccol).astype(BF16)
    lo = lax.broadcasted_iota(jnp.int32, (CHUNK, LANE), 1) < HALF
    ones = _seg_ones()
    nchunk = r_ref.shape[0] // CHUNK

    def stack(x):
        return jnp.concatenate([jnp.where(lo, x, 0.0), jnp.where(lo, 0.0, x)], axis=0)

    def fold(x):
        return x[:CHUNK, :] + x[CHUNK:, :]

    def transitions(sls):
        n = range(len(sls))
        r = [r_ref[sl, :].astype(F32) for sl in sls]
        lw = [lw_ref[sl, :] for sl in sls]
        k = [k_ref[sl, :].astype(F32) for sl in sls]
        v = [v_ref[sl, :] for sl in sls]
        cum = [_tri_cumsum(tri, lw[i]) for i in n]
        last = [cum[i][CHUNK - 1:CHUNK, :] for i in n]
        a_bf, r_st, bk_st, bkh_st, v_st = [], [], [], [], []
        for i, sl in enumerate(sls):
            kb = kb_ref[sl, :].astype(F32)
            e_neg = jnp.exp(-cum[i])
            e_rem = jnp.exp(last[i] - cum[i])
            a_bf.append(stack(-nk_ref[sl, :].astype(F32) * jnp.exp(cum[i] - lw[i])).astype(BF16))
            r_st.append(stack(r[i] * jnp.exp(cum[i])))
            bk_st.append(jnp.concatenate([stack(kb * e_neg), stack(k[i] * e_neg)], axis=0).astype(BF16))
            bkh_st.append(jnp.concatenate([stack(kb * e_rem), stack(k[i] * e_rem)], axis=0).astype(BF16))
            v_st.append(stack(v[i]).astype(BF16))
        sc = [_dot_nt(jnp.concatenate([a_bf[i], r_st[i].astype(BF16)], axis=0), bk_st[i]) for i in n]
        a_ab = [jnp.where(strict, sc[i][:two, :two], 0.0) for i in n]
        a_ak = [jnp.where(strict, sc[i][:two, two:], 0.0).astype(BF16) for i in n]
        r_b = [jnp.where(incl, sc[i][two:, :two], 0.0).astype(BF16) for i in n]
        r_k = [jnp.where(incl, sc[i][two:, two:], 0.0).astype(BF16) for i in n]
        akv = [_dot(a_ak[i], v_st[i]).astype(BF16) for i in n]
        rkv = [_dot(r_k[i], v_st[i]) for i in n]
        inv = [eye + a_ab[i] for i in n]
        pw = [a_ab[i].astype(BF16) for i in n]
        for _ in range(5):
            pw = [_dot(pw[i], pw[i]).astype(BF16) for i in n]
            inv = [inv[i] + _dot(inv[i].astype(BF16), pw[i]) for i in n]
        pu = [_dot(inv[i].astype(BF16), jnp.concatenate([a_bf[i], akv[i]], axis=1)).astype(BF16) for i in n]
        rb_pu = [_dot(r_b[i], pu[i]) for i in n]
        m = [_dot_tn(pu[i][:, :LANE], bkh_st[i][:two, :]).astype(BF16) for i in n]
        n0 = [_dot_tn(jnp.concatenate([pu[i][:, LANE:], v_st[i]], axis=0), bkh_st[i]) for i in n]
        rkr = [_segsum(r[i] * k[i] * rk_ref[...], ones) for i in n]
        return [(fold(r_st[i] + rb_pu[i][:, :LANE]).astype(BF16), fold(rb_pu[i][:, LANE:] + rkv[i]),
                 m[i], n0[i], jnp.exp(last[i]), rkr[i] * v[i]) for i in n]

    def advance(sl, p2, y0, m, n0, dec, bonus):
        state = st_ref[...]
        s_hi, s_lo = _split(state)
        y = _dot_nt(p2, s_hi) + y0
        st_ref[...] = state * dec + _dot(s_hi, m) + _dot(s_lo, m) + n0
        d = y - _segsum(y, ones) * (1.0 / HALF)
        var = _segsum(d * d, ones) * (1.0 / HALF)
        y = d * lax.rsqrt(var + RWKV_LN_EPS) * lnw_ref[...] + lnb_ref[...]
        o_ref[sl, :] = ((y + bonus) * g_ref[sl, :]).astype(o_ref.dtype)

    gsz = min(RWKV_GROUP, nchunk)

    def group(gi, carry):
        sls = [pl.ds(pl.multiple_of((gi * gsz + g) * CHUNK, CHUNK), CHUNK) for g in range(gsz)]
        for sl, part in zip(sls, transitions(sls)):
            advance(sl, *part)
        return carry

    lax.fori_loop(0, nchunk // gsz, group, 0)


def _rwkv_scan(r, lw, k, v, nk, kb, g, r_k, ln_w, ln_b, bsz, seq, tc=1024):
    tc = min(tc, seq)
    ns = seq // tc
    npair = GROUP // LANE
    blk = pl.BlockSpec((tc, LANE), lambda b, p, i: (b * ns + i, p))
    vec = pl.BlockSpec((1, LANE), lambda b, p, i: (0, p))
    return pl.pallas_call(
        _rwkv_body,
        out_shape=jax.ShapeDtypeStruct((bsz * seq, GROUP), BF16),
        grid=(bsz, npair, ns),
        in_specs=[blk] * 7 + [vec] * 3,
        out_specs=blk,
        scratch_shapes=[pltpu.VMEM((LANE, LANE), F32)],
        compiler_params=_cparams(("parallel", "parallel", "arbitrary")),
        name="rwkv_scan",
    )(r, lw, k, v, nk, kb, g, r_k.reshape(1, GROUP), ln_w.reshape(1, GROUP), ln_b.reshape(1, GROUP))


def _pad_rows(w, start):
    full = jnp.zeros((SMALL, w.shape[1]), F32).at[start:start + w.shape[0]].set(w)
    hi = full.astype(BF16)
    return jnp.stack([hi, (full - hi.astype(F32)).astype(BF16)])


def _layer(x, c8, layer_idx, v_first, vres, bias_ad, bias_as, bias_d, p, attn_tile):
    bsz, seq, d = x.shape
    t = bsz * seq
    mod = _ada(c8, p["ada_w"], p["ada_b"], layer_idx)
    mod3 = mod[:bsz].reshape(bsz * 6, 1, d)
    h = _norm_mod(x, mod3, 1, 0)

    vres_cols = vres[0] if vres is not None else jnp.zeros((d, 32), F32)
    w_r = _regroup_w_in(p["w_in"], layer_idx, vres_cols)
    cols, tail = _in_proj(h, w_r)

    qk, vt = _prep_a(cols, p["diff_q_norm"], p["diff_k_norm"])
    o_a = _diff_attention(qk, vt, bias_ad, bias_as, p["diff_lambda"], p["diff_subln"], bsz, seq, layer_idx,
                          attn_tile)
    o_dd = _swa_attention(cols, tail, bias_d, p["swa_q_norm"], p["swa_k_norm"], p["swa_sinks"], bsz, seq)
    mu = p["rwkv_mu"]
    vres_mu = vres[1] if vres is not None else jnp.zeros((32,), F32)
    mu_small = jnp.concatenate([jnp.zeros((S_WD,), F32), mu[3 * GROUP:], vres_mu,
                                jnp.zeros((SMALL - S_VR - 32,), F32)]).reshape(1, SMALL)
    v_up = _pad_rows(vres[2], S_VR) if vres is not None else jnp.zeros((2, SMALL, GROUP), BF16)
    v0 = vres[3] if vres is not None else jnp.zeros((GROUP,), F32)
    r_, lw_, k_, v_, nk_, kb_, g_, la_ = _prep_c(
        cols, tail, seq, mu[:3 * GROUP].reshape(1, -1), mu_small,
        _pad_rows(p["rwkv_w_up"], S_WD), _pad_rows(p["rwkv_a_up"], S_AD), _pad_rows(p["rwkv_g_up"], S_GD),
        v_up, _pad_rows(p["gla_gate_up"], S_GLA),
        p["rwkv_w0"].reshape(1, -1), p["rwkv_a0"].reshape(1, -1), v0.reshape(1, -1),
        p["rwkv_k_k"].reshape(1, -1), p["rwkv_k_a"].reshape(1, -1), p["gla_gate_bias"].reshape(1, -1),
        v_first if vres is not None else None)
    if vres is None:
        v_first = v_
    o_bb = _gla(cols, la_, p["gla_out_norm"], bsz, seq)
    o_c = _rwkv_scan(r_, lw_, k_, v_, nk_, kb_, g_, p["rwkv_r_k"].reshape(-1), p["rwkv_ln_w"], p["rwkv_ln_b"],
                     bsz, seq)

    x = _out_proj([o_a, o_bb, o_c, o_dd], p["w_out"], x, mod3, 2)
    h2 = _norm_mod(x, mod3, 4, 3)
    act = _ffn_up(h2, p["ffn_w1"], p["ffn_w3"], layer_idx)
    x = _ffn_down(act, p["ffn_w2"], x, mod3, 5)
    return x, v_first


def kernel(x, c, rel_bias, ada_w, ada_b, w_in, w_out, diff_q_norm, diff_k_norm, diff_lambda, diff_subln,
           gla_gate_up, gla_gate_bias, gla_out_norm, rwkv_mu, rwkv_w_up, rwkv_w0, rwkv_a_up, rwkv_a0,
           rwkv_g_up, rwkv_k_k, rwkv_k_a, rwkv_r_k, rwkv_ln_w, rwkv_ln_b, rwkv_vres_down, rwkv_vres_mu,
           rwkv_vres_up, rwkv_v0, swa_q_norm, swa_k_norm, swa_sinks, ffn_w1, ffn_w3, ffn_w2):
    bsz, seq, _ = x.shape
    depth = ada_w.shape[0]
    attn_tile = min(1024, seq)
    bias_ad = _bias_tiles(rel_bias[:, :DIFF_HEADS], attn_tile, attn_tile, 0, None, True, True, LOG2E)
    bias_as = _bias_tiles(rel_bias[:, :DIFF_HEADS], attn_tile, attn_tile, attn_tile, None, True, True, LOG2E)
    bias_d = _bias_tiles(rel_bias[:, DIFF_HEADS:], SWA_BLOCK, 2 * SWA_BLOCK, SWA_BLOCK, SWA_BLOCK, False)
    c8 = jnp.zeros((8, c.shape[1]), F32).at[:bsz].set(c)
    v_first = None
    for l in range(depth):
        p = dict(ada_w=ada_w, ada_b=ada_b, w_in=w_in, w_out=_cast_bf16(w_out, l),
                 diff_q_norm=diff_q_norm[l], diff_k_norm=diff_k_norm[l], diff_lambda=diff_lambda[l],
                 diff_subln=diff_subln[l], gla_gate_up=gla_gate_up[l], gla_gate_bias=gla_gate_bias[l],
                 gla_out_norm=gla_out_norm[l], rwkv_mu=rwkv_mu[l], rwkv_w_up=rwkv_w_up[l],
                 rwkv_w0=rwkv_w0[l], rwkv_a_up=rwkv_a_up[l], rwkv_a0=rwkv_a0[l], rwkv_g_up=rwkv_g_up[l],
                 rwkv_k_k=rwkv_k_k[l], rwkv_k_a=rwkv_k_a[l], rwkv_r_k=rwkv_r_k[l], rwkv_ln_w=rwkv_ln_w[l],
                 rwkv_ln_b=rwkv_ln_b[l], swa_q_norm=swa_q_norm[l], swa_k_norm=swa_k_norm[l],
                 swa_sinks=swa_sinks[l], ffn_w1=ffn_w1, ffn_w3=ffn_w3, ffn_w2=_cast_bf16(ffn_w2, l))
        vres = None if l == 0 else (rwkv_vres_down[l - 1], rwkv_vres_mu[l - 1], rwkv_vres_up[l - 1],
                                    rwkv_v0[l - 1])
        x, v_first = _layer(x, c8, l, v_first, vres, bias_ad, bias_as, bias_d, p, attn_tile)
    return x
```

```python
import functools
import math

import jax
import jax.numpy as jnp
from jax import lax
from jax.experimental import pallas as pl
from jax.experimental.pallas import tpu as pltpu

F32 = jnp.float32
BF16 = jnp.bfloat16
HI = lax.Precision.HIGHEST

D_MODEL = 4096
GROUP = 1024
D_FF = 11008
NORM_EPS = 1e-6
NEG_INF = -1e30
LANE = 128
HALF = 64

DIFF_HEADS = 8
DIFF_QK = 64
GLA_HEADS = 4
GLA_DK = 128
GLA_DV = 256
GLA_RANK = 16
GLA_NORMALIZER = 16.0
CHUNK = 64
SCAN_GROUP = 8
RWKV_GROUP = 16
RWKV_LN_EPS = 64e-5
SWA_HEADS = 16
SWA_KV_HEADS = 2
SWA_BLOCK = 128
REL_BUCKETS = 32
REL_MAX_DIST = 128

OFF_A = 0
OFF_B = 3072
OFF_C = 6144
OFF_D = 9216
OFF_S = 10496
SMALL = 256
NCOLS = OFF_S + SMALL
S_GLA, S_WD, S_AD, S_GD, S_VR = 0, 16, 80, 144, 208
TAIL = 512
OFF_T = NCOLS - TAIL

VMEM_LIMIT = 56 * 1024 * 1024


def _cparams(sem, vmem=VMEM_LIMIT):
    return pltpu.CompilerParams(dimension_semantics=sem, vmem_limit_bytes=vmem)


def _dot(a, b, prec=None):
    return jnp.dot(a, b, preferred_element_type=F32, precision=prec)


def _dot_nt(a, b, prec=None):
    return lax.dot_general(a, b, (((1,), (1,)), ((), ())), preferred_element_type=F32, precision=prec)


def _dot_tn(a, b, prec=None):
    return lax.dot_general(a, b, (((0,), (0,)), ((), ())), preferred_element_type=F32, precision=prec)


def _sigmoid(z):
    return 1.0 / (1.0 + jnp.exp(-z))


def _softplus(z):
    return jnp.maximum(z, 0.0) + jnp.log(1.0 + jnp.exp(-jnp.abs(z)))


def _seg_ones():
    r = lax.broadcasted_iota(jnp.int32, (LANE, LANE), 0) // HALF
    c = lax.broadcasted_iota(jnp.int32, (LANE, LANE), 1) // HALF
    return (r == c).astype(BF16)


def _split(x):
    hi = x.astype(BF16)
    return hi, (x - hi.astype(F32)).astype(BF16)


def _dot_left2(x, w):
    hi, lo = _split(x)
    return _dot(hi, w) + _dot(lo, w)


def _tri_cumsum(tri, x):
    hi, lo = _split(x)
    return _dot(tri, hi) + _dot(tri, lo)


def _segsum(x, ones):
    return _dot_left2(x, ones)


def _ada_body(c_ref, w_ref, b_ref, o_ref):
    c = c_ref[...]
    s = (c * _sigmoid(c)).astype(BF16)
    o_ref[...] = _dot(s, w_ref[...].astype(BF16)) + b_ref[...]


def _ada(c8, ada_w, ada_b, layer, tn=512):
    _, d, n = ada_w.shape
    return pl.pallas_call(
        _ada_body,
        out_shape=jax.ShapeDtypeStruct((8, n), F32),
        grid=(n // tn,),
        in_specs=[pl.BlockSpec((8, d), lambda j: (0, 0)),
                  pl.BlockSpec((None, d, tn), lambda j: (layer, 0, j)),
                  pl.BlockSpec((1, tn), lambda j: (0, j))],
        out_specs=pl.BlockSpec((8, tn), lambda j: (0, j)),
        compiler_params=_cparams(("parallel",)),
        name="ada_mod",
    )(c8, ada_w, ada_b[layer].reshape(1, n))


def _cast_body(w_ref, o_ref):
    o_ref[...] = w_ref[...].astype(o_ref.dtype)


def _cast_bf16(w, layer, tr=256):
    _, rows, cols = w.shape
    return pl.pallas_call(
        _cast_body,
        out_shape=jax.ShapeDtypeStruct((rows, cols), BF16),
        grid=(rows // tr,),
        in_specs=[pl.BlockSpec((None, tr, cols), lambda i: (layer, i, 0))],
        out_specs=pl.BlockSpec((tr, cols), lambda i: (i, 0)),
        compiler_params=_cparams(("parallel",)),
        name="weight_cast",
    )(w)


P_IN = 10704
SRC_B, SRC_C, SRC_D = 3072, 6160, 9424
SRC_BS, SRC_CS = SRC_B + 3072, SRC_C + 3072


PIECE = 16
PIECES = LANE // PIECE


def _piece_index(j, s):
    n_ab, n_c, n_d = OFF_C // LANE, OFF_D // LANE, OFF_S // LANE
    small0 = SRC_BS // PIECE if s == 0 else SRC_CS // PIECE + s - 1
    small1 = SRC_CS // PIECE + PIECES - 1 + min(s, 4)
    return jnp.where(j < n_ab, j * PIECES + s,
           jnp.where(j < n_c, SRC_C // PIECE + (j - n_ab) * PIECES + s,
           jnp.where(j < n_d, SRC_D // PIECE + (j - n_c) * PIECES + s,
           jnp.where(j == n_d, small0, small1))))


def _regroup_body(*refs):
    w_refs, vr_ref, o_ref = refs[:PIECES], refs[PIECES], refs[PIECES + 1]
    last = pl.program_id(0) == pl.num_programs(0) - 1
    pieces = [r[...] for r in w_refs]
    pieces[5] = jnp.where(last, vr_ref[:PIECE, :], pieces[5])
    pieces[6] = jnp.where(last, vr_ref[PIECE:, :], pieces[6])
    pieces[7] = jnp.where(last, 0.0, pieces[7])
    o_ref[...] = jnp.concatenate(pieces, axis=0).T.astype(o_ref.dtype)


def _regroup_w_in(w_in, layer, vres_down):
    _, d, _ = w_in.shape
    w_t = jnp.swapaxes(w_in, 1, 2)
    piece = lambda s: pl.BlockSpec((None, PIECE, d), lambda j: (layer, _piece_index(j, s), 0))
    return pl.pallas_call(
        _regroup_body,
        out_shape=jax.ShapeDtypeStruct((d, NCOLS), BF16),
        grid=(NCOLS // LANE,),
        in_specs=[piece(s) for s in range(PIECES)] + [pl.BlockSpec((2 * PIECE, d), lambda j: (0, 0))],
        out_specs=pl.BlockSpec((d, LANE), lambda j: (0, j)),
        compiler_params=_cparams(("parallel",)),
        name="w_in_regroup",
    )(*([w_t] * PIECES), vres_down.T)


def _norm_body(x_ref, sc_ref, sh_ref, o_ref):
    x = x_ref[0]
    ms = jnp.mean(x * x, axis=-1, keepdims=True)
    h = x * lax.rsqrt(ms + NORM_EPS) * (1.0 + sc_ref[0]) + sh_ref[0]
    o_ref[...] = h.astype(o_ref.dtype)


def _norm_mod(x, mod3, sc_idx, sh_idx, ts=256):
    bsz, seq, d = x.shape
    ts = min(ts, seq)
    ns = seq // ts
    return pl.pallas_call(
        _norm_body,
        out_shape=jax.ShapeDtypeStruct((bsz * seq, d), BF16),
        grid=(bsz, ns),
        in_specs=[pl.BlockSpec((1, ts, d), lambda b, i: (b, i, 0)),
                  pl.BlockSpec((1, 1, d), lambda b, i: (b * 6 + sc_idx, 0, 0)),
                  pl.BlockSpec((1, 1, d), lambda b, i: (b * 6 + sh_idx, 0, 0))],
        out_specs=pl.BlockSpec((ts, d), lambda b, i: (b * ns + i, 0)),
        compiler_params=_cparams(("parallel", "parallel")),
        name="norm_mod",
    )(x, mod3, mod3)


def _in_proj_body(a_ref, b_ref, main_ref, tail_ref):
    j = pl.program_id(1)
    acc = _dot(a_ref[...], b_ref[...])

    @pl.when(j < pl.num_programs(1) - 1)
    def _():
        main_ref[...] = acc.astype(main_ref.dtype)

    @pl.when(j == pl.num_programs(1) - 1)
    def _():
        tail_ref[...] = acc


def _in_proj(a, b, tm=1024):
    m, k = a.shape
    tm = min(tm, m)
    n_main = OFF_T // TAIL
    return pl.pallas_call(
        _in_proj_body,
        out_shape=[jax.ShapeDtypeStruct((m, OFF_T), BF16), jax.ShapeDtypeStruct((m, TAIL), F32)],
        grid=(m // tm, n_main + 1),
        in_specs=[pl.BlockSpec((tm, k), lambda i, j: (i, 0)),
                  pl.BlockSpec((k, TAIL), lambda i, j: (0, j))],
        out_specs=[pl.BlockSpec((tm, TAIL), lambda i, j: (i, jnp.minimum(j, n_main - 1))),
                   pl.BlockSpec((tm, TAIL), lambda i, j: (i, 0))],
        compiler_params=_cparams(("parallel", "arbitrary")),
        name="in_proj",
    )(a, b)


def _outproj_body(a0, a1, a2, a3, w_ref, x_ref, g_ref, o_ref):
    acc = _dot(a0[...], w_ref[0 * GROUP:1 * GROUP, :])
    acc += _dot(a1[...], w_ref[1 * GROUP:2 * GROUP, :])
    acc += _dot(a2[...], w_ref[2 * GROUP:3 * GROUP, :])
    acc += _dot(a3[...], w_ref[3 * GROUP:4 * GROUP, :])
    o_ref[0] = x_ref[0] + g_ref[0] * acc


def _out_proj(parts, w, x, mod3, gate_idx, tm=1024, tn=512):
    bsz, seq, d = x.shape
    tm = min(tm, seq)
    ns = seq // tm
    a_spec = pl.BlockSpec((tm, GROUP), lambda b, i, j: (b * ns + i, 0))
    return pl.pallas_call(
        _outproj_body,
        out_shape=jax.ShapeDtypeStruct((bsz, seq, d), F32),
        grid=(bsz, ns, d // tn),
        in_specs=[a_spec, a_spec, a_spec, a_spec,
                  pl.BlockSpec((4 * GROUP, tn), lambda b, i, j: (0, j)),
                  pl.BlockSpec((1, tm, tn), lambda b, i, j: (b, i, j)),
                  pl.BlockSpec((1, 1, tn), lambda b, i, j: (b * 6 + gate_idx, 0, j))],
        out_specs=pl.BlockSpec((1, tm, tn), lambda b, i, j: (b, i, j)),
        compiler_params=_cparams(("parallel", "parallel", "parallel")),
        name="out_proj",
    )(*parts, w, x, mod3)


def _ffn_up_body(a_ref, w1_ref, w3_ref, o_ref):
    a = a_ref[...]
    u = _dot(a, w1_ref[...].astype(BF16))
    v = _dot(a, w3_ref[...].astype(BF16))
    o_ref[...] = (u * _sigmoid(u) * v).astype(o_ref.dtype)


def _ffn_up(h, w1, w3, layer, tm=1024, tn=256):
    m, k = h.shape
    n = w1.shape[2]
    tm = min(tm, m)
    return pl.pallas_call(
        _ffn_up_body,
        out_shape=jax.ShapeDtypeStruct((m, n), BF16),
        grid=(m // tm, n // tn),
        in_specs=[pl.BlockSpec((tm, k), lambda i, j: (i, 0)),
                  pl.BlockSpec((None, k, tn), lambda i, j: (layer, 0, j)),
                  pl.BlockSpec((None, k, tn), lambda i, j: (layer, 0, j))],
        out_specs=pl.BlockSpec((tm, tn), lambda i, j: (i, j)),
        compiler_params=_cparams(("parallel", "parallel")),
        name="ffn_up",
    )(h, w1, w3)


def _ffn_down_body(a_ref, w_ref, x_ref, g_ref, o_ref):
    o_ref[0] = x_ref[0] + g_ref[0] * _dot(a_ref[...], w_ref[...])


def _ffn_down(a, w, x, mod3, gate_idx, tm=512, tn=512):
    bsz, seq, d = x.shape
    k = a.shape[1]
    tm = min(tm, seq)
    ns = seq // tm
    return pl.pallas_call(
        _ffn_down_body,
        out_shape=jax.ShapeDtypeStruct((bsz, seq, d), F32),
        grid=(bsz, ns, d // tn),
        in_specs=[pl.BlockSpec((tm, k), lambda b, i, j: (b * ns + i, 0)),
                  pl.BlockSpec((k, tn), lambda b, i, j: (0, j)),
                  pl.BlockSpec((1, tm, tn), lambda b, i, j: (b, i, j)),
                  pl.BlockSpec((1, 1, tn), lambda b, i, j: (b * 6 + gate_idx, 0, j))],
        out_specs=pl.BlockSpec((1, tm, tn), lambda b, i, j: (b, i, j)),
        compiler_params=_cparams(("parallel", "parallel", "parallel")),
        name="ffn_down",
    )(a, w, x, mod3)


def _bias_body(tab_ref, o_ref, *, off, window, shift, key_major, scale):
    h = pl.program_id(0)
    rows, cols = o_ref.shape[1], o_ref.shape[2]
    last = tab_ref[REL_BUCKETS - 1, h]
    max_exact = REL_BUCKETS // 2

    def block(d0):
        qi = lax.broadcasted_iota(jnp.int32, (LANE, LANE), 1 if key_major else 0)
        kj = lax.broadcasted_iota(jnp.int32, (LANE, LANE), 0 if key_major else 1)
        dist = d0 + qi - kj
        n = jnp.maximum(dist, 0)
        nf = jnp.maximum(n, 1).astype(F32)
        large = max_exact + (jnp.log(nf / max_exact) / math.log(REL_MAX_DIST / max_exact)
                             * (REL_BUCKETS - max_exact)).astype(jnp.int32)
        large = jnp.minimum(large, REL_BUCKETS - 1)
        bucket = jnp.where(n < max_exact, n, large)
        bias = jnp.full((LANE, LANE), last, F32)
        for b in range(REL_BUCKETS - 1):
            bias = jnp.where(bucket == b, tab_ref[b, h], bias)
        if shift:
            bias = bias - last
        if scale != 1.0:
            bias = bias * scale
        valid = dist >= 0
        if window is not None:
            valid = jnp.logical_and(valid, dist < window)
        return jnp.where(valid, bias, NEG_INF)

    cache = {}
    for br in range(rows // LANE):
        for bc in range(cols // LANE):
            qb, kb = (bc, br) if key_major else (br, bc)
            d0 = off + LANE * (qb - kb)
            if d0 not in cache:
                if d0 + LANE <= 0 or (window is not None and d0 - LANE >= window):
                    cache[d0] = jnp.full((LANE, LANE), NEG_INF, F32)
                elif d0 - LANE >= REL_MAX_DIST and (window is None or d0 + LANE <= window):
                    const = (0.0 if shift else last) * scale
                    cache[d0] = jnp.full((LANE, LANE), const, F32)
                else:
                    cache[d0] = block(d0)
            o_ref[0, br * LANE:(br + 1) * LANE, bc * LANE:(bc + 1) * LANE] = cache[d0]


def _bias_tiles(table, rows, cols, off, window, shift, key_major=False, scale=1.0):
    nh = table.shape[1]
    return pl.pallas_call(
        functools.partial(_bias_body, off=off, window=window, shift=shift, key_major=key_major, scale=scale),
        out_shape=jax.ShapeDtypeStruct((nh, rows, cols), F32),
        grid=(nh,),
        in_specs=[pl.BlockSpec(memory_space=pltpu.SMEM)],
        out_specs=pl.BlockSpec((1, rows, cols), lambda h: (h, 0, 0)),
        compiler_params=_cparams(("parallel",)),
        name="rel_bias_tiles",
    )(table)


LOG2E = 1.4426950408889634
VT_ROWS = LANE + 16
Q_CHUNK = 512


def _prep_a_body(c_ref, qg_ref, kg_ref, qk_ref, vt_ref):
    ones = _seg_ones()
    scale = DIFF_QK ** -0.5 * LOG2E
    tm = c_ref.shape[0]
    for j in range(2 * DIFF_HEADS):
        x = c_ref[:, j * LANE:(j + 1) * LANE].astype(F32)
        ms = _segsum(x * x, ones) * (1.0 / DIFF_QK)
        gain = qg_ref[...] * scale if j < DIFF_HEADS else kg_ref[...]
        qk_ref[:, j * LANE:(j + 1) * LANE] = (x * lax.rsqrt(ms + NORM_EPS) * gain).astype(qk_ref.dtype)
    for h in range(DIFF_HEADS):
        v = c_ref[:, 2 * GROUP + h * LANE:2 * GROUP + (h + 1) * LANE].astype(F32)
        vt_ref[h * VT_ROWS:h * VT_ROWS + LANE, :] = v.T.astype(vt_ref.dtype)
        vt_ref[h * VT_ROWS + LANE:(h + 1) * VT_ROWS, :] = jnp.ones((VT_ROWS - LANE, tm), vt_ref.dtype)


def _prep_a(cols, q_gain, k_gain, tm=256):
    t = cols.shape[0]
    tm = min(tm, t)
    qg = jnp.tile(q_gain, 2).reshape(1, LANE)
    kg = jnp.tile(k_gain, 2).reshape(1, LANE)
    vec = pl.BlockSpec((1, LANE), lambda i: (0, 0))
    return pl.pallas_call(
        _prep_a_body,
        out_shape=[jax.ShapeDtypeStruct((t, 2 * GROUP), BF16),
                   jax.ShapeDtypeStruct((DIFF_HEADS * VT_ROWS, t), BF16)],
        grid=(t // tm,),
        in_specs=[pl.BlockSpec((tm, 3 * GROUP), lambda i: (i, OFF_A // (3 * GROUP))), vec, vec],
        out_specs=[pl.BlockSpec((tm, 2 * GROUP), lambda i: (i, 0)),
                   pl.BlockSpec((DIFF_HEADS * VT_ROWS, tm), lambda i: (0, i))],
        compiler_params=_cparams(("parallel",)),
        name="diff_prep",
    )(cols, qg, kg)


def _flash_body(it_ref, jt_ref, q_ref, k_ref, vt_ref, bd_ref, bs_ref, lam_ref, sub_ref, o_ref,
                qlo, qhi, m1, a1, m2, a2, *, lam_init):
    t = pl.program_id(2)
    i = it_ref[t]
    j = jt_ref[t]

    @pl.when(j == 0)
    def _():
        q = q_ref[...]
        lane = lax.broadcasted_iota(jnp.int32, q.shape, 1)
        zero = jnp.zeros_like(q)
        qlo[...] = jnp.where(lane < HALF, q, zero)
        qhi[...] = jnp.where(lane >= HALF, q, zero)
        for m, a in ((m1, a1), (m2, a2)):
            m[...] = jnp.full(m.shape, NEG_INF, F32)
            a[...] = jnp.zeros(a.shape, F32)

    def step(bias):
        k = k_ref[...]
        vt = vt_ref[...]
        nq = q_ref.shape[0]
        qc = min(Q_CHUNK, nq)
        chains = [(qq, m, a, slice(c * qc, (c + 1) * qc))
                  for qq, m, a in ((qlo, m1, a1), (qhi, m2, a2)) for c in range(nq // qc)]
        ahead = 2
        if bias is None:
            scores = [_dot_nt(k, qq[qs, :]) for qq, _, _, qs in chains[:ahead]]
        else:
            full = [_dot_nt(k, qq[...]) for qq in (qlo, qhi)]
            scores = [full[i // (nq // qc)][:, chains[i][3]] + bias[:, chains[i][3]] for i in range(len(chains))]
        for idx, (_, m, a, qs) in enumerate(chains):
            if bias is None and idx + ahead < len(chains):
                qq_n, _, _, qs_n = chains[idx + ahead]
                scores.append(_dot_nt(k, qq_n[qs_n, :]))
            s = scores[idx]
            m_old = m[:, qs]
            m_new = jnp.maximum(m_old, jnp.max(s, axis=0, keepdims=True))
            p = jnp.exp2(s - m_new).astype(BF16)
            a[:, qs] = jnp.exp2(m_old - m_new) * a[:, qs] + _dot(vt, p)
            m[:, qs] = m_new

    @pl.when(j < i - 1)
    def _():
        step(None)

    @pl.when(j == i - 1)
    def _():
        step(bs_ref[0])

    @pl.when(j == i)
    def _():
        step(bd_ref[0])
        lam = lam_ref[...]
        e1 = jnp.exp(jnp.sum(lam[0:1] * lam[1:2], axis=-1, keepdims=True))
        e2 = jnp.exp(jnp.sum(lam[2:3] * lam[3:4], axis=-1, keepdims=True))
        lam_full = e1 - e2 + lam_init
        o = a1[:LANE, :] / a1[LANE:LANE + 1, :] - lam_full * (a2[:LANE, :] / a2[LANE:LANE + 1, :])
        ms = jnp.mean(o * o, axis=0, keepdims=True)
        o = o * lax.rsqrt(ms + NORM_EPS) * sub_ref[...] * (1.0 - lam_init)
        o_ref[...] = o.T.astype(o_ref.dtype)


def _diff_attention(qk, vt, bias_diag, bias_sub, lam, subln, bsz, seq, layer_idx, tile):
    nq = seq // tile
    lam_init = 0.8 - 0.6 * math.exp(-0.3 * layer_idx)
    pairs = [(i, j) for i in range(nq) for j in range(i + 1)]
    i_tab = jnp.array([ij[0] for ij in pairs], jnp.int32)
    j_tab = jnp.array([ij[1] for ij in pairs], jnp.int32)
    grid_spec = pltpu.PrefetchScalarGridSpec(
        num_scalar_prefetch=2,
        grid=(bsz, DIFF_HEADS, len(pairs)),
        in_specs=[pl.BlockSpec((tile, LANE), lambda b, h, t, it, jt: (b * nq + it[t], h)),
                  pl.BlockSpec((tile, LANE), lambda b, h, t, it, jt: (b * nq + jt[t], DIFF_HEADS + h)),
                  pl.BlockSpec((VT_ROWS, tile), lambda b, h, t, it, jt: (h, b * nq + jt[t])),
                  pl.BlockSpec((1, tile, tile), lambda b, h, t, it, jt: (h, 0, 0)),
                  pl.BlockSpec((1, tile, tile), lambda b, h, t, it, jt: (h, 0, 0)),
                  pl.BlockSpec((4, DIFF_QK), lambda b, h, t, it, jt: (0, 0)),
                  pl.BlockSpec((LANE, 1), lambda b, h, t, it, jt: (0, 0))],
        out_specs=pl.BlockSpec((tile, LANE), lambda b, h, t, it, jt: (b * nq + it[t], h)),
        scratch_shapes=[pltpu.VMEM((tile, LANE), BF16), pltpu.VMEM((tile, LANE), BF16),
                        pltpu.VMEM((1, tile), F32), pltpu.VMEM((VT_ROWS, tile), F32),
                        pltpu.VMEM((1, tile), F32), pltpu.VMEM((VT_ROWS, tile), F32)])
    return pl.pallas_call(
        functools.partial(_flash_body, lam_init=lam_init),
        out_shape=jax.ShapeDtypeStruct((bsz * seq, GROUP), BF16),
        grid_spec=grid_spec,
        compiler_params=_cparams(("parallel", "parallel", "arbitrary")),
        name="diff_attention",
    )(i_tab, j_tab, qk, qk, vt, bias_diag, bias_sub, lam, subln.reshape(LANE, 1))


def _swa_body(q_ref, kp_ref, kc_ref, vp_ref, vc_ref, bias_ref, qg_ref, kg_ref, sink_ref, o_ref):
    n = pl.program_id(1)
    g = pl.program_id(2)
    ones = _seg_ones()
    lane = lax.broadcasted_iota(jnp.int32, (SWA_BLOCK, LANE), 1)
    lo = lane < HALF

    def norm(x, gain):
        ms = _segsum(x * x, ones) * (1.0 / HALF)
        return x * lax.rsqrt(ms + NORM_EPS) * gain

    k = jnp.concatenate([norm(kp_ref[...], kg_ref[...]), norm(kc_ref[...], kg_ref[...])], axis=0)
    v = jnp.concatenate([vp_ref[...], vc_ref[...]], axis=0)
    k_sw = pltpu.roll(k, HALF, axis=1)
    v_sw = pltpu.roll(v, HALF, axis=1)
    first_head = g == 0
    lo2 = lax.broadcasted_iota(jnp.int32, (2 * SWA_BLOCK, LANE), 1) < HALF
    k_a = jnp.where(first_head, k, k_sw).astype(BF16)
    k_b = jnp.where(first_head, k_sw, k).astype(BF16)
    v_mine_lo = jnp.where(first_head, v, v_sw)
    v_mine_hi = jnp.where(first_head, v_sw, v)
    v_a = jnp.where(lo2, v_mine_lo, 0.0).astype(BF16)
    v_b = jnp.where(lo2, 0.0, v_mine_hi).astype(BF16)
    kcol = lax.broadcasted_iota(jnp.int32, (SWA_BLOCK, 2 * SWA_BLOCK), 1)
    pad = jnp.logical_and(n == 0, kcol < SWA_BLOCK)
    scale = HALF ** -0.5
    grp = SWA_HEADS // SWA_KV_HEADS
    heads = range(grp)
    q = [norm(q_ref[:, pr * LANE:(pr + 1) * LANE].astype(F32), qg_ref[...] * scale) for pr in range(grp // 2)]
    qm = [jnp.where(lo if hh % 2 == 0 else jnp.logical_not(lo), q[hh // 2], 0.0).astype(BF16) for hh in heads]
    s = [_dot_nt(qm[hh], k_a if hh % 2 == 0 else k_b) for hh in heads]
    s = [jnp.where(pad, NEG_INF, s[hh] + bias_ref[hh]) for hh in heads]
    sink = [sink_ref[g * grp + hh] for hh in heads]
    m = [jnp.maximum(jnp.max(s[hh], axis=-1, keepdims=True), sink[hh]) for hh in heads]
    p = [jnp.exp(s[hh] - m[hh]) for hh in heads]
    den = [jnp.sum(p[hh], axis=-1, keepdims=True) + jnp.exp(sink[hh] - m[hh]) for hh in heads]
    pv = [_dot(p[hh].astype(BF16), v_a if hh % 2 == 0 else v_b) for hh in heads]
    for pr in range(grp // 2):
        o = pv[2 * pr] / den[2 * pr] + pv[2 * pr + 1] / den[2 * pr + 1]
        o_ref[:, pr * LANE:(pr + 1) * LANE] = o.astype(o_ref.dtype)


def _swa_attention(cols, tail, bias, q_gain, k_gain, sinks, bsz, seq):
    nb = seq // SWA_BLOCK
    grp = SWA_HEADS // SWA_KV_HEADS
    qw = grp * HALF
    kcol = (OFF_D + GROUP - OFF_T) // LANE
    vcol = kcol + 1
    prev = lambda b, n, g: b * nb + jnp.maximum(n - 1, 0)
    cur = lambda b, n, g: b * nb + n
    vec = pl.BlockSpec((1, LANE), lambda b, n, g: (0, 0))
    return pl.pallas_call(
        _swa_body,
        out_shape=jax.ShapeDtypeStruct((bsz * seq, GROUP), BF16),
        grid=(bsz, nb, SWA_KV_HEADS),
        in_specs=[pl.BlockSpec((SWA_BLOCK, qw), lambda b, n, g: (cur(b, n, g), OFF_D // qw + g)),
                  pl.BlockSpec((SWA_BLOCK, LANE), lambda b, n, g: (prev(b, n, g), kcol)),
                  pl.BlockSpec((SWA_BLOCK, LANE), lambda b, n, g: (cur(b, n, g), kcol)),
                  pl.BlockSpec((SWA_BLOCK, LANE), lambda b, n, g: (prev(b, n, g), vcol)),
                  pl.BlockSpec((SWA_BLOCK, LANE), lambda b, n, g: (cur(b, n, g), vcol)),
                  pl.BlockSpec((grp, SWA_BLOCK, 2 * SWA_BLOCK), lambda b, n, g: (g, 0, 0)),
                  vec, vec,
                  pl.BlockSpec(memory_space=pltpu.SMEM)],
        out_specs=pl.BlockSpec((SWA_BLOCK, qw), lambda b, n, g: (cur(b, n, g), g)),
        compiler_params=_cparams(("parallel", "parallel", "parallel")),
        name="swa_attention",
    )(cols, tail, tail, tail, tail, bias,
      jnp.tile(q_gain, 2).reshape(1, LANE), jnp.tile(k_gain, 2).reshape(1, LANE), sinks)


def _prep_c_body(*refs, seq, with_vres):
    (cm_ref, cs_ref, pm_ref, ps_ref, mum_ref, mus_ref, wup_ref, aup_ref, gup_ref, vup_ref, glaup_ref,
     w0_ref, a0_ref, v0_ref, kk_ref, ka_ref, glab_ref) = refs[:17]
    rest = refs[17:]
    if with_vres:
        vf_ref, rest = rest[0], rest[1:]
    r_ref, lw_ref, k_ref, v_ref, nk_ref, kb_ref, g_ref, la_ref = rest
    tm = cm_ref.shape[0]
    i = pl.program_id(0)
    seq_start = (i * tm) % seq == 0

    def shifted(cur, prev_rows, mu):
        row = lax.broadcasted_iota(jnp.int32, cur.shape, 0)
        before = jnp.where(seq_start, 0.0, prev_rows[PREV_ROWS - 1:PREV_ROWS, :].astype(F32))
        prev = jnp.where(row == 0, before, pltpu.roll(cur, 1, axis=0))
        return cur + (prev - cur) * mu

    def low_rank(x, w2_ref):
        hi, lo = _split(x)
        return _dot(hi, w2_ref[0]) + _dot(lo, w2_ref[0]) + _dot(hi, w2_ref[1])

    cs = cs_ref[...]
    la_ref[...] = -_softplus(-(low_rank(cs, glaup_ref) + glab_ref[...])) * (1.0 / GLA_NORMALIZER)
    sm = shifted(cm_ref[...].astype(F32), pm_ref, mum_ref[...])
    ss = shifted(cs, ps_ref, mus_ref[...])
    r = sm[:, :GROUP]
    k = sm[:, GROUP:2 * GROUP]
    v = sm[:, 2 * GROUP:]
    if with_vres:
        gate = _sigmoid(v0_ref[...] + low_rank(ss, vup_ref))
        v = v + (vf_ref[...] - v) * gate
    w_log = -_softplus(-(w0_ref[...] + low_rank(jnp.tanh(ss), wup_ref))) - 0.5
    a = _sigmoid(a0_ref[...] + low_rank(ss, aup_ref))
    r_ref[...] = r.astype(r_ref.dtype)
    lw_ref[...] = -jnp.exp(w_log)
    v_ref[...] = v
    g_ref[...] = low_rank(_sigmoid(ss), gup_ref).astype(g_ref.dtype)
    k_ref[...] = (k * (1.0 + (a - 1.0) * ka_ref[...])).astype(k_ref.dtype)
    ones = _seg_ones()
    kk = k * kk_ref[...]
    for j in range(GROUP // LANE):
        sl = slice(j * LANE, (j + 1) * LANE)
        x = kk[:, sl]
        nrm = jnp.maximum(jnp.sqrt(_segsum(x * x, ones)), 1e-12)
        x = x / nrm
        nk_ref[:, sl] = x.astype(nk_ref.dtype)
        kb_ref[:, sl] = (x * a[:, sl]).astype(kb_ref.dtype)


PREV_ROWS = 16


def _prep_c(cols, tail, seq, mu_main, mu_small, w_up, a_up, g_up, v_up, gla_up, w0, a0, v0, k_k, k_a, gla_bias,
            v_first, tm=256):
    t = cols.shape[0]
    tm = min(tm, seq)
    with_vres = v_first is not None
    cmain = OFF_C // (3 * GROUP)
    csmall = (OFF_S - OFF_T) // SMALL
    prev = lambda i: jnp.maximum(i * (tm // PREV_ROWS) - 1, 0)
    full = lambda r, c: pl.BlockSpec((r, c), lambda i: (0, 0))
    hilo = lambda c: pl.BlockSpec((2, SMALL, c), lambda i: (0, 0, 0))
    in_specs = [pl.BlockSpec((tm, 3 * GROUP), lambda i: (i, cmain)),
                pl.BlockSpec((tm, SMALL), lambda i: (i, csmall)),
                pl.BlockSpec((PREV_ROWS, 3 * GROUP), lambda i: (prev(i), cmain)),
                pl.BlockSpec((PREV_ROWS, SMALL), lambda i: (prev(i), csmall)),
                full(1, 3 * GROUP), full(1, SMALL),
                hilo(GROUP), hilo(GROUP), hilo(GROUP), hilo(GROUP), hilo(GLA_HEADS * GLA_DK),
                full(1, GROUP), full(1, GROUP), full(1, GROUP), full(1, GROUP), full(1, GROUP),
                full(1, GLA_HEADS * GLA_DK)]
    args = [cols, tail, cols, tail, mu_main, mu_small, w_up, a_up, g_up, v_up, gla_up,
            w0, a0, v0, k_k, k_a, gla_bias]
    if with_vres:
        in_specs.append(pl.BlockSpec((tm, GROUP), lambda i: (i, 0)))
        args.append(v_first)
    row = pl.BlockSpec((tm, GROUP), lambda i: (i, 0))
    out = [jax.ShapeDtypeStruct((t, GROUP), dt) for dt in (BF16, F32, BF16, F32, BF16, BF16, BF16)]
    return pl.pallas_call(
        functools.partial(_prep_c_body, seq=seq, with_vres=with_vres),
        out_shape=out + [jax.ShapeDtypeStruct((t, GLA_HEADS * GLA_DK), F32)],
        grid=(t // tm,),
        in_specs=in_specs,
        out_specs=[row] * 7 + [pl.BlockSpec((tm, GLA_HEADS * GLA_DK), lambda i: (i, 0))],
        compiler_params=_cparams(("parallel",)),
        name="rwkv_gla_prep",
    )(*args)


def _gla_body(q_ref, k_ref, v_ref, g_ref, la_ref, gain_ref, o_ref, st_ref):
    @pl.when(pl.program_id(2) == 0)
    def _():
        st_ref[...] = jnp.zeros(st_ref.shape, F32)

    row = lax.broadcasted_iota(jnp.int32, (CHUNK, CHUNK), 0)
    col = lax.broadcasted_iota(jnp.int32, (CHUNK, CHUNK), 1)
    causal = row >= col
    tri = causal.astype(BF16)
    nchunk = q_ref.shape[0] // CHUNK

    def local(sls):
        n = range(len(sls))
        k = [k_ref[sl, :].astype(F32) for sl in sls]
        v = [v_ref[sl, :].astype(BF16) for sl in sls]
        b = [_tri_cumsum(tri, la_ref[sl, :]) for sl in sls]
        b_last = [b[i][CHUNK - 1:CHUNK, :] for i in n]
        q_dec = [(q_ref[sl, :].astype(F32) * (GLA_DK ** -0.5) * jnp.exp(b[i])).astype(BF16)
                 for i, sl in enumerate(sls)]
        a_intra = [jnp.where(causal, _dot_nt(q_dec[i], (k[i] * jnp.exp(-b[i])).astype(BF16)), 0.0) for i in n]
        upd = [_dot_tn(v[i], (k[i] * jnp.exp(b_last[i] - b[i])).astype(BF16)) for i in n]
        o_intra = [_dot(a_intra[i].astype(BF16), v[i]) for i in n]
        return [(q_dec[i], o_intra[i], upd[i], jnp.exp(b_last[i])) for i in n]

    def advance(sl, q_dec, o_intra, upd, dec):
        state = st_ref[...]
        s_hi, s_lo = _split(state)
        o = o_intra + _dot_nt(q_dec, s_hi) + _dot_nt(q_dec, s_lo)
        st_ref[...] = state * dec + upd
        ms = jnp.mean(o * o, axis=-1, keepdims=True)
        o = o * lax.rsqrt(ms + NORM_EPS) * gain_ref[...]
        gate = g_ref[sl, :].astype(F32)
        o_ref[sl, :] = (o * (gate * _sigmoid(gate))).astype(o_ref.dtype)

    def group(gi, carry):
        sls = [pl.ds(pl.multiple_of((gi * SCAN_GROUP + g) * CHUNK, CHUNK), CHUNK) for g in range(SCAN_GROUP)]
        for sl, part in zip(sls, local(sls)):
            advance(sl, *part)
        return carry

    lax.fori_loop(0, nchunk // SCAN_GROUP, group, 0)


def _gla(cols, log_a, out_gain, bsz, seq, tc=512):
    tc = min(tc, seq)
    ns = seq // tc
    qc = OFF_B // GLA_DK
    kc = qc + GLA_HEADS
    vc = (OFF_B + 2 * GLA_HEADS * GLA_DK) // GLA_DV
    gc = vc + GLA_HEADS
    rows = lambda b, h, i: b * ns + i
    return pl.pallas_call(
        _gla_body,
        out_shape=jax.ShapeDtypeStruct((bsz * seq, GROUP), BF16),
        grid=(bsz, GLA_HEADS, ns),
        in_specs=[pl.BlockSpec((tc, GLA_DK), lambda b, h, i: (rows(b, h, i), qc + h)),
                  pl.BlockSpec((tc, GLA_DK), lambda b, h, i: (rows(b, h, i), kc + h)),
                  pl.BlockSpec((tc, GLA_DV), lambda b, h, i: (rows(b, h, i), vc + h)),
                  pl.BlockSpec((tc, GLA_DV), lambda b, h, i: (rows(b, h, i), gc + h)),
                  pl.BlockSpec((tc, GLA_DK), lambda b, h, i: (rows(b, h, i), h)),
                  pl.BlockSpec((1, GLA_DV), lambda b, h, i: (0, 0))],
        out_specs=pl.BlockSpec((tc, GLA_DV), lambda b, h, i: (rows(b, h, i), h)),
        scratch_shapes=[pltpu.VMEM((GLA_DV, GLA_DK), F32)],
        compiler_params=_cparams(("parallel", "parallel", "arbitrary")),
        name="gla_scan",
    )(cols, cols, cols, cols, log_a, out_gain.reshape(1, GLA_DV))


def _rwkv_body(r_ref, lw_ref, k_ref, v_ref, nk_ref, kb_ref, g_ref, rk_ref, lnw_ref, lnb_ref, o_ref, st_ref):
    @pl.when(pl.program_id(2) == 0)
    def _():
        st_ref[...] = jnp.zeros(st_ref.shape, F32)

    two = 2 * CHUNK
    row = lax.broadcasted_iota(jnp.int32, (two, two), 0)
    col = lax.broadcasted_iota(jnp.int32, (two, two), 1)
    same = (row // CHUNK) == (col // CHUNK)
    strict = jnp.logical_and(same, (row % CHUNK) > (col % CHUNK))
    incl = jnp.logical_and(same, (row % CHUNK) >= (col % CHUNK))
    eye = (row == col).astype(F32)
    crow = lax.broadcasted_iota(jnp.int32, (CHUNK, CHUNK), 0)
    ccol = lax.broadcasted_iota(jnp.int32, (CHUNK, CHUNK), 1)
    tri = (crow >= ccol).astype(BF16)
    lo = lax.broadcasted_iota(jnp.int32, (CHUNK, LANE), 1) < HALF
    ones = _seg_ones()
    nchunk = r_ref.shape[0] // CHUNK

    def stack(x):
        return jnp.concatenate([jnp.where(lo, x, 0.0), jnp.where(lo, 0.0, x)], axis=0)

    def fold(x):
        return x[:CHUNK, :] + x[CHUNK:, :]

    def transitions(sls):
        n = range(len(sls))
        r = [r_ref[sl, :].astype(F32) for sl in sls]
        lw = [lw_ref[sl, :] for sl in sls]
        k = [k_ref[sl, :].astype(F32) for sl in sls]
        v = [v_ref[sl, :] for sl in sls]
        cum = [_tri_cumsum(tri, lw[i]) for i in n]
        last = [cum[i][CHUNK - 1:CHUNK, :] for i in n]
        a_bf, r_st, bk_st, bkh_st, v_st = [], [], [], [], []
        for i, sl in enumerate(sls):
            kb = kb_ref[sl, :].astype(F32)
            e_neg = jnp.exp(-cum[i])
            e_rem = jnp.exp(last[i] - cum[i])
            a_bf.append(stack(-nk_ref[sl, :].astype(F32) * jnp.exp(cum[i] - lw[i])).astype(BF16))
            r_st.append(stack(r[i] * jnp.exp(cum[i])))
            bk_st.append(jnp.concatenate([stack(kb * e_neg), stack(k[i] * e_neg)], axis=0).astype(BF16))
            bkh_st.append(jnp.concatenate([stack(kb * e_rem), stack(k[i] * e_rem)], axis=0).astype(BF16))
            v_st.append(stack(v[i]).astype(BF16))
        sc = [_dot_nt(jnp.concatenate([a_bf[i], r_st[i].astype(BF16)], axis=0), bk_st[i]) for i in n]
        a_ab = [jnp.where(strict, sc[i][:two, :two], 0.0) for i in n]
        a_ak = [jnp.where(strict, sc[i][:two, two:], 0.0).astype(BF16) for i in n]
        r_b = [jnp.where(incl, sc[i][two:, :two], 0.0).astype(BF16) for i in n]
        r_k = [jnp.where(incl, sc[i][two:, two:], 0.0).astype(BF16) for i in n]
        akv = [_dot(a_ak[i], v_st[i]).astype(BF16) for i in n]
        rkv = [_dot(r_k[i], v_st[i]) for i in n]
        inv = [eye + a_ab[i] for i in n]
        pw = [a_ab[i].astype(BF16) for i in n]
        for _ in range(5):
            pw = [_dot(pw[i], pw[i]).astype(BF16) for i in n]
            inv = [inv[i] + _dot(inv[i].astype(BF16), pw[i]) for i in n]
        pu = [_dot(inv[i].astype(BF16), jnp.concatenate([a_bf[i], akv[i]], axis=1)).astype(BF16) for i in n]
        rb_pu = [_dot(r_b[i], pu[i]) for i in n]
        m = [_dot_tn(pu[i][:, :LANE], bkh_st[i][:two, :]).astype(BF16) for i in n]
        n0 = [_dot_tn(jnp.concatenate([pu[i][:, LANE:], v_st[i]], axis=0), bkh_st[i]) for i in n]
        rkr = [_segsum(r[i] * k[i] * rk_ref[...], ones) for i in n]
        return [(fold(r_st[i] + rb_pu[i][:, :LANE]).astype(BF16), fold(rb_pu[i][:, LANE:] + rkv[i]),
                 m[i], n0[i], jnp.exp(last[i]), rkr[i] * v[i]) for i in n]

    def advance(sl, p2, y0, m, n0, dec, bonus):
        state = st_ref[...]
        s_hi, s_lo = _split(state)
        y = _dot_nt(p2, s_hi) + y0
        st_ref[...] = state * dec + _dot(s_hi, m) + _dot(s_lo, m) + n0
        d = y - _segsum(y, ones) * (1.0 / HALF)
        var = _segsum(d * d, ones) * (1.0 / HALF)
        y = d * lax.rsqrt(var + RWKV_LN_EPS) * lnw_ref[...] + lnb_ref[...]
        o_ref[sl, :] = ((y + bonus) * g_ref[sl, :]).astype(o_ref.dtype)

    gsz = min(RWKV_GROUP, nchunk)

    def group(gi, carry):
        sls = [pl.ds(pl.multiple_of((gi * gsz + g) * CHUNK, CHUNK), CHUNK) for g in range(gsz)]
        for sl, part in zip(sls, transitions(sls)):
            advance(sl, *part)
        return carry

    lax.fori_loop(0, nchunk // gsz, group, 0)


def _rwkv_scan(r, lw, k, v, nk, kb, g, r_k, ln_w, ln_b, bsz, seq, tc=1024):
    tc = min(tc, seq)
    ns = seq // tc
    npair = GROUP // LANE
    blk = pl.BlockSpec((tc, LANE), lambda b, p, i: (b * ns + i, p))
    vec = pl.BlockSpec((1, LANE), lambda b, p, i: (0, p))
    return pl.pallas_call(
        _rwkv_body,
        out_shape=jax.ShapeDtypeStruct((bsz * seq, GROUP), BF16),
        grid=(bsz, npair, ns),
        in_specs=[blk] * 7 + [vec] * 3,
        out_specs=blk,
        scratch_shapes=[pltpu.VMEM((LANE, LANE), F32)],
        compiler_params=_cparams(("parallel", "parallel", "arbitrary")),
        name="rwkv_scan",
    )(r, lw, k, v, nk, kb, g, r_k.reshape(1, GROUP), ln_w.reshape(1, GROUP), ln_b.reshape(1, GROUP))


def _pad_rows(w, start):
    full = jnp.zeros((SMALL, w.shape[1]), F32).at[start:start + w.shape[0]].set(w)
    hi = full.astype(BF16)
    return jnp.stack([hi, (full - hi.astype(F32)).astype(BF16)])


def _layer(x, c8, layer_idx, v_first, vres, bias_ad, bias_as, bias_d, p, attn_tile):
    bsz, seq, d = x.shape
    t = bsz * seq
    mod = _ada(c8, p["ada_w"], p["ada_b"], layer_idx)
    mod3 = mod[:bsz].reshape(bsz * 6, 1, d)
    h = _norm_mod(x, mod3, 1, 0)

    vres_cols = vres[0] if vres is not None else jnp.zeros((d, 32), F32)
    w_r = _regroup_w_in(p["w_in"], layer_idx, vres_cols)
    cols, tail = _in_proj(h, w_r)

    qk, vt = _prep_a(cols, p["diff_q_norm"], p["diff_k_norm"])
    o_a = _diff_attention(qk, vt, bias_ad, bias_as, p["diff_lambda"], p["diff_subln"], bsz, seq, layer_idx,
                          attn_tile)
    o_dd = _swa_attention(cols, tail, bias_d, p["swa_q_norm"], p["swa_k_norm"], p["swa_sinks"], bsz, seq)
    mu = p["rwkv_mu"]
    vres_mu = vres[1] if vres is not None else jnp.zeros((32,), F32)
    mu_small = jnp.concatenate([jnp.zeros((S_WD,), F32), mu[3 * GROUP:], vres_mu,
                                jnp.zeros((SMALL - S_VR - 32,), F32)]).reshape(1, SMALL)
    v_up = _pad_rows(vres[2], S_VR) if vres is not None else jnp.zeros((2, SMALL, GROUP), BF16)
    v0 = vres[3] if vres is not None else jnp.zeros((GROUP,), F32)
    r_, lw_, k_, v_, nk_, kb_, g_, la_ = _prep_c(
        cols, tail, seq, mu[:3 * GROUP].reshape(1, -1), mu_small,
        _pad_rows(p["rwkv_w_up"], S_WD), _pad_rows(p["rwkv_a_up"], S_AD), _pad_rows(p["rwkv_g_up"], S_GD),
        v_up, _pad_rows(p["gla_gate_up"], S_GLA),
        p["rwkv_w0"].reshape(1, -1), p["rwkv_a0"].reshape(1, -1), v0.reshape(1, -1),
        p["rwkv_k_k"].reshape(1, -1), p["rwkv_k_a"].reshape(1, -1), p["gla_gate_bias"].reshape(1, -1),
        v_first if vres is not None else None)
    if vres is None:
        v_first = v_
    o_bb = _gla(cols, la_, p["gla_out_norm"], bsz, seq)
    o_c = _rwkv_scan(r_, lw_, k_, v_, nk_, kb_, g_, p["rwkv_r_k"].reshape(-1), p["rwkv_ln_w"], p["rwkv_ln_b"],
                     bsz, seq)

    x = _out_proj([o_a, o_bb, o_c, o_dd], p["w_out"], x, mod3, 2)
    h2 = _norm_mod(x, mod3, 4, 3)
    act = _ffn_up(h2, p["ffn_w1"], p["ffn_w3"], layer_idx)
    x = _ffn_down(act, p["ffn_w2"], x, mod3, 5)
    return x, v_first


def kernel(x, c, rel_bias, ada_w, ada_b, w_in, w_out, diff_q_norm, diff_k_norm, diff_lambda, diff_subln,
           gla_gate_up, gla_gate_bias, gla_out_norm, rwkv_mu, rwkv_w_up, rwkv_w0, rwkv_a_up, rwkv_a0,
           rwkv_g_up, rwkv_k_k, rwkv_k_a, rwkv_r_k, rwkv_ln_w, rwkv_ln_b, rwkv_vres_down, rwkv_vres_mu,
           rwkv_vres_up, rwkv_v0, swa_q_norm, swa_k_norm, swa_sinks, ffn_w1, ffn_w3, ffn_w2):
    bsz, seq, _ = x.shape
    depth = ada_w.shape[0]
    attn_tile = min(1024, seq)
    bias_ad = _bias_tiles(rel_bias[:, :DIFF_HEADS], attn_tile, attn_tile, 0, None, True, True, LOG2E)
    bias_as = _bias_tiles(rel_bias[:, :DIFF_HEADS], attn_tile, attn_tile, attn_tile, None, True, True, LOG2E)
    bias_d = _bias_tiles(rel_bias[:, DIFF_HEADS:], SWA_BLOCK, 2 * SWA_BLOCK, SWA_BLOCK, SWA_BLOCK, False)
    c8 = jnp.zeros((8, c.shape[1]), F32).at[:bsz].set(c)
    v_first = None
    for l in range(depth):
        p = dict(ada_w=ada_w, ada_b=ada_b, w_in=w_in, w_out=_cast_bf16(w_out, l),
                 diff_q_norm=diff_q_norm[l], diff_k_norm=diff_k_norm[l], diff_lambda=diff_lambda[l],
                 diff_subln=diff_subln[l], gla_gate_up=gla_gate_up[l], gla_gate_bias=gla_gate_bias[l],
                 gla_out_norm=gla_out_norm[l], rwkv_mu=rwkv_mu[l], rwkv_w_up=rwkv_w_up[l],
                 rwkv_w0=rwkv_w0[l], rwkv_a_up=rwkv_a_up[l], rwkv_a0=rwkv_a0[l], rwkv_g_up=rwkv_g_up[l],
                 rwkv_k_k=rwkv_k_k[l], rwkv_k_a=rwkv_k_a[l], rwkv_r_k=rwkv_r_k[l], rwkv_ln_w=rwkv_ln_w[l],
                 rwkv_ln_b=rwkv_ln_b[l], swa_q_norm=swa_q_norm[l], swa_k_norm=swa_k_norm[l],
                 swa_sinks=swa_sinks[l], ffn_w1=ffn_w1, ffn_w3=ffn_w3, ffn_w2=_cast_bf16(ffn_w2, l))
        vres = None if l == 0 else (rwkv_vres_down[l - 1], rwkv_vres_mu[l - 1], rwkv_vres_up[l - 1],
                                    rwkv_v0[l - 1])
        x, v_first = _layer(x, c8, l, v_first, vres, bias_ad, bias_as, bias_d, p, attn_tile)
    return x
```

```python
import functools
import math

import jax
import jax.numpy as jnp
from jax import lax
from jax.experimental import pallas as pl
from jax.experimental.pallas import tpu as pltpu

F32 = jnp.float32
BF16 = jnp.bfloat16
HI = lax.Precision.HIGHEST

D_MODEL = 4096
GROUP = 1024
D_FF = 11008
NORM_EPS = 1e-6
NEG_INF = -1e30
LANE = 128
HALF = 64

DIFF_HEADS = 8
DIFF_QK = 64
GLA_HEADS = 4
GLA_DK = 128
GLA_DV = 256
GLA_RANK = 16
GLA_NORMALIZER = 16.0
CHUNK = 64
SCAN_GROUP = 8
RWKV_GROUP = 16
RWKV_LN_EPS = 64e-5
SWA_HEADS = 16
SWA_KV_HEADS = 2
SWA_BLOCK = 128
REL_BUCKETS = 32
REL_MAX_DIST = 128

OFF_A = 0
OFF_B = 3072
OFF_C = 6144
OFF_D = 9216
OFF_S = 10496
SMALL = 256
NCOLS = OFF_S + SMALL
S_GLA, S_WD, S_AD, S_GD, S_VR = 0, 16, 80, 144, 208
TAIL = 512
OFF_T = NCOLS - TAIL

VMEM_LIMIT = 56 * 1024 * 1024


def _cparams(sem, vmem=VMEM_LIMIT):
    return pltpu.CompilerParams(dimension_semantics=sem, vmem_limit_bytes=vmem)


def _dot(a, b, prec=None):
    return jnp.dot(a, b, preferred_element_type=F32, precision=prec)


def _dot_nt(a, b, prec=None):
    return lax.dot_general(a, b, (((1,), (1,)), ((), ())), preferred_element_type=F32, precision=prec)


def _dot_tn(a, b, prec=None):
    return lax.dot_general(a, b, (((0,), (0,)), ((), ())), preferred_element_type=F32, precision=prec)


def _sigmoid(z):
    return 1.0 / (1.0 + jnp.exp(-z))


def _softplus(z):
    return jnp.maximum(z, 0.0) + jnp.log(1.0 + jnp.exp(-jnp.abs(z)))


def _seg_ones():
    r = lax.broadcasted_iota(jnp.int32, (LANE, LANE), 0) // HALF
    c = lax.broadcasted_iota(jnp.int32, (LANE, LANE), 1) // HALF
    return (r == c).astype(BF16)


def _split(x):
    hi = x.astype(BF16)
    return hi, (x - hi.astype(F32)).astype(BF16)


def _dot_left2(x, w):
    hi, lo = _split(x)
    return _dot(hi, w) + _dot(lo, w)


def _tri_cumsum(tri, x):
    hi, lo = _split(x)
    return _dot(tri, hi) + _dot(tri, lo)


def _segsum(x, ones):
    return _dot_left2(x, ones)


def _ada_body(c_ref, w_ref, b_ref, o_ref):
    c = c_ref[...]
    s = (c * _sigmoid(c)).astype(BF16)
    o_ref[...] = _dot(s, w_ref[...].astype(BF16)) + b_ref[...]


def _ada(c8, ada_w, ada_b, layer, tn=512):
    _, d, n = ada_w.shape
    return pl.pallas_call(
        _ada_body,
        out_shape=jax.ShapeDtypeStruct((8, n), F32),
        grid=(n // tn,),
        in_specs=[pl.BlockSpec((8, d), lambda j: (0, 0)),
                  pl.BlockSpec((None, d, tn), lambda j: (layer, 0, j)),
                  pl.BlockSpec((1, tn), lambda j: (0, j))],
        out_specs=pl.BlockSpec((8, tn), lambda j: (0, j)),
        compiler_params=_cparams(("parallel",)),
        name="ada_mod",
    )(c8, ada_w, ada_b[layer].reshape(1, n))


def _cast_body(w_ref, o_ref):
    o_ref[...] = w_ref[...].astype(o_ref.dtype)


def _cast_bf16(w, layer, tr=256):
    _, rows, cols = w.shape
    return pl.pallas_call(
        _cast_body,
        out_shape=jax.ShapeDtypeStruct((rows, cols), BF16),
        grid=(rows // tr,),
        in_specs=[pl.BlockSpec((None, tr, cols), lambda i: (layer, i, 0))],
        out_specs=pl.BlockSpec((tr, cols), lambda i: (i, 0)),
        compiler_params=_cparams(("parallel",)),
        name="weight_cast",
    )(w)


P_IN = 10704
SRC_B, SRC_C, SRC_D = 3072, 6160, 9424
SRC_BS, SRC_CS = SRC_B + 3072, SRC_C + 3072


PIECE = 16
PIECES = LANE // PIECE


def _piece_index(j, s):
    n_ab, n_c, n_d = OFF_C // LANE, OFF_D // LANE, OFF_S // LANE
    small0 = SRC_BS // PIECE if s == 0 else SRC_CS // PIECE + s - 1
    small1 = SRC_CS // PIECE + PIECES - 1 + min(s, 4)
    return jnp.where(j < n_ab, j * PIECES + s,
           jnp.where(j < n_c, SRC_C // PIECE + (j - n_ab) * PIECES + s,
           jnp.where(j < n_d, SRC_D // PIECE + (j - n_c) * PIECES + s,
           jnp.where(j == n_d, small0, small1))))


def _regroup_body(*refs):
    w_refs, vr_ref, o_ref = refs[:PIECES], refs[PIECES], refs[PIECES + 1]
    last = pl.program_id(0) == pl.num_programs(0) - 1
    pieces = [r[...] for r in w_refs]
    pieces[5] = jnp.where(last, vr_ref[:PIECE, :], pieces[5])
    pieces[6] = jnp.where(last, vr_ref[PIECE:, :], pieces[6])
    pieces[7] = jnp.where(last, 0.0, pieces[7])
    o_ref[...] = jnp.concatenate(pieces, axis=0).T.astype(o_ref.dtype)


def _regroup_w_in(w_in, layer, vres_down):
    _, d, _ = w_in.shape
    w_t = jnp.swapaxes(w_in, 1, 2)
    piece = lambda s: pl.BlockSpec((None, PIECE, d), lambda j: (layer, _piece_index(j, s), 0))
    return pl.pallas_call(
        _regroup_body,
        out_shape=jax.ShapeDtypeStruct((d, NCOLS), BF16),
        grid=(NCOLS // LANE,),
        in_specs=[piece(s) for s in range(PIECES)] + [pl.BlockSpec((2 * PIECE, d), lambda j: (0, 0))],
        out_specs=pl.BlockSpec((d, LANE), lambda j: (0, j)),
        compiler_params=_cparams(("parallel",)),
        name="w_in_regroup",
    )(*([w_t] * PIECES), vres_down.T)


def _norm_body(x_ref, sc_ref, sh_ref, o_ref):
    x = x_ref[0]
    ms = jnp.mean(x * x, axis=-1, keepdims=True)
    h = x * lax.rsqrt(ms + NORM_EPS) * (1.0 + sc_ref[0]) + sh_ref[0]
    o_ref[...] = h.astype(o_ref.dtype)


def _norm_mod(x, mod3, sc_idx, sh_idx, ts=256):
    bsz, seq, d = x.shape
    ts = min(ts, seq)
    ns = seq // ts
    return pl.pallas_call(
        _norm_body,
        out_shape=jax.ShapeDtypeStruct((bsz * seq, d), BF16),
        grid=(bsz, ns),
        in_specs=[pl.BlockSpec((1, ts, d), lambda b, i: (b, i, 0)),
                  pl.BlockSpec((1, 1, d), lambda b, i: (b * 6 + sc_idx, 0, 0)),
                  pl.BlockSpec((1, 1, d), lambda b, i: (b * 6 + sh_idx, 0, 0))],
        out_specs=pl.BlockSpec((ts, d), lambda b, i: (b * ns + i, 0)),
        compiler_params=_cparams(("parallel", "parallel")),
        name="norm_mod",
    )(x, mod3, mod3)


def _in_proj_body(a_ref, b_ref, main_ref, tail_ref):
    j = pl.program_id(1)
    acc = _dot(a_ref[...], b_ref[...])

    @pl.when(j < pl.num_programs(1) - 1)
    def _():
        main_ref[...] = acc.astype(main_ref.dtype)

    @pl.when(j == pl.num_programs(1) - 1)
    def _():
        tail_ref[...] = acc


def _in_proj(a, b, tm=1024):
    m, k = a.shape
    tm = min(tm, m)
    n_main = OFF_T // TAIL
    return pl.pallas_call(
        _in_proj_body,
        out_shape=[jax.ShapeDtypeStruct((m, OFF_T), BF16), jax.ShapeDtypeStruct((m, TAIL), F32)],
        grid=(m // tm, n_main + 1),
        in_specs=[pl.BlockSpec((tm, k), lambda i, j: (i, 0)),
                  pl.BlockSpec((k, TAIL), lambda i, j: (0, j))],
        out_specs=[pl.BlockSpec((tm, TAIL), lambda i, j: (i, jnp.minimum(j, n_main - 1))),
                   pl.BlockSpec((tm, TAIL), lambda i, j: (i, 0))],
        compiler_params=_cparams(("parallel", "arbitrary")),
        name="in_proj",
    )(a, b)


def _outproj_body(a0, a1, a2, a3, w_ref, x_ref, g_ref, o_ref):
    acc = _dot(a0[...], w_ref[0 * GROUP:1 * GROUP, :])
    acc += _dot(a1[...], w_ref[1 * GROUP:2 * GROUP, :])
    acc += _dot(a2[...], w_ref[2 * GROUP:3 * GROUP, :])
    acc += _dot(a3[...], w_ref[3 * GROUP:4 * GROUP, :])
    o_ref[0] = x_ref[0] + g_ref[0] * acc


def _out_proj(parts, w, x, mod3, gate_idx, tm=1024, tn=512):
    bsz, seq, d = x.shape
    tm = min(tm, seq)
    ns = seq // tm
    a_spec = pl.BlockSpec((tm, GROUP), lambda b, i, j: (b * ns + i, 0))
    return pl.pallas_call(
        _outproj_body,
        out_shape=jax.ShapeDtypeStruct((bsz, seq, d), F32),
        grid=(bsz, ns, d // tn),
        in_specs=[a_spec, a_spec, a_spec, a_spec,
                  pl.BlockSpec((4 * GROUP, tn), lambda b, i, j: (0, j)),
                  pl.BlockSpec((1, tm, tn), lambda b, i, j: (b, i, j)),
                  pl.BlockSpec((1, 1, tn), lambda b, i, j: (b * 6 + gate_idx, 0, j))],
        out_specs=pl.BlockSpec((1, tm, tn), lambda b, i, j: (b, i, j)),
        compiler_params=_cparams(("parallel", "parallel", "parallel")),
        name="out_proj",
    )(*parts, w, x, mod3)


def _ffn_up_body(a_ref, w1_ref, w3_ref, o_ref):
    a = a_ref[...]
    u = _dot(a, w1_ref[...].astype(BF16))
    v = _dot(a, w3_ref[...].astype(BF16))
    o_ref[...] = (u * _sigmoid(u) * v).astype(o_ref.dtype)


def _ffn_up(h, w1, w3, layer, tm=1024, tn=256):
    m, k = h.shape
    n = w1.shape[2]
    tm = min(tm, m)
    return pl.pallas_call(
        _ffn_up_body,
        out_shape=jax.ShapeDtypeStruct((m, n), BF16),
        grid=(m // tm, n // tn),
        in_specs=[pl.BlockSpec((tm, k), lambda i, j: (i, 0)),
                  pl.BlockSpec((None, k, tn), lambda i, j: (layer, 0, j)),
                  pl.BlockSpec((None, k, tn), lambda i, j: (layer, 0, j))],
        out_specs=pl.BlockSpec((tm, tn), lambda i, j: (i, j)),
        compiler_params=_cparams(("parallel", "parallel")),
        name="ffn_up",
    )(h, w1, w3)


def _ffn_down_body(a_ref, w_ref, x_ref, g_ref, o_ref):
    o_ref[0] = x_ref[0] + g_ref[0] * _dot(a_ref[...], w_ref[...])


def _ffn_down(a, w, x, mod3, gate_idx, tm=512, tn=512):
    bsz, seq, d = x.shape
    k = a.shape[1]
    tm = min(tm, seq)
    ns = seq // tm
    return pl.pallas_call(
        _ffn_down_body,
        out_shape=jax.ShapeDtypeStruct((bsz, seq, d), F32),
        grid=(bsz, ns, d // tn),
        in_specs=[pl.BlockSpec((tm, k), lambda b, i, j: (b * ns + i, 0)),
                  pl.BlockSpec((k, tn), lambda b, i, j: (0, j)),
                  pl.BlockSpec((1, tm, tn), lambda b, i, j: (b, i, j)),
                  pl.BlockSpec((1, 1, tn), lambda b, i, j: (b * 6 + gate_idx, 0, j))],
        out_specs=pl.BlockSpec((1, tm, tn), lambda b, i, j: (b, i, j)),
        compiler_params=_cparams(("parallel", "parallel", "parallel")),
        name="ffn_down",
    )(a, w, x, mod3)


def _bias_body(tab_ref, o_ref, *, off, window, shift, key_major, scale):
    h = pl.program_id(0)
    rows, cols = o_ref.shape[1], o_ref.shape[2]
    last = tab_ref[REL_BUCKETS - 1, h]
    max_exact = REL_BUCKETS // 2

    def block(d0):
        qi = lax.broadcasted_iota(jnp.int32, (LANE, LANE), 1 if key_major else 0)
        kj = lax.broadcasted_iota(jnp.int32, (LANE, LANE), 0 if key_major else 1)
        dist = d0 + qi - kj
        n = jnp.maximum(dist, 0)
        nf = jnp.maximum(n, 1).astype(F32)
        large = max_exact + (jnp.log(nf / max_exact) / math.log(REL_MAX_DIST / max_exact)
                             * (REL_BUCKETS - max_exact)).astype(jnp.int32)
        large = jnp.minimum(large, REL_BUCKETS - 1)
        bucket = jnp.where(n < max_exact, n, large)
        bias = jnp.full((LANE, LANE), last, F32)
        for b in range(REL_BUCKETS - 1):
            bias = jnp.where(bucket == b, tab_ref[b, h], bias)
        if shift:
            bias = bias - last
        if scale != 1.0:
            bias = bias * scale
        valid = dist >= 0
        if window is not None:
            valid = jnp.logical_and(valid, dist < window)
        return jnp.where(valid, bias, NEG_INF)

    cache = {}
    for br in range(rows // LANE):
        for bc in range(cols // LANE):
            qb, kb = (bc, br) if key_major else (br, bc)
            d0 = off + LANE * (qb - kb)
            if d0 not in cache:
                if d0 + LANE <= 0 or (window is not None and d0 - LANE >= window):
                    cache[d0] = jnp.full((LANE, LANE), NEG_INF, F32)
                elif d0 - LANE >= REL_MAX_DIST and (window is None or d0 + LANE <= window):
                    const = (0.0 if shift else last) * scale
                    cache[d0] = jnp.full((LANE, LANE), const, F32)
                else:
                    cache[d0] = block(d0)
            o_ref[0, br * LANE:(br + 1) * LANE, bc * LANE:(bc + 1) * LANE] = cache[d0]


def _bias_tiles(table, rows, cols, off, window, shift, key_major=False, scale=1.0):
    nh = table.shape[1]
    return pl.pallas_call(
        functools.partial(_bias_body, off=off, window=window, shift=shift, key_major=key_major, scale=scale),
        out_shape=jax.ShapeDtypeStruct((nh, rows, cols), F32),
        grid=(nh,),
        in_specs=[pl.BlockSpec(memory_space=pltpu.SMEM)],
        out_specs=pl.BlockSpec((1, rows, cols), lambda h: (h, 0, 0)),
        compiler_params=_cparams(("parallel",)),
        name="rel_bias_tiles",
    )(table)


LOG2E = 1.4426950408889634
VT_ROWS = LANE + 16
Q_CHUNK = 512


def _prep_a_body(c_ref, qg_ref, kg_ref, qk_ref, vt_ref):
    ones = _seg_ones()
    scale = DIFF_QK ** -0.5 * LOG2E
    tm = c_ref.shape[0]
    for j in range(2 * DIFF_HEADS):
        x = c_ref[:, j * LANE:(j + 1) * LANE].astype(F32)
        ms = _segsum(x * x, ones) * (1.0 / DIFF_QK)
        gain = qg_ref[...] * scale if j < DIFF_HEADS else kg_ref[...]
        qk_ref[:, j * LANE:(j + 1) * LANE] = (x * lax.rsqrt(ms + NORM_EPS) * gain).astype(qk_ref.dtype)
    for h in range(DIFF_HEADS):
        v = c_ref[:, 2 * GROUP + h * LANE:2 * GROUP + (h + 1) * LANE].astype(F32)
        vt_ref[h * VT_ROWS:h * VT_ROWS + LANE, :] = v.T.astype(vt_ref.dtype)
        vt_ref[h * VT_ROWS + LANE:(h + 1) * VT_ROWS, :] = jnp.ones((VT_ROWS - LANE, tm), vt_ref.dtype)


def _prep_a(cols, q_gain, k_gain, tm=256):
    t = cols.shape[0]
    tm = min(tm, t)
    qg = jnp.tile(q_gain, 2).reshape(1, LANE)
    kg = jnp.tile(k_gain, 2).reshape(1, LANE)
    vec = pl.BlockSpec((1, LANE), lambda i: (0, 0))
    return pl.pallas_call(
        _prep_a_body,
        out_shape=[jax.ShapeDtypeStruct((t, 2 * GROUP), BF16),
                   jax.ShapeDtypeStruct((DIFF_HEADS * VT_ROWS, t), BF16)],
        grid=(t // tm,),
        in_specs=[pl.BlockSpec((tm, 3 * GROUP), lambda i: (i, OFF_A // (3 * GROUP))), vec, vec],
        out_specs=[pl.BlockSpec((tm, 2 * GROUP), lambda i: (i, 0)),
                   pl.BlockSpec((DIFF_HEADS * VT_ROWS, tm), lambda i: (0, i))],
        compiler_params=_cparams(("parallel",)),
        name="diff_prep",
    )(cols, qg, kg)


def _flash_body(it_ref, jt_ref, q_ref, k_ref, vt_ref, vtp_ref, bd_ref, bs_ref, lam_ref, sub_ref, o_ref,
                qlo, qhi, m1, a1, m2, a2, p_scr, al_scr, *, lam_init):
    t = pl.program_id(2)
    i = it_ref[t]
    j = jt_ref[t]

    @pl.when(j == 0)
    def _():
        q = q_ref[...]
        lane = lax.broadcasted_iota(jnp.int32, q.shape, 1)
        zero = jnp.zeros_like(q)
        qlo[...] = jnp.where(lane < HALF, q, zero)
        qhi[...] = jnp.where(lane >= HALF, q, zero)
        for m, a in ((m1, a1), (m2, a2)):
            m[...] = jnp.full(m.shape, NEG_INF, F32)
            a[...] = jnp.zeros(a.shape, F32)
        p_scr[...] = jnp.zeros(p_scr.shape, BF16)
        al_scr[...] = jnp.ones(al_scr.shape, F32)

    def step(bias, last):
        k = k_ref[...]
        vt = vt_ref[...]
        vtp = vtp_ref[...]
        nq = q_ref.shape[0]
        qc = min(Q_CHUNK, nq)
        chains = [(mi, qq, m, a, slice(c * qc, (c + 1) * qc))
                  for mi, (qq, m, a) in enumerate(((qlo, m1, a1), (qhi, m2, a2))) for c in range(nq // qc)]
        ahead = 2
        if bias is None:
            scores = [_dot_nt(k, qq[qs, :]) for _, qq, _, _, qs in chains[:ahead]]
        else:
            full = [_dot_nt(k, qq[...]) for qq in (qlo, qhi)]
            scores = [full[c[0]][:, c[4]] + bias[:, c[4]] for c in chains]
        for idx, (mi, _, m, a, qs) in enumerate(chains):
            if bias is None and idx + ahead < len(chains):
                _, qq_n, _, _, qs_n = chains[idx + ahead]
                scores.append(_dot_nt(k, qq_n[qs_n, :]))
            acc = al_scr[mi, :, qs] * a[:, qs] + _dot(vtp, p_scr[mi, :, qs])
            s = scores[idx]
            m_old = m[:, qs]
            m_new = jnp.maximum(m_old, jnp.max(s, axis=0, keepdims=True))
            p = jnp.exp2(s - m_new).astype(BF16)
            alpha = jnp.exp2(m_old - m_new)
            if last:
                acc = alpha * acc + _dot(vt, p)
            else:
                p_scr[mi, :, qs] = p
                al_scr[mi, :, qs] = alpha
            a[:, qs] = acc
            m[:, qs] = m_new

    @pl.when(j < i - 1)
    def _():
        step(None, False)

    @pl.when(j == i - 1)
    def _():
        step(bs_ref[0], False)

    @pl.when(j == i)
    def _():
        step(bd_ref[0], True)
        lam = lam_ref[...]
        e1 = jnp.exp(jnp.sum(lam[0:1] * lam[1:2], axis=-1, keepdims=True))
        e2 = jnp.exp(jnp.sum(lam[2:3] * lam[3:4], axis=-1, keepdims=True))
        lam_full = e1 - e2 + lam_init
        o = a1[:LANE, :] / a1[LANE:LANE + 1, :] - lam_full * (a2[:LANE, :] / a2[LANE:LANE + 1, :])
        ms = jnp.mean(o * o, axis=0, keepdims=True)
        o = o * lax.rsqrt(ms + NORM_EPS) * sub_ref[...] * (1.0 - lam_init)
        o_ref[...] = o.T.astype(o_ref.dtype)


def _diff_attention(qk, vt, bias_diag, bias_sub, lam, subln, bsz, seq, layer_idx, tile):
    nq = seq // tile
    lam_init = 0.8 - 0.6 * math.exp(-0.3 * layer_idx)
    pairs = [(i, j) for i in range(nq) for j in range(i + 1)]
    i_tab = jnp.array([ij[0] for ij in pairs], jnp.int32)
    j_tab = jnp.array([ij[1] for ij in pairs], jnp.int32)
    grid_spec = pltpu.PrefetchScalarGridSpec(
        num_scalar_prefetch=2,
        grid=(bsz, DIFF_HEADS, len(pairs)),
        in_specs=[pl.BlockSpec((tile, LANE), lambda b, h, t, it, jt: (b * nq + it[t], h)),
                  pl.BlockSpec((tile, LANE), lambda b, h, t, it, jt: (b * nq + jt[t], DIFF_HEADS + h)),
                  pl.BlockSpec((VT_ROWS, tile), lambda b, h, t, it, jt: (h, b * nq + jt[t])),
                  pl.BlockSpec((VT_ROWS, tile), lambda b, h, t, it, jt: (h, b * nq + jnp.maximum(jt[t] - 1, 0))),
                  pl.BlockSpec((1, tile, tile), lambda b, h, t, it, jt: (h, 0, 0)),
                  pl.BlockSpec((1, tile, tile), lambda b, h, t, it, jt: (h, 0, 0)),
                  pl.BlockSpec((4, DIFF_QK), lambda b, h, t, it, jt: (0, 0)),
                  pl.BlockSpec((LANE, 1), lambda b, h, t, it, jt: (0, 0))],
        out_specs=pl.BlockSpec((tile, LANE), lambda b, h, t, it, jt: (b * nq + it[t], h)),
        scratch_shapes=[pltpu.VMEM((tile, LANE), BF16), pltpu.VMEM((tile, LANE), BF16),
                        pltpu.VMEM((1, tile), F32), pltpu.VMEM((VT_ROWS, tile), F32),
                        pltpu.VMEM((1, tile), F32), pltpu.VMEM((VT_ROWS, tile), F32),
                        pltpu.VMEM((2, tile, tile), BF16), pltpu.VMEM((2, 1, tile), F32)])
    return pl.pallas_call(
        functools.partial(_flash_body, lam_init=lam_init),
        out_shape=jax.ShapeDtypeStruct((bsz * seq, GROUP), BF16),
        grid_spec=grid_spec,
        compiler_params=_cparams(("parallel", "parallel", "arbitrary")),
        name="diff_attention",
    )(i_tab, j_tab, qk, qk, vt, vt, bias_diag, bias_sub, lam, subln.reshape(LANE, 1))


def _swa_body(q_ref, kp_ref, kc_ref, vp_ref, vc_ref, bias_ref, qg_ref, kg_ref, sink_ref, o_ref):
    n = pl.program_id(1)
    g = pl.program_id(2)
    ones = _seg_ones()
    lane = lax.broadcasted_iota(jnp.int32, (SWA_BLOCK, LANE), 1)
    lo = lane < HALF

    def norm(x, gain):
        ms = _segsum(x * x, ones) * (1.0 / HALF)
        return x * lax.rsqrt(ms + NORM_EPS) * gain

    k = jnp.concatenate([norm(kp_ref[...], kg_ref[...]), norm(kc_ref[...], kg_ref[...])], axis=0)
    v = jnp.concatenate([vp_ref[...], vc_ref[...]], axis=0)
    k_sw = pltpu.roll(k, HALF, axis=1)
    v_sw = pltpu.roll(v, HALF, axis=1)
    first_head = g == 0
    lo2 = lax.broadcasted_iota(jnp.int32, (2 * SWA_BLOCK, LANE), 1) < HALF
    k_a = jnp.where(first_head, k, k_sw).astype(BF16)
    k_b = jnp.where(first_head, k_sw, k).astype(BF16)
    v_mine_lo = jnp.where(first_head, v, v_sw)
    v_mine_hi = jnp.where(first_head, v_sw, v)
    v_a = jnp.where(lo2, v_mine_lo, 0.0).astype(BF16)
    v_b = jnp.where(lo2, 0.0, v_mine_hi).astype(BF16)
    kcol = lax.broadcasted_iota(jnp.int32, (SWA_BLOCK, 2 * SWA_BLOCK), 1)
    pad = jnp.logical_and(n == 0, kcol < SWA_BLOCK)
    scale = HALF ** -0.5
    grp = SWA_HEADS // SWA_KV_HEADS
    heads = range(grp)
    q = [norm(q_ref[:, pr * LANE:(pr + 1) * LANE].astype(F32), qg_ref[...] * scale) for pr in range(grp // 2)]
    qm = [jnp.where(lo if hh % 2 == 0 else jnp.logical_not(lo), q[hh // 2], 0.0).astype(BF16) for hh in heads]
    s = [_dot_nt(qm[hh], k_a if hh % 2 == 0 else k_b) for hh in heads]
    s = [jnp.where(pad, NEG_INF, s[hh] + bias_ref[hh]) for hh in heads]
    sink = [sink_ref[g * grp + hh] for hh in heads]
    m = [jnp.maximum(jnp.max(s[hh], axis=-1, keepdims=True), sink[hh]) for hh in heads]
    p = [jnp.exp(s[hh] - m[hh]) for hh in heads]
    den = [jnp.sum(p[hh], axis=-1, keepdims=True) + jnp.exp(sink[hh] - m[hh]) for hh in heads]
    pv = [_dot(p[hh].astype(BF16), v_a if hh % 2 == 0 else v_b) for hh in heads]
    for pr in range(grp // 2):
        o = pv[2 * pr] / den[2 * pr] + pv[2 * pr + 1] / den[2 * pr + 1]
        o_ref[:, pr * LANE:(pr + 1) * LANE] = o.astype(o_ref.dtype)


def _swa_attention(cols, tail, bias, q_gain, k_gain, sinks, bsz, seq):
    nb = seq // SWA_BLOCK
    grp = SWA_HEADS // SWA_KV_HEADS
    qw = grp * HALF
    kcol = (OFF_D + GROUP - OFF_T) // LANE
    vcol = kcol + 1
    prev = lambda b, n, g: b * nb + jnp.maximum(n - 1, 0)
    cur = lambda b, n, g: b * nb + n
    vec = pl.BlockSpec((1, LANE), lambda b, n, g: (0, 0))
    return pl.pallas_call(
        _swa_body,
        out_shape=jax.ShapeDtypeStruct((bsz * seq, GROUP), BF16),
        grid=(bsz, nb, SWA_KV_HEADS),
        in_specs=[pl.BlockSpec((SWA_BLOCK, qw), lambda b, n, g: (cur(b, n, g), OFF_D // qw + g)),
                  pl.BlockSpec((SWA_BLOCK, LANE), lambda b, n, g: (prev(b, n, g), kcol)),
                  pl.BlockSpec((SWA_BLOCK, LANE), lambda b, n, g: (cur(b, n, g), kcol)),
                  pl.BlockSpec((SWA_BLOCK, LANE), lambda b, n, g: (prev(b, n, g), vcol)),
                  pl.BlockSpec((SWA_BLOCK, LANE), lambda b, n, g: (cur(b, n, g), vcol)),
                  pl.BlockSpec((grp, SWA_BLOCK, 2 * SWA_BLOCK), lambda b, n, g: (g, 0, 0)),
                  vec, vec,
                  pl.BlockSpec(memory_space=pltpu.SMEM)],
        out_specs=pl.BlockSpec((SWA_BLOCK, qw), lambda b, n, g: (cur(b, n, g), g)),
        compiler_params=_cparams(("parallel", "parallel", "parallel")),
        name="swa_attention",
    )(cols, tail, tail, tail, tail, bias,
      jnp.tile(q_gain, 2).reshape(1, LANE), jnp.tile(k_gain, 2).reshape(1, LANE), sinks)


def _prep_c_body(*refs, seq, with_vres):
    (cm_ref, cs_ref, pm_ref, ps_ref, mum_ref, mus_ref, wup_ref, aup_ref, gup_ref, vup_ref, glaup_ref,
     w0_ref, a0_ref, v0_ref, kk_ref, ka_ref, glab_ref) = refs[:17]
    rest = refs[17:]
    if with_vres:
        vf_ref, rest = rest[0], rest[1:]
    r_ref, lw_ref, k_ref, v_ref, nk_ref, kb_ref, g_ref, la_ref = rest
    tm = cm_ref.shape[0]
    i = pl.program_id(0)
    seq_start = (i * tm) % seq == 0

    def shifted(cur, prev_rows, mu):
        row = lax.broadcasted_iota(jnp.int32, cur.shape, 0)
        before = jnp.where(seq_start, 0.0, prev_rows[PREV_ROWS - 1:PREV_ROWS, :].astype(F32))
        prev = jnp.where(row == 0, before, pltpu.roll(cur, 1, axis=0))
        return cur + (prev - cur) * mu

    def low_rank(x, w2_ref):
        hi, lo = _split(x)
        return _dot(hi, w2_ref[0]) + _dot(lo, w2_ref[0]) + _dot(hi, w2_ref[1])

    cs = cs_ref[...]
    la_ref[...] = -_softplus(-(low_rank(cs, glaup_ref) + glab_ref[...])) * (1.0 / GLA_NORMALIZER)
    sm = shifted(cm_ref[...].astype(F32), pm_ref, mum_ref[...])
    ss = shifted(cs, ps_ref, mus_ref[...])
    r = sm[:, :GROUP]
    k = sm[:, GROUP:2 * GROUP]
    v = sm[:, 2 * GROUP:]
    if with_vres:
        gate = _sigmoid(v0_ref[...] + low_rank(ss, vup_ref))
        v = v + (vf_ref[...] - v) * gate
    w_log = -_softplus(-(w0_ref[...] + low_rank(jnp.tanh(ss), wup_ref))) - 0.5
    a = _sigmoid(a0_ref[...] + low_rank(ss, aup_ref))
    r_ref[...] = r.astype(r_ref.dtype)
    lw_ref[...] = -jnp.exp(w_log)
    v_ref[...] = v
    g_ref[...] = low_rank(_sigmoid(ss), gup_ref).astype(g_ref.dtype)
    k_ref[...] = (k * (1.0 + (a - 1.0) * ka_ref[...])).astype(k_ref.dtype)
    ones = _seg_ones()
    kk = k * kk_ref[...]
    for j in range(GROUP // LANE):
        sl = slice(j * LANE, (j + 1) * LANE)
        x = kk[:, sl]
        nrm = jnp.maximum(jnp.sqrt(_segsum(x * x, ones)), 1e-12)
        x = x / nrm
        nk_ref[:, sl] = x.astype(nk_ref.dtype)
        kb_ref[:, sl] = (x * a[:, sl]).astype(kb_ref.dtype)


PREV_ROWS = 16


def _prep_c(cols, tail, seq, mu_main, mu_small, w_up, a_up, g_up, v_up, gla_up, w0, a0, v0, k_k, k_a, gla_bias,
            v_first, tm=256):
    t = cols.shape[0]
    tm = min(tm, seq)
    with_vres = v_first is not None
    cmain = OFF_C // (3 * GROUP)
    csmall = (OFF_S - OFF_T) // SMALL
    prev = lambda i: jnp.maximum(i * (tm // PREV_ROWS) - 1, 0)
    full = lambda r, c: pl.BlockSpec((r, c), lambda i: (0, 0))
    hilo = lambda c: pl.BlockSpec((2, SMALL, c), lambda i: (0, 0, 0))
    in_specs = [pl.BlockSpec((tm, 3 * GROUP), lambda i: (i, cmain)),
                pl.BlockSpec((tm, SMALL), lambda i: (i, csmall)),
                pl.BlockSpec((PREV_ROWS, 3 * GROUP), lambda i: (prev(i), cmain)),
                pl.BlockSpec((PREV_ROWS, SMALL), lambda i: (prev(i), csmall)),
                full(1, 3 * GROUP), full(1, SMALL),
                hilo(GROUP), hilo(GROUP), hilo(GROUP), hilo(GROUP), hilo(GLA_HEADS * GLA_DK),
                full(1, GROUP), full(1, GROUP), full(1, GROUP), full(1, GROUP), full(1, GROUP),
                full(1, GLA_HEADS * GLA_DK)]
    args = [cols, tail, cols, tail, mu_main, mu_small, w_up, a_up, g_up, v_up, gla_up,
            w0, a0, v0, k_k, k_a, gla_bias]
    if with_vres:
        in_specs.append(pl.BlockSpec((tm, GROUP), lambda i: (i, 0)))
        args.append(v_first)
    row = pl.BlockSpec((tm, GROUP), lambda i: (i, 0))
    out = [jax.ShapeDtypeStruct((t, GROUP), dt) for dt in (BF16, F32, BF16, F32, BF16, BF16, BF16)]
    return pl.pallas_call(
        functools.partial(_prep_c_body, seq=seq, with_vres=with_vres),
        out_shape=out + [jax.ShapeDtypeStruct((t, GLA_HEADS * GLA_DK), F32)],
        grid=(t // tm,),
        in_specs=in_specs,
        out_specs=[row] * 7 + [pl.BlockSpec((tm, GLA_HEADS * GLA_DK), lambda i: (i, 0))],
        compiler_params=_cparams(("parallel",)),
        name="rwkv_gla_prep",
    )(*args)


def _gla_body(q_ref, k_ref, v_ref, g_ref, la_ref, gain_ref, o_ref, st_ref):
    @pl.when(pl.program_id(2) == 0)
    def _():
        st_ref[...] = jnp.zeros(st_ref.shape, F32)

    row = lax.broadcasted_iota(jnp.int32, (CHUNK, CHUNK), 0)
    col = lax.broadcasted_iota(jnp.int32, (CHUNK, CHUNK), 1)
    causal = row >= col
    tri = causal.astype(BF16)
    nchunk = q_ref.shape[0] // CHUNK

    def local(sls):
        n = range(len(sls))
        k = [k_ref[sl, :].astype(F32) for sl in sls]
        v = [v_ref[sl, :].astype(BF16) for sl in sls]
        b = [_tri_cumsum(tri, la_ref[sl, :]) for sl in sls]
        b_last = [b[i][CHUNK - 1:CHUNK, :] for i in n]
        q_dec = [(q_ref[sl, :].astype(F32) * (GLA_DK ** -0.5) * jnp.exp(b[i])).astype(BF16)
                 for i, sl in enumerate(sls)]
        a_intra = [jnp.where(causal, _dot_nt(q_dec[i], (k[i] * jnp.exp(-b[i])).astype(BF16)), 0.0) for i in n]
        upd = [_dot_tn(v[i], (k[i] * jnp.exp(b_last[i] - b[i])).astype(BF16)) for i in n]
        o_intra = [_dot(a_intra[i].astype(BF16), v[i]) for i in n]
        return [(q_dec[i], o_intra[i], upd[i], jnp.exp(b_last[i])) for i in n]

    def advance(sl, q_dec, o_intra, upd, dec):
        state = st_ref[...]
        s_hi, s_lo = _split(state)
        o = o_intra + _dot_nt(q_dec, s_hi) + _dot_nt(q_dec, s_lo)
        st_ref[...] = state * dec + upd
        ms = jnp.mean(o * o, axis=-1, keepdims=True)
        o = o * lax.rsqrt(ms + NORM_EPS) * gain_ref[...]
        gate = g_ref[sl, :].astype(F32)
        o_ref[sl, :] = (o * (gate * _sigmoid(gate))).astype(o_ref.dtype)

    def group(gi, carry):
        sls = [pl.ds(pl.multiple_of((gi * SCAN_GROUP + g) * CHUNK, CHUNK), CHUNK) for g in range(SCAN_GROUP)]
        for sl, part in zip(sls, local(sls)):
            advance(sl, *part)
        return carry

    lax.fori_loop(0, nchunk // SCAN_GROUP, group, 0)


def _gla(cols, log_a, out_gain, bsz, seq, tc=512):
    tc = min(tc, seq)
    ns = seq // tc
    qc = OFF_B // GLA_DK
    kc = qc + GLA_HEADS
    vc = (OFF_B + 2 * GLA_HEADS * GLA_DK) // GLA_DV
    gc = vc + GLA_HEADS
    rows = lambda b, h, i: b * ns + i
    return pl.pallas_call(
        _gla_body,
        out_shape=jax.ShapeDtypeStruct((bsz * seq, GROUP), BF16),
        grid=(bsz, GLA_HEADS, ns),
        in_specs=[pl.BlockSpec((tc, GLA_DK), lambda b, h, i: (rows(b, h, i), qc + h)),
                  pl.BlockSpec((tc, GLA_DK), lambda b, h, i: (rows(b, h, i), kc + h)),
                  pl.BlockSpec((tc, GLA_DV), lambda b, h, i: (rows(b, h, i), vc + h)),
                  pl.BlockSpec((tc, GLA_DV), lambda b, h, i: (rows(b, h, i), gc + h)),
                  pl.BlockSpec((tc, GLA_DK), lambda b, h, i: (rows(b, h, i), h)),
                  pl.BlockSpec((1, GLA_DV), lambda b, h, i: (0, 0))],
        out_specs=pl.BlockSpec((tc, GLA_DV), lambda b, h, i: (rows(b, h, i), h)),
        scratch_shapes=[pltpu.VMEM((GLA_DV, GLA_DK), F32)],
        compiler_params=_cparams(("parallel", "parallel", "arbitrary")),
        name="gla_scan",
    )(cols, cols, cols, cols, log_a, out_gain.reshape(1, GLA_DV))


def _rwkv_body(r_ref, lw_ref, k_ref, v_ref, nk_ref, kb_ref, g_ref, rk_ref, lnw_ref, lnb_ref, o_ref, st_ref):
    @pl.when(pl.program_id(2) == 0)
    def _():
        st_ref[...] = jnp.zeros(st_ref.shape, F32)

    two = 2 * CHUNK
    row = lax.broadcasted_iota(jnp.int32, (two, two), 0)
    col = lax.broadcasted_iota(jnp.int32, (two, two), 1)
    same = (row // CHUNK) == (col // CHUNK)
    strict = jnp.logical_and(same, (row % CHUNK) > (col % CHUNK))
    incl = jnp.logical_and(same, (row % CHUNK) >= (col % CHUNK))
    eye = (row == col).astype(F32)
    crow = lax.broadcasted_iota(jnp.int32, (CHUNK, CHUNK), 0)
    ccol = lax.broadcasted_iota(jnp.int32, (CHUNK, CHUNK), 1)
    tri = (crow >= ccol).astype(BF16)
    lo = lax.broadcasted_iota(jnp.int32, (CHUNK, LANE), 1) < HALF
    ones = _seg_ones()
    nchunk = r_ref.shape[0] // CHUNK

    def stack(x):
        return jnp.concatenate([jnp.where(lo, x, 0.0), jnp.where(lo, 0.0, x)], axis=0)

    def fold(x):
        return x[:CHUNK, :] + x[CHUNK:, :]

    def transitions(sls):
        n = range(len(sls))
        r = [r_ref[sl, :].astype(F32) for sl in sls]
        lw = [lw_ref[sl, :] for sl in sls]
        k = [k_ref[sl, :].astype(F32) for sl in sls]
        v = [v_ref[sl, :] for sl in sls]
        cum = [_tri_cumsum(tri, lw[i]) for i in n]
        last = [cum[i][CHUNK - 1:CHUNK, :] for i in n]
        a_bf, r_st, bk_st, bkh_st, v_st = [], [], [], [], []
        for i, sl in enumerate(sls):
            kb = kb_ref[sl, :].astype(F32)
            e_neg = jnp.exp(-cum[i])
            e_rem = jnp.exp(last[i] - cum[i])
            a_bf.append(stack(-nk_ref[sl, :].astype(F32) * jnp.exp(cum[i] - lw[i])).astype(BF16))
            r_st.append(stack(r[i] * jnp.exp(cum[i])))
            bk_st.append(jnp.concatenate([stack(kb * e_neg), stack(k[i] * e_neg)], axis=0).astype(BF16))
            bkh_st.append(jnp.concatenate([stack(kb * e_rem), stack(k[i] * e_rem)], axis=0).astype(BF16))
            v_st.append(stack(v[i]).astype(BF16))
        sc = [_dot_nt(jnp.concatenate([a_bf[i], r_st[i].astype(BF16)], axis=0), bk_st[i]) for i in n]
        a_ab = [jnp.where(strict, sc[i][:two, :two], 0.0) for i in n]
        a_ak = [jnp.where(strict, sc[i][:two, two:], 0.0).astype(BF16) for i in n]
        r_b = [jnp.where(incl, sc[i][two:, :two], 0.0).astype(BF16) for i in n]
        r_k = [jnp.where(incl, sc[i][two:, two:], 0.0).astype(BF16) for i in n]
        akv = [_dot(a_ak[i], v_st[i]).astype(BF16) for i in n]
        rkv = [_dot(r_k[i], v_st[i]) for i in n]
        inv = [eye + a_ab[i] for i in n]
        pw = [a_ab[i].astype(BF16) for i in n]
        for _ in range(5):
            pw = [_dot(pw[i], pw[i]).astype(BF16) for i in n]
            inv = [inv[i] + _dot(inv[i].astype(BF16), pw[i]) for i in n]
        pu = [_dot(inv[i].astype(BF16), jnp.concatenate([a_bf[i], akv[i]], axis=1)).astype(BF16) for i in n]
        rb_pu = [_dot(r_b[i], pu[i]) for i in n]
        m = [_dot_tn(pu[i][:, :LANE], bkh_st[i][:two, :]).astype(BF16) for i in n]
        n0 = [_dot_tn(jnp.concatenate([pu[i][:, LANE:], v_st[i]], axis=0), bkh_st[i]) for i in n]
        rkr = [_segsum(r[i] * k[i] * rk_ref[...], ones) for i in n]
        return [(fold(r_st[i] + rb_pu[i][:, :LANE]).astype(BF16), fold(rb_pu[i][:, LANE:] + rkv[i]),
                 m[i], n0[i], jnp.exp(last[i]), rkr[i] * v[i]) for i in n]

    def advance(sl, p2, y0, m, n0, dec, bonus):
        state = st_ref[...]
        s_hi, s_lo = _split(state)
        y = _dot_nt(p2, s_hi) + y0
        st_ref[...] = state * dec + _dot(s_hi, m) + _dot(s_lo, m) + n0
        d = y - _segsum(y, ones) * (1.0 / HALF)
        var = _segsum(d * d, ones) * (1.0 / HALF)
        y = d * lax.rsqrt(var + RWKV_LN_EPS) * lnw_ref[...] + lnb_ref[...]
        o_ref[sl, :] = ((y + bonus) * g_ref[sl, :]).astype(o_ref.dtype)

    gsz = min(RWKV_GROUP, nchunk)

    def group(gi, carry):
        sls = [pl.ds(pl.multiple_of((gi * gsz + g) * CHUNK, CHUNK), CHUNK) for g in range(gsz)]
        for sl, part in zip(sls, transitions(sls)):
            advance(sl, *part)
        return carry

    lax.fori_loop(0, nchunk // gsz, group, 0)


def _rwkv_scan(r, lw, k, v, nk, kb, g, r_k, ln_w, ln_b, bsz, seq, tc=1024):
    tc = min(tc, seq)
    ns = seq // tc
    npair = GROUP // LANE
    blk = pl.BlockSpec((tc, LANE), lambda b, p, i: (b * ns + i, p))
    vec = pl.BlockSpec((1, LANE), lambda b, p, i: (0, p))
    return pl.pallas_call(
        _rwkv_body,
        out_shape=jax.ShapeDtypeStruct((bsz * seq, GROUP), BF16),
        grid=(bsz, npair, ns),
        in_specs=[blk] * 7 + [vec] * 3,
        out_specs=blk,
        scratch_shapes=[pltpu.VMEM((LANE, LANE), F32)],
        compiler_params=_cparams(("parallel", "parallel", "arbitrary")),
        name="rwkv_scan",
    )(r, lw, k, v, nk, kb, g, r_k.reshape(1, GROUP), ln_w.reshape(1, GROUP), ln_b.reshape(1, GROUP))


def _pad_rows(w, start):
    full = jnp.zeros((SMALL, w.shape[1]), F32).at[start:start + w.shape[0]].set(w)
    hi = full.astype(BF16)
    return jnp.stack([hi, (full - hi.astype(F32)).astype(BF16)])


def _layer(x, c8, layer_idx, v_first, vres, bias_ad, bias_as, bias_d, p, attn_tile):
    bsz, seq, d = x.shape
    t = bsz * seq
    mod = _ada(c8, p["ada_w"], p["ada_b"], layer_idx)
    mod3 = mod[:bsz].reshape(bsz * 6, 1, d)
    h = _norm_mod(x, mod3, 1, 0)

    vres_cols = vres[0] if vres is not None else jnp.zeros((d, 32), F32)
    w_r = _regroup_w_in(p["w_in"], layer_idx, vres_cols)
    cols, tail = _in_proj(h, w_r)

    qk, vt = _prep_a(cols, p["diff_q_norm"], p["diff_k_norm"])
    o_a = _diff_attention(qk, vt, bias_ad, bias_as, p["diff_lambda"], p["diff_subln"], bsz, seq, layer_idx,
                          attn_tile)
    o_dd = _swa_attention(cols, tail, bias_d, p["swa_q_norm"], p["swa_k_norm"], p["swa_sinks"], bsz, seq)
    mu = p["rwkv_mu"]
    vres_mu = vres[1] if vres is not None else jnp.zeros((32,), F32)
    mu_small = jnp.concatenate([jnp.zeros((S_WD,), F32), mu[3 * GROUP:], vres_mu,
                                jnp.zeros((SMALL - S_VR - 32,), F32)]).reshape(1, SMALL)
    v_up = _pad_rows(vres[2], S_VR) if vres is not None else jnp.zeros((2, SMALL, GROUP), BF16)
    v0 = vres[3] if vres is not None else jnp.zeros((GROUP,), F32)
    r_, lw_, k_, v_, nk_, kb_, g_, la_ = _prep_c(
        cols, tail, seq, mu[:3 * GROUP].reshape(1, -1), mu_small,
        _pad_rows(p["rwkv_w_up"], S_WD), _pad_rows(p["rwkv_a_up"], S_AD), _pad_rows(p["rwkv_g_up"], S_GD),
        v_up, _pad_rows(p["gla_gate_up"], S_GLA),
        p["rwkv_w0"].reshape(1, -1), p["rwkv_a0"].reshape(1, -1), v0.reshape(1, -1),
        p["rwkv_k_k"].reshape(1, -1), p["rwkv_k_a"].reshape(1, -1), p["gla_gate_bias"].reshape(1, -1),
        v_first if vres is not None else None)
    if vres is None:
        v_first = v_
    o_bb = _gla(cols, la_, p["gla_out_norm"], bsz, seq)
    o_c = _rwkv_scan(r_, lw_, k_, v_, nk_, kb_, g_, p["rwkv_r_k"].reshape(-1), p["rwkv_ln_w"], p["rwkv_ln_b"],
                     bsz, seq)

    x = _out_proj([o_a, o_bb, o_c, o_dd], p["w_out"], x, mod3, 2)
    h2 = _norm_mod(x, mod3, 4, 3)
    act = _ffn_up(h2, p["ffn_w1"], p["ffn_w3"], layer_idx)
    x = _ffn_down(act, p["ffn_w2"], x, mod3, 5)
    return x, v_first


def kernel(x, c, rel_bias, ada_w, ada_b, w_in, w_out, diff_q_norm, diff_k_norm, diff_lambda, diff_subln,
           gla_gate_up, gla_gate_bias, gla_out_norm, rwkv_mu, rwkv_w_up, rwkv_w0, rwkv_a_up, rwkv_a0,
           rwkv_g_up, rwkv_k_k, rwkv_k_a, rwkv_r_k, rwkv_ln_w, rwkv_ln_b, rwkv_vres_down, rwkv_vres_mu,
           rwkv_vres_up, rwkv_v0, swa_q_norm, swa_k_norm, swa_sinks, ffn_w1, ffn_w3, ffn_w2):
    bsz, seq, _ = x.shape
    depth = ada_w.shape[0]
    attn_tile = min(1024, seq)
    bias_ad = _bias_tiles(rel_bias[:, :DIFF_HEADS], attn_tile, attn_tile, 0, None, True, True, LOG2E)
    bias_as = _bias_tiles(rel_bias[:, :DIFF_HEADS], attn_tile, attn_tile, attn_tile, None, True, True, LOG2E)
    bias_d = _bias_tiles(rel_bias[:, DIFF_HEADS:], SWA_BLOCK, 2 * SWA_BLOCK, SWA_BLOCK, SWA_BLOCK, False)
    c8 = jnp.zeros((8, c.shape[1]), F32).at[:bsz].set(c)
    v_first = None
    for l in range(depth):
        p = dict(ada_w=ada_w, ada_b=ada_b, w_in=w_in, w_out=_cast_bf16(w_out, l),
                 diff_q_norm=diff_q_norm[l], diff_k_norm=diff_k_norm[l], diff_lambda=diff_lambda[l],
                 diff_subln=diff_subln[l], gla_gate_up=gla_gate_up[l], gla_gate_bias=gla_gate_bias[l],
                 gla_out_norm=gla_out_norm[l], rwkv_mu=rwkv_mu[l], rwkv_w_up=rwkv_w_up[l],
                 rwkv_w0=rwkv_w0[l], rwkv_a_up=rwkv_a_up[l], rwkv_a0=rwkv_a0[l], rwkv_g_up=rwkv_g_up[l],
                 rwkv_k_k=rwkv_k_k[l], rwkv_k_a=rwkv_k_a[l], rwkv_r_k=rwkv_r_k[l], rwkv_ln_w=rwkv_ln_w[l],
                 rwkv_ln_b=rwkv_ln_b[l], swa_q_norm=swa_q_norm[l], swa_k_norm=swa_k_norm[l],
                 swa_sinks=swa_sinks[l], ffn_w1=ffn_w1, ffn_w3=ffn_w3, ffn_w2=_cast_bf16(ffn_w2, l))
        vres = None if l == 0 else (rwkv_vres_down[l - 1], rwkv_vres_mu[l - 1], rwkv_vres_up[l - 1],
                                    rwkv_v0[l - 1])
        x, v_first = _layer(x, c8, l, v_first, vres, bias_ad, bias_as, bias_d, p, attn_tile)
    return x
```

```python
import functools
import math

import jax
import jax.numpy as jnp
from jax import lax
from jax.experimental import pallas as pl
from jax.experimental.pallas import tpu as pltpu

F32 = jnp.float32
BF16 = jnp.bfloat16
HI = lax.Precision.HIGHEST

D_MODEL = 4096
GROUP = 1024
D_FF = 11008
NORM_EPS = 1e-6
NEG_INF = -1e30
LANE = 128
HALF = 64

DIFF_HEADS = 8
DIFF_QK = 64
GLA_HEADS = 4
GLA_DK = 128
GLA_DV = 256
GLA_RANK = 16
GLA_NORMALIZER = 16.0
CHUNK = 64
SCAN_GROUP = 8
RWKV_GROUP = 8
RWKV_TILES = 2
RWKV_LN_EPS = 64e-5
SWA_HEADS = 16
SWA_KV_HEADS = 2
SWA_BLOCK = 128
REL_BUCKETS = 32
REL_MAX_DIST = 128

OFF_A = 0
OFF_B = 3072
OFF_C = 6144
OFF_D = 9216
OFF_S = 10496
SMALL = 256
NCOLS = OFF_S + SMALL
S_GLA, S_WD, S_AD, S_GD, S_VR = 0, 16, 80, 144, 208
TAIL = 512
OFF_T = NCOLS - TAIL

VMEM_LIMIT = 56 * 1024 * 1024


def _cparams(sem, vmem=VMEM_LIMIT):
    return pltpu.CompilerParams(dimension_semantics=sem, vmem_limit_bytes=vmem)


def _dot(a, b, prec=None):
    return jnp.dot(a, b, preferred_element_type=F32, precision=prec)


def _dot_nt(a, b, prec=None):
    return lax.dot_general(a, b, (((1,), (1,)), ((), ())), preferred_element_type=F32, precision=prec)


def _dot_tn(a, b, prec=None):
    return lax.dot_general(a, b, (((0,), (0,)), ((), ())), preferred_element_type=F32, precision=prec)


def _sigmoid(z):
    return 1.0 / (1.0 + jnp.exp(-z))


def _softplus(z):
    return jnp.maximum(z, 0.0) + jnp.log(1.0 + jnp.exp(-jnp.abs(z)))


def _seg_ones():
    r = lax.broadcasted_iota(jnp.int32, (LANE, LANE), 0) // HALF
    c = lax.broadcasted_iota(jnp.int32, (LANE, LANE), 1) // HALF
    return (r == c).astype(BF16)


def _split(x):
    hi = x.astype(BF16)
    return hi, (x - hi.astype(F32)).astype(BF16)


def _dot_left2(x, w):
    hi, lo = _split(x)
    return _dot(hi, w) + _dot(lo, w)


def _tri_cumsum(tri, x):
    hi, lo = _split(x)
    return _dot(tri, hi) + _dot(tri, lo)


def _segsum(x, ones):
    return _dot_left2(x, ones)


def _ada_body(c_ref, w_ref, b_ref, o_ref):
    c = c_ref[...]
    s = (c * _sigmoid(c)).astype(BF16)
    o_ref[...] = _dot(s, w_ref[...].astype(BF16)) + b_ref[...]


def _ada(c8, ada_w, ada_b, layer, tn=512):
    _, d, n = ada_w.shape
    return pl.pallas_call(
        _ada_body,
        out_shape=jax.ShapeDtypeStruct((8, n), F32),
        grid=(n // tn,),
        in_specs=[pl.BlockSpec((8, d), lambda j: (0, 0)),
                  pl.BlockSpec((None, d, tn), lambda j: (layer, 0, j)),
                  pl.BlockSpec((1, tn), lambda j: (0, j))],
        out_specs=pl.BlockSpec((8, tn), lambda j: (0, j)),
        compiler_params=_cparams(("parallel",)),
        name="ada_mod",
    )(c8, ada_w, ada_b[layer].reshape(1, n))


def _cast_body(w_ref, o_ref):
    o_ref[...] = w_ref[...].astype(o_ref.dtype)


def _cast_bf16(w, layer, tr=256):
    _, rows, cols = w.shape
    return pl.pallas_call(
        _cast_body,
        out_shape=jax.ShapeDtypeStruct((rows, cols), BF16),
        grid=(rows // tr,),
        in_specs=[pl.BlockSpec((None, tr, cols), lambda i: (layer, i, 0))],
        out_specs=pl.BlockSpec((tr, cols), lambda i: (i, 0)),
        compiler_params=_cparams(("parallel",)),
        name="weight_cast",
    )(w)


P_IN = 10704
SRC_B, SRC_C, SRC_D = 3072, 6160, 9424
SRC_BS, SRC_CS = SRC_B + 3072, SRC_C + 3072


PIECE = 16
PIECES = LANE // PIECE


def _piece_index(j, s):
    n_ab, n_c, n_d = OFF_C // LANE, OFF_D // LANE, OFF_S // LANE
    small0 = SRC_BS // PIECE if s == 0 else SRC_CS // PIECE + s - 1
    small1 = SRC_CS // PIECE + PIECES - 1 + min(s, 4)
    return jnp.where(j < n_ab, j * PIECES + s,
           jnp.where(j < n_c, SRC_C // PIECE + (j - n_ab) * PIECES + s,
           jnp.where(j < n_d, SRC_D // PIECE + (j - n_c) * PIECES + s,
           jnp.where(j == n_d, small0, small1))))


def _regroup_body(*refs):
    w_refs, vr_ref, o_ref = refs[:PIECES], refs[PIECES], refs[PIECES + 1]
    last = pl.program_id(0) == pl.num_programs(0) - 1
    pieces = [r[...] for r in w_refs]
    pieces[5] = jnp.where(last, vr_ref[:PIECE, :], pieces[5])
    pieces[6] = jnp.where(last, vr_ref[PIECE:, :], pieces[6])
    pieces[7] = jnp.where(last, 0.0, pieces[7])
    o_ref[...] = jnp.concatenate(pieces, axis=0).T.astype(o_ref.dtype)


def _regroup_w_in(w_in, layer, vres_down):
    _, d, _ = w_in.shape
    w_t = jnp.swapaxes(w_in, 1, 2)
    piece = lambda s: pl.BlockSpec((None, PIECE, d), lambda j: (layer, _piece_index(j, s), 0))
    return pl.pallas_call(
        _regroup_body,
        out_shape=jax.ShapeDtypeStruct((d, NCOLS), BF16),
        grid=(NCOLS // LANE,),
        in_specs=[piece(s) for s in range(PIECES)] + [pl.BlockSpec((2 * PIECE, d), lambda j: (0, 0))],
        out_specs=pl.BlockSpec((d, LANE), lambda j: (0, j)),
        compiler_params=_cparams(("parallel",)),
        name="w_in_regroup",
    )(*([w_t] * PIECES), vres_down.T)


def _norm_body(x_ref, sc_ref, sh_ref, o_ref):
    x = x_ref[0]
    ms = jnp.mean(x * x, axis=-1, keepdims=True)
    h = x * lax.rsqrt(ms + NORM_EPS) * (1.0 + sc_ref[0]) + sh_ref[0]
    o_ref[...] = h.astype(o_ref.dtype)


def _norm_mod(x, mod3, sc_idx, sh_idx, ts=256):
    bsz, seq, d = x.shape
    ts = min(ts, seq)
    ns = seq // ts
    return pl.pallas_call(
        _norm_body,
        out_shape=jax.ShapeDtypeStruct((bsz * seq, d), BF16),
        grid=(bsz, ns),
        in_specs=[pl.BlockSpec((1, ts, d), lambda b, i: (b, i, 0)),
                  pl.BlockSpec((1, 1, d), lambda b, i: (b * 6 + sc_idx, 0, 0)),
                  pl.BlockSpec((1, 1, d), lambda b, i: (b * 6 + sh_idx, 0, 0))],
        out_specs=pl.BlockSpec((ts, d), lambda b, i: (b * ns + i, 0)),
        compiler_params=_cparams(("parallel", "parallel")),
        name="norm_mod",
    )(x, mod3, mod3)


def _in_proj_body(a_ref, b_ref, main_ref, tail_ref):
    j = pl.program_id(1)
    acc = _dot(a_ref[...], b_ref[...])

    @pl.when(j < pl.num_programs(1) - 1)
    def _():
        main_ref[...] = acc.astype(main_ref.dtype)

    @pl.when(j == pl.num_programs(1) - 1)
    def _():
        tail_ref[...] = acc


def _in_proj(a, b, tm=1024):
    m, k = a.shape
    tm = min(tm, m)
    n_main = OFF_T // TAIL
    return pl.pallas_call(
        _in_proj_body,
        out_shape=[jax.ShapeDtypeStruct((m, OFF_T), BF16), jax.ShapeDtypeStruct((m, TAIL), F32)],
        grid=(m // tm, n_main + 1),
        in_specs=[pl.BlockSpec((tm, k), lambda i, j: (i, 0)),
                  pl.BlockSpec((k, TAIL), lambda i, j: (0, j))],
        out_specs=[pl.BlockSpec((tm, TAIL), lambda i, j: (i, jnp.minimum(j, n_main - 1))),
                   pl.BlockSpec((tm, TAIL), lambda i, j: (i, 0))],
        compiler_params=_cparams(("parallel", "arbitrary")),
        name="in_proj",
    )(a, b)


def _outproj_body(a0, a1, a2, a3, w_ref, x_ref, g_ref, o_ref):
    acc = _dot(a0[...], w_ref[0 * GROUP:1 * GROUP, :])
    acc += _dot(a1[...], w_ref[1 * GROUP:2 * GROUP, :])
    acc += _dot(a2[...], w_ref[2 * GROUP:3 * GROUP, :])
    acc += _dot(a3[...], w_ref[3 * GROUP:4 * GROUP, :])
    o_ref[0] = x_ref[0] + g_ref[0] * acc


def _out_proj(parts, w, x, mod3, gate_idx, tm=1024, tn=512):
    bsz, seq, d = x.shape
    tm = min(tm, seq)
    ns = seq // tm
    a_spec = pl.BlockSpec((tm, GROUP), lambda b, i, j: (b * ns + i, 0))
    return pl.pallas_call(
        _outproj_body,
        out_shape=jax.ShapeDtypeStruct((bsz, seq, d), F32),
        grid=(bsz, ns, d // tn),
        in_specs=[a_spec, a_spec, a_spec, a_spec,
                  pl.BlockSpec((4 * GROUP, tn), lambda b, i, j: (0, j)),
                  pl.BlockSpec((1, tm, tn), lambda b, i, j: (b, i, j)),
                  pl.BlockSpec((1, 1, tn), lambda b, i, j: (b * 6 + gate_idx, 0, j))],
        out_specs=pl.BlockSpec((1, tm, tn), lambda b, i, j: (b, i, j)),
        compiler_params=_cparams(("parallel", "parallel", "parallel")),
        name="out_proj",
    )(*parts, w, x, mod3)


def _ffn_up_body(a_ref, w1_ref, w3_ref, o_ref):
    a = a_ref[...]
    u = _dot(a, w1_ref[...].astype(BF16))
    v = _dot(a, w3_ref[...].astype(BF16))
    o_ref[...] = (u * _sigmoid(u) * v).astype(o_ref.dtype)


def _ffn_up(h, w1, w3, layer, tm=1024, tn=256):
    m, k = h.shape
    n = w1.shape[2]
    tm = min(tm, m)
    return pl.pallas_call(
        _ffn_up_body,
        out_shape=jax.ShapeDtypeStruct((m, n), BF16),
        grid=(m // tm, n // tn),
        in_specs=[pl.BlockSpec((tm, k), lambda i, j: (i, 0)),
                  pl.BlockSpec((None, k, tn), lambda i, j: (layer, 0, j)),
                  pl.BlockSpec((None, k, tn), lambda i, j: (layer, 0, j))],
        out_specs=pl.BlockSpec((tm, tn), lambda i, j: (i, j)),
        compiler_params=_cparams(("parallel", "parallel")),
        name="ffn_up",
    )(h, w1, w3)


def _ffn_down_body(a_ref, w_ref, x_ref, g_ref, o_ref):
    o_ref[0] = x_ref[0] + g_ref[0] * _dot(a_ref[...], w_ref[...])


def _ffn_down(a, w, x, mod3, gate_idx, tm=512, tn=512):
    bsz, seq, d = x.shape
    k = a.shape[1]
    tm = min(tm, seq)
    ns = seq // tm
    return pl.pallas_call(
        _ffn_down_body,
        out_shape=jax.ShapeDtypeStruct((bsz, seq, d), F32),
        grid=(bsz, ns, d // tn),
        in_specs=[pl.BlockSpec((tm, k), lambda b, i, j: (b * ns + i, 0)),
                  pl.BlockSpec((k, tn), lambda b, i, j: (0, j)),
                  pl.BlockSpec((1, tm, tn), lambda b, i, j: (b, i, j)),
                  pl.BlockSpec((1, 1, tn), lambda b, i, j: (b * 6 + gate_idx, 0, j))],
        out_specs=pl.BlockSpec((1, tm, tn), lambda b, i, j: (b, i, j)),
        compiler_params=_cparams(("parallel", "parallel", "parallel")),
        name="ffn_down",
    )(a, w, x, mod3)


def _bias_body(tab_ref, o_ref, *, off, window, shift, key_major, scale):
    h = pl.program_id(0)
    rows, cols = o_ref.shape[1], o_ref.shape[2]
    last = tab_ref[REL_BUCKETS - 1, h]
    max_exact = REL_BUCKETS // 2

    def block(d0):
        qi = lax.broadcasted_iota(jnp.int32, (LANE, LANE), 1 if key_major else 0)
        kj = lax.broadcasted_iota(jnp.int32, (LANE, LANE), 0 if key_major else 1)
        dist = d0 + qi - kj
        n = jnp.maximum(dist, 0)
        nf = jnp.maximum(n, 1).astype(F32)
        large = max_exact + (jnp.log(nf / max_exact) / math.log(REL_MAX_DIST / max_exact)
                             * (REL_BUCKETS - max_exact)).astype(jnp.int32)
        large = jnp.minimum(large, REL_BUCKETS - 1)
        bucket = jnp.where(n < max_exact, n, large)
        bias = jnp.full((LANE, LANE), last, F32)
        for b in range(REL_BUCKETS - 1):
            bias = jnp.where(bucket == b, tab_ref[b, h], bias)
        if shift:
            bias = bias - last
        if scale != 1.0:
            bias = bias * scale
        valid = dist >= 0
        if window is not None:
            valid = jnp.logical_and(valid, dist < window)
        return jnp.where(valid, bias, NEG_INF)

    cache = {}
    for br in range(rows // LANE):
        for bc in range(cols // LANE):
            qb, kb = (bc, br) if key_major else (br, bc)
            d0 = off + LANE * (qb - kb)
            if d0 not in cache:
                if d0 + LANE <= 0 or (window is not None and d0 - LANE >= window):
                    cache[d0] = jnp.full((LANE, LANE), NEG_INF, F32)
                elif d0 - LANE >= REL_MAX_DIST and (window is None or d0 + LANE <= window):
                    const = (0.0 if shift else last) * scale
                    cache[d0] = jnp.full((LANE, LANE), const, F32)
                else:
                    cache[d0] = block(d0)
            o_ref[0, br * LANE:(br + 1) * LANE, bc * LANE:(bc + 1) * LANE] = cache[d0]


def _bias_tiles(table, rows, cols, off, window, shift, key_major=False, scale=1.0):
    nh = table.shape[1]
    return pl.pallas_call(
        functools.partial(_bias_body, off=off, window=window, shift=shift, key_major=key_major, scale=scale),
        out_shape=jax.ShapeDtypeStruct((nh, rows, cols), F32),
        grid=(nh,),
        in_specs=[pl.BlockSpec(memory_space=pltpu.SMEM)],
        out_specs=pl.BlockSpec((1, rows, cols), lambda h: (h, 0, 0)),
        compiler_params=_cparams(("parallel",)),
        name="rel_bias_tiles",
    )(table)


LOG2E = 1.4426950408889634
VT_ROWS = LANE + 16
Q_CHUNK = 512


def _prep_a_body(c_ref, qg_ref, kg_ref, qk_ref, vt_ref):
    ones = _seg_ones()
    scale = DIFF_QK ** -0.5 * LOG2E
    tm = c_ref.shape[0]
    for j in range(2 * DIFF_HEADS):
        x = c_ref[:, j * LANE:(j + 1) * LANE].astype(F32)
        ms = _segsum(x * x, ones) * (1.0 / DIFF_QK)
        gain = qg_ref[...] * scale if j < DIFF_HEADS else kg_ref[...]
        qk_ref[:, j * LANE:(j + 1) * LANE] = (x * lax.rsqrt(ms + NORM_EPS) * gain).astype(qk_ref.dtype)
    for h in range(DIFF_HEADS):
        v = c_ref[:, 2 * GROUP + h * LANE:2 * GROUP + (h + 1) * LANE].astype(F32)
        vt_ref[h * VT_ROWS:h * VT_ROWS + LANE, :] = v.T.astype(vt_ref.dtype)
        vt_ref[h * VT_ROWS + LANE:(h + 1) * VT_ROWS, :] = jnp.ones((VT_ROWS - LANE, tm), vt_ref.dtype)


def _prep_a(cols, q_gain, k_gain, tm=256):
    t = cols.shape[0]
    tm = min(tm, t)
    qg = jnp.tile(q_gain, 2).reshape(1, LANE)
    kg = jnp.tile(k_gain, 2).reshape(1, LANE)
    vec = pl.BlockSpec((1, LANE), lambda i: (0, 0))
    return pl.pallas_call(
        _prep_a_body,
        out_shape=[jax.ShapeDtypeStruct((t, 2 * GROUP), BF16),
                   jax.ShapeDtypeStruct((DIFF_HEADS * VT_ROWS, t), BF16)],
        grid=(t // tm,),
        in_specs=[pl.BlockSpec((tm, 3 * GROUP), lambda i: (i, OFF_A // (3 * GROUP))), vec, vec],
        out_specs=[pl.BlockSpec((tm, 2 * GROUP), lambda i: (i, 0)),
                   pl.BlockSpec((DIFF_HEADS * VT_ROWS, tm), lambda i: (0, i))],
        compiler_params=_cparams(("parallel",)),
        name="diff_prep",
    )(cols, qg, kg)


def _flash_body(it_ref, jt_ref, q_ref, k_ref, vt_ref, vtp_ref, bd_ref, bs_ref, lam_ref, sub_ref, o_ref,
                qlo, qhi, m1, a1, m2, a2, p_scr, al_scr, *, lam_init):
    t = pl.program_id(2)
    i = it_ref[t]
    j = jt_ref[t]

    @pl.when(j == 0)
    def _():
        q = q_ref[...]
        lane = lax.broadcasted_iota(jnp.int32, q.shape, 1)
        zero = jnp.zeros_like(q)
        qlo[...] = jnp.where(lane < HALF, q, zero)
        qhi[...] = jnp.where(lane >= HALF, q, zero)
        for m, a in ((m1, a1), (m2, a2)):
            m[...] = jnp.full(m.shape, NEG_INF, F32)
            a[...] = jnp.zeros(a.shape, F32)
        p_scr[...] = jnp.zeros(p_scr.shape, BF16)
        al_scr[...] = jnp.ones(al_scr.shape, F32)

    def step(bias, last):
        k = k_ref[...]
        vt = vt_ref[...]
        vtp = vtp_ref[...]
        nq = q_ref.shape[0]
        qc = min(Q_CHUNK, nq)
        chains = [(mi, qq, m, a, slice(c * qc, (c + 1) * qc))
                  for mi, (qq, m, a) in enumerate(((qlo, m1, a1), (qhi, m2, a2))) for c in range(nq // qc)]
        ahead = 2
        if bias is None:
            scores = [_dot_nt(k, qq[qs, :]) for _, qq, _, _, qs in chains[:ahead]]
        else:
            full = [_dot_nt(k, qq[...]) for qq in (qlo, qhi)]
            scores = [full[c[0]][:, c[4]] + bias[:, c[4]] for c in chains]
        for idx, (mi, _, m, a, qs) in enumerate(chains):
            if bias is None and idx + ahead < len(chains):
                _, qq_n, _, _, qs_n = chains[idx + ahead]
                scores.append(_dot_nt(k, qq_n[qs_n, :]))
            acc = al_scr[mi, :, qs] * a[:, qs] + _dot(vtp, p_scr[mi, :, qs])
            s = scores[idx]
            m_old = m[:, qs]
            m_new = jnp.maximum(m_old, jnp.max(s, axis=0, keepdims=True))
            p = jnp.exp2(s - m_new).astype(BF16)
            alpha = jnp.exp2(m_old - m_new)
            if last:
                acc = alpha * acc + _dot(vt, p)
            else:
                p_scr[mi, :, qs] = p
                al_scr[mi, :, qs] = alpha
            a[:, qs] = acc
            m[:, qs] = m_new

    @pl.when(j < i - 1)
    def _():
        step(None, False)

    @pl.when(j == i - 1)
    def _():
        step(bs_ref[0], False)

    @pl.when(j == i)
    def _():
        step(bd_ref[0], True)
        lam = lam_ref[...]
        e1 = jnp.exp(jnp.sum(lam[0:1] * lam[1:2], axis=-1, keepdims=True))
        e2 = jnp.exp(jnp.sum(lam[2:3] * lam[3:4], axis=-1, keepdims=True))
        lam_full = e1 - e2 + lam_init
        o = a1[:LANE, :] / a1[LANE:LANE + 1, :] - lam_full * (a2[:LANE, :] / a2[LANE:LANE + 1, :])
        ms = jnp.mean(o * o, axis=0, keepdims=True)
        o = o * lax.rsqrt(ms + NORM_EPS) * sub_ref[...] * (1.0 - lam_init)
        o_ref[...] = o.T.astype(o_ref.dtype)


def _diff_attention(qk, vt, bias_diag, bias_sub, lam, subln, bsz, seq, layer_idx, tile):
    nq = seq // tile
    lam_init = 0.8 - 0.6 * math.exp(-0.3 * layer_idx)
    pairs = [(i, j) for i in range(nq) for j in range(i + 1)]
    i_tab = jnp.array([ij[0] for ij in pairs], jnp.int32)
    j_tab = jnp.array([ij[1] for ij in pairs], jnp.int32)
    grid_spec = pltpu.PrefetchScalarGridSpec(
        num_scalar_prefetch=2,
        grid=(bsz, DIFF_HEADS, len(pairs)),
        in_specs=[pl.BlockSpec((tile, LANE), lambda b, h, t, it, jt: (b * nq + it[t], h)),
                  pl.BlockSpec((tile, LANE), lambda b, h, t, it, jt: (b * nq + jt[t], DIFF_HEADS + h)),
                  pl.BlockSpec((VT_ROWS, tile), lambda b, h, t, it, jt: (h, b * nq + jt[t])),
                  pl.BlockSpec((VT_ROWS, tile), lambda b, h, t, it, jt: (h, b * nq + jnp.maximum(jt[t] - 1, 0))),
                  pl.BlockSpec((1, tile, tile), lambda b, h, t, it, jt: (h, 0, 0)),
                  pl.BlockSpec((1, tile, tile), lambda b, h, t, it, jt: (h, 0, 0)),
                  pl.BlockSpec((4, DIFF_QK), lambda b, h, t, it, jt: (0, 0)),
                  pl.BlockSpec((LANE, 1), lambda b, h, t, it, jt: (0, 0))],
        out_specs=pl.BlockSpec((tile, LANE), lambda b, h, t, it, jt: (b * nq + it[t], h)),
        scratch_shapes=[pltpu.VMEM((tile, LANE), BF16), pltpu.VMEM((tile, LANE), BF16),
                        pltpu.VMEM((1, tile), F32), pltpu.VMEM((VT_ROWS, tile), F32),
                        pltpu.VMEM((1, tile), F32), pltpu.VMEM((VT_ROWS, tile), F32),
                        pltpu.VMEM((2, tile, tile), BF16), pltpu.VMEM((2, 1, tile), F32)])
    return pl.pallas_call(
        functools.partial(_flash_body, lam_init=lam_init),
        out_shape=jax.ShapeDtypeStruct((bsz * seq, GROUP), BF16),
        grid_spec=grid_spec,
        compiler_params=_cparams(("parallel", "parallel", "arbitrary")),
        name="diff_attention",
    )(i_tab, j_tab, qk, qk, vt, vt, bias_diag, bias_sub, lam, subln.reshape(LANE, 1))


def _swa_body(q_ref, kp_ref, kc_ref, vp_ref, vc_ref, bias_ref, qg_ref, kg_ref, sink_ref, o_ref):
    n = pl.program_id(1)
    g = pl.program_id(2)
    ones = _seg_ones()
    lane = lax.broadcasted_iota(jnp.int32, (SWA_BLOCK, LANE), 1)
    lo = lane < HALF

    def norm(x, gain):
        ms = _segsum(x * x, ones) * (1.0 / HALF)
        return x * lax.rsqrt(ms + NORM_EPS) * gain

    k = jnp.concatenate([norm(kp_ref[...], kg_ref[...]), norm(kc_ref[...], kg_ref[...])], axis=0)
    v = jnp.concatenate([vp_ref[...], vc_ref[...]], axis=0)
    k_sw = pltpu.roll(k, HALF, axis=1)
    v_sw = pltpu.roll(v, HALF, axis=1)
    first_head = g == 0
    lo2 = lax.broadcasted_iota(jnp.int32, (2 * SWA_BLOCK, LANE), 1) < HALF
    k_a = jnp.where(first_head, k, k_sw).astype(BF16)
    k_b = jnp.where(first_head, k_sw, k).astype(BF16)
    v_mine_lo = jnp.where(first_head, v, v_sw)
    v_mine_hi = jnp.where(first_head, v_sw, v)
    v_a = jnp.where(lo2, v_mine_lo, 0.0).astype(BF16)
    v_b = jnp.where(lo2, 0.0, v_mine_hi).astype(BF16)
    kcol = lax.broadcasted_iota(jnp.int32, (SWA_BLOCK, 2 * SWA_BLOCK), 1)
    pad = jnp.logical_and(n == 0, kcol < SWA_BLOCK)
    scale = HALF ** -0.5
    grp = SWA_HEADS // SWA_KV_HEADS
    heads = range(grp)
    q = [norm(q_ref[:, pr * LANE:(pr + 1) * LANE].astype(F32), qg_ref[...] * scale) for pr in range(grp // 2)]
    qm = [jnp.where(lo if hh % 2 == 0 else jnp.logical_not(lo), q[hh // 2], 0.0).astype(BF16) for hh in heads]
    s = [_dot_nt(qm[hh], k_a if hh % 2 == 0 else k_b) for hh in heads]
    s = [jnp.where(pad, NEG_INF, s[hh] + bias_ref[hh]) for hh in heads]
    sink = [sink_ref[g * grp + hh] for hh in heads]
    m = [jnp.maximum(jnp.max(s[hh], axis=-1, keepdims=True), sink[hh]) for hh in heads]
    p = [jnp.exp(s[hh] - m[hh]) for hh in heads]
    den = [jnp.sum(p[hh], axis=-1, keepdims=True) + jnp.exp(sink[hh] - m[hh]) for hh in heads]
    pv = [_dot(p[hh].astype(BF16), v_a if hh % 2 == 0 else v_b) for hh in heads]
    for pr in range(grp // 2):
        o = pv[2 * pr] / den[2 * pr] + pv[2 * pr + 1] / den[2 * pr + 1]
        o_ref[:, pr * LANE:(pr + 1) * LANE] = o.astype(o_ref.dtype)


def _swa_attention(cols, tail, bias, q_gain, k_gain, sinks, bsz, seq):
    nb = seq // SWA_BLOCK
    grp = SWA_HEADS // SWA_KV_HEADS
    qw = grp * HALF
    kcol = (OFF_D + GROUP - OFF_T) // LANE
    vcol = kcol + 1
    prev = lambda b, n, g: b * nb + jnp.maximum(n - 1, 0)
    cur = lambda b, n, g: b * nb + n
    vec = pl.BlockSpec((1, LANE), lambda b, n, g: (0, 0))
    return pl.pallas_call(
        _swa_body,
        out_shape=jax.ShapeDtypeStruct((bsz * seq, GROUP), BF16),
        grid=(bsz, nb, SWA_KV_HEADS),
        in_specs=[pl.BlockSpec((SWA_BLOCK, qw), lambda b, n, g: (cur(b, n, g), OFF_D // qw + g)),
                  pl.BlockSpec((SWA_BLOCK, LANE), lambda b, n, g: (prev(b, n, g), kcol)),
                  pl.BlockSpec((SWA_BLOCK, LANE), lambda b, n, g: (cur(b, n, g), kcol)),
                  pl.BlockSpec((SWA_BLOCK, LANE), lambda b, n, g: (prev(b, n, g), vcol)),
                  pl.BlockSpec((SWA_BLOCK, LANE), lambda b, n, g: (cur(b, n, g), vcol)),
                  pl.BlockSpec((grp, SWA_BLOCK, 2 * SWA_BLOCK), lambda b, n, g: (g, 0, 0)),
                  vec, vec,
                  pl.BlockSpec(memory_space=pltpu.SMEM)],
        out_specs=pl.BlockSpec((SWA_BLOCK, qw), lambda b, n, g: (cur(b, n, g), g)),
        compiler_params=_cparams(("parallel", "parallel", "parallel")),
        name="swa_attention",
    )(cols, tail, tail, tail, tail, bias,
      jnp.tile(q_gain, 2).reshape(1, LANE), jnp.tile(k_gain, 2).reshape(1, LANE), sinks)


def _prep_c_body(*refs, seq, with_vres):
    (cm_ref, cs_ref, pm_ref, ps_ref, mum_ref, mus_ref, wup_ref, aup_ref, gup_ref, vup_ref, glaup_ref,
     w0_ref, a0_ref, v0_ref, kk_ref, ka_ref, glab_ref) = refs[:17]
    rest = refs[17:]
    if with_vres:
        vf_ref, rest = rest[0], rest[1:]
    r_ref, lw_ref, k_ref, v_ref, nk_ref, kb_ref, g_ref, la_ref = rest
    tm = cm_ref.shape[0]
    i = pl.program_id(0)
    seq_start = (i * tm) % seq == 0

    def shifted(cur, prev_rows, mu):
        row = lax.broadcasted_iota(jnp.int32, cur.shape, 0)
        before = jnp.where(seq_start, 0.0, prev_rows[PREV_ROWS - 1:PREV_ROWS, :].astype(F32))
        prev = jnp.where(row == 0, before, pltpu.roll(cur, 1, axis=0))
        return cur + (prev - cur) * mu

    def low_rank(x, w2_ref):
        hi, lo = _split(x)
        return _dot(hi, w2_ref[0]) + _dot(lo, w2_ref[0]) + _dot(hi, w2_ref[1])

    cs = cs_ref[...]
    la_ref[...] = -_softplus(-(low_rank(cs, glaup_ref) + glab_ref[...])) * (1.0 / GLA_NORMALIZER)
    sm = shifted(cm_ref[...].astype(F32), pm_ref, mum_ref[...])
    ss = shifted(cs, ps_ref, mus_ref[...])
    r = sm[:, :GROUP]
    k = sm[:, GROUP:2 * GROUP]
    v = sm[:, 2 * GROUP:]
    if with_vres:
        gate = _sigmoid(v0_ref[...] + low_rank(ss, vup_ref))
        v = v + (vf_ref[...] - v) * gate
    w_log = -_softplus(-(w0_ref[...] + low_rank(jnp.tanh(ss), wup_ref))) - 0.5
    a = _sigmoid(a0_ref[...] + low_rank(ss, aup_ref))
    r_ref[...] = r.astype(r_ref.dtype)
    lw_ref[...] = -jnp.exp(w_log)
    v_ref[...] = v
    g_ref[...] = low_rank(_sigmoid(ss), gup_ref).astype(g_ref.dtype)
    k_ref[...] = (k * (1.0 + (a - 1.0) * ka_ref[...])).astype(k_ref.dtype)
    ones = _seg_ones()
    kk = k * kk_ref[...]
    for j in range(GROUP // LANE):
        sl = slice(j * LANE, (j + 1) * LANE)
        x = kk[:, sl]
        nrm = jnp.maximum(jnp.sqrt(_segsum(x * x, ones)), 1e-12)
        x = x / nrm
        nk_ref[:, sl] = x.astype(nk_ref.dtype)
        kb_ref[:, sl] = (x * a[:, sl]).astype(kb_ref.dtype)


PREV_ROWS = 16


def _prep_c(cols, tail, seq, mu_main, mu_small, w_up, a_up, g_up, v_up, gla_up, w0, a0, v0, k_k, k_a, gla_bias,
            v_first, tm=256):
    t = cols.shape[0]
    tm = min(tm, seq)
    with_vres = v_first is not None
    cmain = OFF_C // (3 * GROUP)
    csmall = (OFF_S - OFF_T) // SMALL
    prev = lambda i: jnp.maximum(i * (tm // PREV_ROWS) - 1, 0)
    full = lambda r, c: pl.BlockSpec((r, c), lambda i: (0, 0))
    hilo = lambda c: pl.BlockSpec((2, SMALL, c), lambda i: (0, 0, 0))
    in_specs = [pl.BlockSpec((tm, 3 * GROUP), lambda i: (i, cmain)),
                pl.BlockSpec((tm, SMALL), lambda i: (i, csmall)),
                pl.BlockSpec((PREV_ROWS, 3 * GROUP), lambda i: (prev(i), cmain)),
                pl.BlockSpec((PREV_ROWS, SMALL), lambda i: (prev(i), csmall)),
                full(1, 3 * GROUP), full(1, SMALL),
                hilo(GROUP), hilo(GROUP), hilo(GROUP), hilo(GROUP), hilo(GLA_HEADS * GLA_DK),
                full(1, GROUP), full(1, GROUP), full(1, GROUP), full(1, GROUP), full(1, GROUP),
                full(1, GLA_HEADS * GLA_DK)]
    args = [cols, tail, cols, tail, mu_main, mu_small, w_up, a_up, g_up, v_up, gla_up,
            w0, a0, v0, k_k, k_a, gla_bias]
    if with_vres:
        in_specs.append(pl.BlockSpec((tm, GROUP), lambda i: (i, 0)))
        args.append(v_first)
    row = pl.BlockSpec((tm, GROUP), lambda i: (i, 0))
    out = [jax.ShapeDtypeStruct((t, GROUP), dt) for dt in (BF16, F32, BF16, F32, BF16, BF16, BF16)]
    return pl.pallas_call(
        functools.partial(_prep_c_body, seq=seq, with_vres=with_vres),
        out_shape=out + [jax.ShapeDtypeStruct((t, GLA_HEADS * GLA_DK), F32)],
        grid=(t // tm,),
        in_specs=in_specs,
        out_specs=[row] * 7 + [pl.BlockSpec((tm, GLA_HEADS * GLA_DK), lambda i: (i, 0))],
        compiler_params=_cparams(("parallel",)),
        name="rwkv_gla_prep",
    )(*args)


def _gla_body(q_ref, k_ref, v_ref, g_ref, la_ref, gain_ref, o_ref, st_ref):
    @pl.when(pl.program_id(2) == 0)
    def _():
        st_ref[...] = jnp.zeros(st_ref.shape, F32)

    row = lax.broadcasted_iota(jnp.int32, (CHUNK, CHUNK), 0)
    col = lax.broadcasted_iota(jnp.int32, (CHUNK, CHUNK), 1)
    causal = row >= col
    tri = causal.astype(BF16)
    nchunk = q_ref.shape[0] // CHUNK

    def local(sls):
        n = range(len(sls))
        k = [k_ref[sl, :].astype(F32) for sl in sls]
        v = [v_ref[sl, :].astype(BF16) for sl in sls]
        b = [_tri_cumsum(tri, la_ref[sl, :]) for sl in sls]
        b_last = [b[i][CHUNK - 1:CHUNK, :] for i in n]
        q_dec = [(q_ref[sl, :].astype(F32) * (GLA_DK ** -0.5) * jnp.exp(b[i])).astype(BF16)
                 for i, sl in enumerate(sls)]
        a_intra = [jnp.where(causal, _dot_nt(q_dec[i], (k[i] * jnp.exp(-b[i])).astype(BF16)), 0.0) for i in n]
        upd = [_dot_tn(v[i], (k[i] * jnp.exp(b_last[i] - b[i])).astype(BF16)) for i in n]
        o_intra = [_dot(a_intra[i].astype(BF16), v[i]) for i in n]
        return [(q_dec[i], o_intra[i], upd[i], jnp.exp(b_last[i])) for i in n]

    def advance(sl, q_dec, o_intra, upd, dec):
        state = st_ref[...]
        s_hi, s_lo = _split(state)
        o = o_intra + _dot_nt(q_dec, s_hi) + _dot_nt(q_dec, s_lo)
        st_ref[...] = state * dec + upd
        ms = jnp.mean(o * o, axis=-1, keepdims=True)
        o = o * lax.rsqrt(ms + NORM_EPS) * gain_ref[...]
        gate = g_ref[sl, :].astype(F32)
        o_ref[sl, :] = (o * (gate * _sigmoid(gate))).astype(o_ref.dtype)

    def group(gi, carry):
        sls = [pl.ds(pl.multiple_of((gi * SCAN_GROUP + g) * CHUNK, CHUNK), CHUNK) for g in range(SCAN_GROUP)]
        for sl, part in zip(sls, local(sls)):
            advance(sl, *part)
        return carry

    lax.fori_loop(0, nchunk // SCAN_GROUP, group, 0)


def _gla(cols, log_a, out_gain, bsz, seq, tc=512):
    tc = min(tc, seq)
    ns = seq // tc
    qc = OFF_B // GLA_DK
    kc = qc + GLA_HEADS
    vc = (OFF_B + 2 * GLA_HEADS * GLA_DK) // GLA_DV
    gc = vc + GLA_HEADS
    rows = lambda b, h, i: b * ns + i
    return pl.pallas_call(
        _gla_body,
        out_shape=jax.ShapeDtypeStruct((bsz * seq, GROUP), BF16),
        grid=(bsz, GLA_HEADS, ns),
        in_specs=[pl.BlockSpec((tc, GLA_DK), lambda b, h, i: (rows(b, h, i), qc + h)),
                  pl.BlockSpec((tc, GLA_DK), lambda b, h, i: (rows(b, h, i), kc + h)),
                  pl.BlockSpec((tc, GLA_DV), lambda b, h, i: (rows(b, h, i), vc + h)),
                  pl.BlockSpec((tc, GLA_DV), lambda b, h, i: (rows(b, h, i), gc + h)),
                  pl.BlockSpec((tc, GLA_DK), lambda b, h, i: (rows(b, h, i), h)),
                  pl.BlockSpec((1, GLA_DV), lambda b, h, i: (0, 0))],
        out_specs=pl.BlockSpec((tc, GLA_DV), lambda b, h, i: (rows(b, h, i), h)),
        scratch_shapes=[pltpu.VMEM((GLA_DV, GLA_DK), F32)],
        compiler_params=_cparams(("parallel", "parallel", "arbitrary")),
        name="gla_scan",
    )(cols, cols, cols, cols, log_a, out_gain.reshape(1, GLA_DV))


def _rwkv_body(r_ref, lw_ref, k_ref, v_ref, nk_ref, kb_ref, g_ref, rk_ref, lnw_ref, lnb_ref, o_ref, st_ref):
    @pl.when(pl.program_id(2) == 0)
    def _():
        st_ref[...] = jnp.zeros(st_ref.shape, F32)

    two = 2 * CHUNK
    row = lax.broadcasted_iota(jnp.int32, (two, two), 0)
    col = lax.broadcasted_iota(jnp.int32, (two, two), 1)
    same = (row // CHUNK) == (col // CHUNK)
    strict = jnp.logical_and(same, (row % CHUNK) > (col % CHUNK))
    incl = jnp.logical_and(same, (row % CHUNK) >= (col % CHUNK))
    eye = (row == col).astype(F32)
    crow = lax.broadcasted_iota(jnp.int32, (CHUNK, CHUNK), 0)
    ccol = lax.broadcasted_iota(jnp.int32, (CHUNK, CHUNK), 1)
    tri = (crow >= ccol).astype(BF16)
    lo = lax.broadcasted_iota(jnp.int32, (CHUNK, LANE), 1) < HALF
    ones = _seg_ones()
    nchunk = r_ref.shape[0] // CHUNK

    def stack(x):
        return jnp.concatenate([jnp.where(lo, x, 0.0), jnp.where(lo, 0.0, x)], axis=0)

    def fold(x):
        return x[:CHUNK, :] + x[CHUNK:, :]

    def transitions(items):
        n = range(len(items))
        r = [r_ref[sl, ls].astype(F32) for sl, ls in items]
        lw = [lw_ref[sl, ls] for sl, ls in items]
        k = [k_ref[sl, ls].astype(F32) for sl, ls in items]
        v = [v_ref[sl, ls] for sl, ls in items]
        cum = [_tri_cumsum(tri, lw[i]) for i in n]
        last = [cum[i][CHUNK - 1:CHUNK, :] for i in n]
        a_bf, r_st, bk_st, bkh_st, v_st = [], [], [], [], []
        for i, (sl, ls) in enumerate(items):
            kb = kb_ref[sl, ls].astype(F32)
            e_neg = jnp.exp(-cum[i])
            e_rem = jnp.exp(last[i] - cum[i])
            a_bf.append(stack(-nk_ref[sl, ls].astype(F32) * jnp.exp(cum[i] - lw[i])).astype(BF16))
            r_st.append(stack(r[i] * jnp.exp(cum[i])))
            bk_st.append(jnp.concatenate([stack(kb * e_neg), stack(k[i] * e_neg)], axis=0).astype(BF16))
            bkh_st.append(jnp.concatenate([stack(kb * e_rem), stack(k[i] * e_rem)], axis=0).astype(BF16))
            v_st.append(stack(v[i]).astype(BF16))
        sc = [_dot_nt(jnp.concatenate([a_bf[i], r_st[i].astype(BF16)], axis=0), bk_st[i]) for i in n]
        a_ab = [jnp.where(strict, sc[i][:two, :two], 0.0) for i in n]
        a_ak = [jnp.where(strict, sc[i][:two, two:], 0.0).astype(BF16) for i in n]
        r_b = [jnp.where(incl, sc[i][two:, :two], 0.0).astype(BF16) for i in n]
        r_k = [jnp.where(incl, sc[i][two:, two:], 0.0).astype(BF16) for i in n]
        akv = [_dot(a_ak[i], v_st[i]).astype(BF16) for i in n]
        rkv = [_dot(r_k[i], v_st[i]) for i in n]
        inv = [eye + a_ab[i] for i in n]
        pw = [a_ab[i].astype(BF16) for i in n]
        for _ in range(5):
            pw = [_dot(pw[i], pw[i]).astype(BF16) for i in n]
            inv = [inv[i] + _dot(inv[i].astype(BF16), pw[i]) for i in n]
        pu = [_dot(inv[i].astype(BF16), jnp.concatenate([a_bf[i], akv[i]], axis=1)).astype(BF16) for i in n]
        rb_pu = [_dot(r_b[i], pu[i]) for i in n]
        m = [_dot_tn(pu[i][:, :LANE], bkh_st[i][:two, :]).astype(BF16) for i in n]
        n0 = [_dot_tn(jnp.concatenate([pu[i][:, LANE:], v_st[i]], axis=0), bkh_st[i]) for i in n]
        rkr = [_segsum(r[i] * k[i] * rk_ref[:, items[i][1]], ones) for i in n]
        return [(fold(r_st[i] + rb_pu[i][:, :LANE]).astype(BF16), fold(rb_pu[i][:, LANE:] + rkv[i]),
                 m[i], n0[i], jnp.exp(last[i]), rkr[i] * v[i]) for i in n]

    def advance(tile, sl, ls, p2, y0, m, n0, dec, bonus):
        state = st_ref[tile]
        s_hi, s_lo = _split(state)
        y = _dot_nt(p2, s_hi) + y0
        st_ref[tile] = state * dec + _dot(s_hi, m) + _dot(s_lo, m) + n0
        d = y - _segsum(y, ones) * (1.0 / HALF)
        var = _segsum(d * d, ones) * (1.0 / HALF)
        y = d * lax.rsqrt(var + RWKV_LN_EPS) * lnw_ref[:, ls] + lnb_ref[:, ls]
        o_ref[sl, ls] = ((y + bonus) * g_ref[sl, ls]).astype(o_ref.dtype)

    gsz = min(RWKV_GROUP, nchunk)

    def group(gi, carry):
        sls = [pl.ds(pl.multiple_of((gi * gsz + g) * CHUNK, CHUNK), CHUNK) for g in range(gsz)]
        tiles = range(r_ref.shape[1] // LANE)
        items = [(sl, slice(q * LANE, (q + 1) * LANE)) for sl in sls for q in tiles]
        for idx, ((sl, ls), part) in enumerate(zip(items, transitions(items))):
            advance(idx % len(tiles), sl, ls, *part)
        return carry

    lax.fori_loop(0, nchunk // gsz, group, 0)


def _rwkv_scan(r, lw, k, v, nk, kb, g, r_k, ln_w, ln_b, bsz, seq, tc=512):
    tc = min(tc, seq)
    ns = seq // tc
    width = RWKV_TILES * LANE
    npair = GROUP // width
    blk = pl.BlockSpec((tc, width), lambda b, p, i: (b * ns + i, p))
    vec = pl.BlockSpec((1, width), lambda b, p, i: (0, p))
    return pl.pallas_call(
        _rwkv_body,
        out_shape=jax.ShapeDtypeStruct((bsz * seq, GROUP), BF16),
        grid=(bsz, npair, ns),
        in_specs=[blk] * 7 + [vec] * 3,
        out_specs=blk,
        scratch_shapes=[pltpu.VMEM((RWKV_TILES, LANE, LANE), F32)],
        compiler_params=_cparams(("parallel", "parallel", "arbitrary")),
        name="rwkv_scan",
    )(r, lw, k, v, nk, kb, g, r_k.reshape(1, GROUP), ln_w.reshape(1, GROUP), ln_b.reshape(1, GROUP))


def _pad_rows(w, start):
    full = jnp.zeros((SMALL, w.shape[1]), F32).at[start:start + w.shape[0]].set(w)
    hi = full.astype(BF16)
    return jnp.stack([hi, (full - hi.astype(F32)).astype(BF16)])


def _layer(x, c8, layer_idx, v_first, vres, bias_ad, bias_as, bias_d, p, attn_tile):
    bsz, seq, d = x.shape
    t = bsz * seq
    mod = _ada(c8, p["ada_w"], p["ada_b"], layer_idx)
    mod3 = mod[:bsz].reshape(bsz * 6, 1, d)
    h = _norm_mod(x, mod3, 1, 0)

    vres_cols = vres[0] if vres is not None else jnp.zeros((d, 32), F32)
    w_r = _regroup_w_in(p["w_in"], layer_idx, vres_cols)
    cols, tail = _in_proj(h, w_r)

    qk, vt = _prep_a(cols, p["diff_q_norm"], p["diff_k_norm"])
    o_a = _diff_attention(qk, vt, bias_ad, bias_as, p["diff_lambda"], p["diff_subln"], bsz, seq, layer_idx,
                          attn_tile)
    o_dd = _swa_attention(cols, tail, bias_d, p["swa_q_norm"], p["swa_k_norm"], p["swa_sinks"], bsz, seq)
    mu = p["rwkv_mu"]
    vres_mu = vres[1] if vres is not None else jnp.zeros((32,), F32)
    mu_small = jnp.concatenate([jnp.zeros((S_WD,), F32), mu[3 * GROUP:], vres_mu,
                                jnp.zeros((SMALL - S_VR - 32,), F32)]).reshape(1, SMALL)
    v_up = _pad_rows(vres[2], S_VR) if vres is not None else jnp.zeros((2, SMALL, GROUP), BF16)
    v0 = vres[3] if vres is not None else jnp.zeros((GROUP,), F32)
    r_, lw_, k_, v_, nk_, kb_, g_, la_ = _prep_c(
        cols, tail, seq, mu[:3 * GROUP].reshape(1, -1), mu_small,
        _pad_rows(p["rwkv_w_up"], S_WD), _pad_rows(p["rwkv_a_up"], S_AD), _pad_rows(p["rwkv_g_up"], S_GD),
        v_up, _pad_rows(p["gla_gate_up"], S_GLA),
        p["rwkv_w0"].reshape(1, -1), p["rwkv_a0"].reshape(1, -1), v0.reshape(1, -1),
        p["rwkv_k_k"].reshape(1, -1), p["rwkv_k_a"].reshape(1, -1), p["gla_gate_bias"].reshape(1, -1),
        v_first if vres is not None else None)
    if vres is None:
        v_first = v_
    o_bb = _gla(cols, la_, p["gla_out_norm"], bsz, seq)
    o_c = _rwkv_scan(r_, lw_, k_, v_, nk_, kb_, g_, p["rwkv_r_k"].reshape(-1), p["rwkv_ln_w"], p["rwkv_ln_b"],
                     bsz, seq)

    x = _out_proj([o_a, o_bb, o_c, o_dd], p["w_out"], x, mod3, 2)
    h2 = _norm_mod(x, mod3, 4, 3)
    act = _ffn_up(h2, p["ffn_w1"], p["ffn_w3"], layer_idx)
    x = _ffn_down(act, p["ffn_w2"], x, mod3, 5)
    return x, v_first


def kernel(x, c, rel_bias, ada_w, ada_b, w_in, w_out, diff_q_norm, diff_k_norm, diff_lambda, diff_subln,
           gla_gate_up, gla_gate_bias, gla_out_norm, rwkv_mu, rwkv_w_up, rwkv_w0, rwkv_a_up, rwkv_a0,
           rwkv_g_up, rwkv_k_k, rwkv_k_a, rwkv_r_k, rwkv_ln_w, rwkv_ln_b, rwkv_vres_down, rwkv_vres_mu,
           rwkv_vres_up, rwkv_v0, swa_q_norm, swa_k_norm, swa_sinks, ffn_w1, ffn_w3, ffn_w2):
    bsz, seq, _ = x.shape
    depth = ada_w.shape[0]
    attn_tile = min(1024, seq)
    bias_ad = _bias_tiles(rel_bias[:, :DIFF_HEADS], attn_tile, attn_tile, 0, None, True, True, LOG2E)
    bias_as = _bias_tiles(rel_bias[:, :DIFF_HEADS], attn_tile, attn_tile, attn_tile, None, True, True, LOG2E)
    bias_d = _bias_tiles(rel_bias[:, DIFF_HEADS:], SWA_BLOCK, 2 * SWA_BLOCK, SWA_BLOCK, SWA_BLOCK, False)
    c8 = jnp.zeros((8, c.shape[1]), F32).at[:bsz].set(c)
    v_first = None
    for l in range(depth):
        p = dict(ada_w=ada_w, ada_b=ada_b, w_in=w_in, w_out=_cast_bf16(w_out, l),
                 diff_q_norm=diff_q_norm[l], diff_k_norm=diff_k_norm[l], diff_lambda=diff_lambda[l],
                 diff_subln=diff_subln[l], gla_gate_up=gla_gate_up[l], gla_gate_bias=gla_gate_bias[l],
                 gla_out_norm=gla_out_norm[l], rwkv_mu=rwkv_mu[l], rwkv_w_up=rwkv_w_up[l],
                 rwkv_w0=rwkv_w0[l], rwkv_a_up=rwkv_a_up[l], rwkv_a0=rwkv_a0[l], rwkv_g_up=rwkv_g_up[l],
                 rwkv_k_k=rwkv_k_k[l], rwkv_k_a=rwkv_k_a[l], rwkv_r_k=rwkv_r_k[l], rwkv_ln_w=rwkv_ln_w[l],
                 rwkv_ln_b=rwkv_ln_b[l], swa_q_norm=swa_q_norm[l], swa_k_norm=swa_k_norm[l],
                 swa_sinks=swa_sinks[l], ffn_w1=ffn_w1, ffn_w3=ffn_w3, ffn_w2=_cast_bf16(ffn_w2, l))
        vres = None if l == 0 else (rwkv_vres_down[l - 1], rwkv_vres_mu[l - 1], rwkv_vres_up[l - 1],
                                    rwkv_v0[l - 1])
        x, v_first = _layer(x, c8, l, v_first, vres, bias_ad, bias_as, bias_d, p, attn_tile)
    return x
```

```python
import functools
import math

import jax
import jax.numpy as jnp
from jax import lax
from jax.experimental import pallas as pl
from jax.experimental.pallas import tpu as pltpu

F32 = jnp.float32
BF16 = jnp.bfloat16
HI = lax.Precision.HIGHEST

D_MODEL = 4096
GROUP = 1024
D_FF = 11008
NORM_EPS = 1e-6
NEG_INF = -1e30
LANE = 128
HALF = 64

DIFF_HEADS = 8
DIFF_QK = 64
GLA_HEADS = 4
GLA_DK = 128
GLA_DV = 256
GLA_RANK = 16
GLA_NORMALIZER = 16.0
CHUNK = 64
SCAN_GROUP = 8
RWKV_GROUP = 8
RWKV_TILES = 2
RWKV_LN_EPS = 64e-5
SWA_HEADS = 16
SWA_KV_HEADS = 2
SWA_BLOCK = 128
REL_BUCKETS = 32
REL_MAX_DIST = 128

OFF_A = 0
OFF_B = 3072
OFF_C = 6144
OFF_D = 9216
OFF_S = 10496
SMALL = 256
NCOLS = OFF_S + SMALL
S_GLA, S_WD, S_AD, S_GD, S_VR = 0, 16, 80, 144, 208
TAIL = 512
OFF_T = NCOLS - TAIL

VMEM_LIMIT = 56 * 1024 * 1024


def _cparams(sem, vmem=VMEM_LIMIT):
    return pltpu.CompilerParams(dimension_semantics=sem, vmem_limit_bytes=vmem)


def _dot(a, b, prec=None):
    return jnp.dot(a, b, preferred_element_type=F32, precision=prec)


def _dot_nt(a, b, prec=None):
    return lax.dot_general(a, b, (((1,), (1,)), ((), ())), preferred_element_type=F32, precision=prec)


def _dot_tn(a, b, prec=None):
    return lax.dot_general(a, b, (((0,), (0,)), ((), ())), preferred_element_type=F32, precision=prec)


def _sigmoid(z):
    return 1.0 / (1.0 + jnp.exp(-z))


def _softplus(z):
    return jnp.maximum(z, 0.0) + jnp.log(1.0 + jnp.exp(-jnp.abs(z)))


def _seg_ones():
    r = lax.broadcasted_iota(jnp.int32, (LANE, LANE), 0) // HALF
    c = lax.broadcasted_iota(jnp.int32, (LANE, LANE), 1) // HALF
    return (r == c).astype(BF16)


def _split(x):
    hi = x.astype(BF16)
    return hi, (x - hi.astype(F32)).astype(BF16)


def _dot_left2(x, w):
    hi, lo = _split(x)
    return _dot(hi, w) + _dot(lo, w)


def _tri_cumsum(tri, x):
    hi, lo = _split(x)
    return _dot(tri, hi) + _dot(tri, lo)


def _segsum(x, ones):
    return _dot_left2(x, ones)


def _ada_body(c_ref, w_ref, b_ref, o_ref):
    c = c_ref[...]
    s = (c * _sigmoid(c)).astype(BF16)
    o_ref[...] = _dot(s, w_ref[...].astype(BF16)) + b_ref[...]


def _ada(c8, ada_w, ada_b, layer, tn=512):
    _, d, n = ada_w.shape
    return pl.pallas_call(
        _ada_body,
        out_shape=jax.ShapeDtypeStruct((8, n), F32),
        grid=(n // tn,),
        in_specs=[pl.BlockSpec((8, d), lambda j: (0, 0)),
                  pl.BlockSpec((None, d, tn), lambda j: (layer, 0, j)),
                  pl.BlockSpec((1, tn), lambda j: (0, j))],
        out_specs=pl.BlockSpec((8, tn), lambda j: (0, j)),
        compiler_params=_cparams(("parallel",)),
        name="ada_mod",
    )(c8, ada_w, ada_b[layer].reshape(1, n))


def _cast_body(w_ref, o_ref):
    o_ref[...] = w_ref[...].astype(o_ref.dtype)


def _cast_bf16(w, layer, tr=256):
    _, rows, cols = w.shape
    return pl.pallas_call(
        _cast_body,
        out_shape=jax.ShapeDtypeStruct((rows, cols), BF16),
        grid=(rows // tr,),
        in_specs=[pl.BlockSpec((None, tr, cols), lambda i: (layer, i, 0))],
        out_specs=pl.BlockSpec((tr, cols), lambda i: (i, 0)),
        compiler_params=_cparams(("parallel",)),
        name="weight_cast",
    )(w)


P_IN = 10704
SRC_B, SRC_C, SRC_D = 3072, 6160, 9424
SRC_BS, SRC_CS = SRC_B + 3072, SRC_C + 3072


PIECE = 16
PIECES = LANE // PIECE


def _piece_index(j, s):
    n_ab, n_c, n_d = OFF_C // LANE, OFF_D // LANE, OFF_S // LANE
    small0 = SRC_BS // PIECE if s == 0 else SRC_CS // PIECE + s - 1
    small1 = SRC_CS // PIECE + PIECES - 1 + min(s, 4)
    return jnp.where(j < n_ab, j * PIECES + s,
           jnp.where(j < n_c, SRC_C // PIECE + (j - n_ab) * PIECES + s,
           jnp.where(j < n_d, SRC_D // PIECE + (j - n_c) * PIECES + s,
           jnp.where(j == n_d, small0, small1))))


def _regroup_body(*refs):
    w_refs, vr_ref, o_ref = refs[:PIECES], refs[PIECES], refs[PIECES + 1]
    last = pl.program_id(0) == pl.num_programs(0) - 1
    pieces = [r[...] for r in w_refs]
    pieces[5] = jnp.where(last, vr_ref[:PIECE, :], pieces[5])
    pieces[6] = jnp.where(last, vr_ref[PIECE:, :], pieces[6])
    pieces[7] = jnp.where(last, 0.0, pieces[7])
    o_ref[...] = jnp.concatenate(pieces, axis=0).T.astype(o_ref.dtype)


def _regroup_w_in(w_in, layer, vres_down):
    _, d, _ = w_in.shape
    w_t = jnp.swapaxes(w_in, 1, 2)
    piece = lambda s: pl.BlockSpec((None, PIECE, d), lambda j: (layer, _piece_index(j, s), 0))
    return pl.pallas_call(
        _regroup_body,
        out_shape=jax.ShapeDtypeStruct((d, NCOLS), BF16),
        grid=(NCOLS // LANE,),
        in_specs=[piece(s) for s in range(PIECES)] + [pl.BlockSpec((2 * PIECE, d), lambda j: (0, 0))],
        out_specs=pl.BlockSpec((d, LANE), lambda j: (0, j)),
        compiler_params=_cparams(("parallel",)),
        name="w_in_regroup",
    )(*([w_t] * PIECES), vres_down.T)


def _norm_body(x_ref, sc_ref, sh_ref, o_ref):
    x = x_ref[0]
    ms = jnp.mean(x * x, axis=-1, keepdims=True)
    h = x * lax.rsqrt(ms + NORM_EPS) * (1.0 + sc_ref[0]) + sh_ref[0]
    o_ref[...] = h.astype(o_ref.dtype)


def _norm_mod(x, mod3, sc_idx, sh_idx, ts=256):
    bsz, seq, d = x.shape
    ts = min(ts, seq)
    ns = seq // ts
    return pl.pallas_call(
        _norm_body,
        out_shape=jax.ShapeDtypeStruct((bsz * seq, d), BF16),
        grid=(bsz, ns),
        in_specs=[pl.BlockSpec((1, ts, d), lambda b, i: (b, i, 0)),
                  pl.BlockSpec((1, 1, d), lambda b, i: (b * 6 + sc_idx, 0, 0)),
                  pl.BlockSpec((1, 1, d), lambda b, i: (b * 6 + sh_idx, 0, 0))],
        out_specs=pl.BlockSpec((ts, d), lambda b, i: (b * ns + i, 0)),
        compiler_params=_cparams(("parallel", "parallel")),
        name="norm_mod",
    )(x, mod3, mod3)


def _in_proj_body(a_ref, b_ref, main_ref, tail_ref):
    j = pl.program_id(1)
    acc = _dot(a_ref[...], b_ref[...])

    @pl.when(j < pl.num_programs(1) - 1)
    def _():
        main_ref[...] = acc.astype(main_ref.dtype)

    @pl.when(j == pl.num_programs(1) - 1)
    def _():
        tail_ref[...] = acc


def _in_proj(a, b, tm=1024):
    m, k = a.shape
    tm = min(tm, m)
    n_main = OFF_T // TAIL
    return pl.pallas_call(
        _in_proj_body,
        out_shape=[jax.ShapeDtypeStruct((m, OFF_T), BF16), jax.ShapeDtypeStruct((m, TAIL), F32)],
        grid=(m // tm, n_main + 1),
        in_specs=[pl.BlockSpec((tm, k), lambda i, j: (i, 0)),
                  pl.BlockSpec((k, TAIL), lambda i, j: (0, j))],
        out_specs=[pl.BlockSpec((tm, TAIL), lambda i, j: (i, jnp.minimum(j, n_main - 1))),
                   pl.BlockSpec((tm, TAIL), lambda i, j: (i, 0))],
        compiler_params=_cparams(("parallel", "arbitrary")),
        name="in_proj",
    )(a, b)


def _outproj_body(a0, a1, a2, a3, w_ref, x_ref, g_ref, o_ref):
    acc = _dot(a0[...], w_ref[0 * GROUP:1 * GROUP, :])
    acc += _dot(a1[...], w_ref[1 * GROUP:2 * GROUP, :])
    acc += _dot(a2[...], w_ref[2 * GROUP:3 * GROUP, :])
    acc += _dot(a3[...], w_ref[3 * GROUP:4 * GROUP, :])
    o_ref[0] = x_ref[0] + g_ref[0] * acc


def _out_proj(parts, w, x, mod3, gate_idx, tm=1024, tn=512):
    bsz, seq, d = x.shape
    tm = min(tm, seq)
    ns = seq // tm
    a_spec = pl.BlockSpec((tm, GROUP), lambda b, i, j: (b * ns + i, 0))
    return pl.pallas_call(
        _outproj_body,
        out_shape=jax.ShapeDtypeStruct((bsz, seq, d), F32),
        grid=(bsz, ns, d // tn),
        in_specs=[a_spec, a_spec, a_spec, a_spec,
                  pl.BlockSpec((4 * GROUP, tn), lambda b, i, j: (0, j)),
                  pl.BlockSpec((1, tm, tn), lambda b, i, j: (b, i, j)),
                  pl.BlockSpec((1, 1, tn), lambda b, i, j: (b * 6 + gate_idx, 0, j))],
        out_specs=pl.BlockSpec((1, tm, tn), lambda b, i, j: (b, i, j)),
        compiler_params=_cparams(("parallel", "parallel", "parallel")),
        name="out_proj",
    )(*parts, w, x, mod3)


def _ffn_up_body(a_ref, w1_ref, w3_ref, o_ref):
    a = a_ref[...]
    u = _dot(a, w1_ref[...].astype(BF16))
    v = _dot(a, w3_ref[...].astype(BF16))
    o_ref[...] = (u * _sigmoid(u) * v).astype(o_ref.dtype)


def _ffn_up(h, w1, w3, layer, tm=1024, tn=256):
    m, k = h.shape
    n = w1.shape[2]
    tm = min(tm, m)
    return pl.pallas_call(
        _ffn_up_body,
        out_shape=jax.ShapeDtypeStruct((m, n), BF16),
        grid=(m // tm, n // tn),
        in_specs=[pl.BlockSpec((tm, k), lambda i, j: (i, 0)),
                  pl.BlockSpec((None, k, tn), lambda i, j: (layer, 0, j)),
                  pl.BlockSpec((None, k, tn), lambda i, j: (layer, 0, j))],
        out_specs=pl.BlockSpec((tm, tn), lambda i, j: (i, j)),
        compiler_params=_cparams(("parallel", "parallel")),
        name="ffn_up",
    )(h, w1, w3)


def _ffn_down_body(a_ref, w_ref, x_ref, g_ref, o_ref):
    o_ref[0] = x_ref[0] + g_ref[0] * _dot(a_ref[...], w_ref[...])


def _ffn_down(a, w, x, mod3, gate_idx, tm=512, tn=512):
    bsz, seq, d = x.shape
    k = a.shape[1]
    tm = min(tm, seq)
    ns = seq // tm
    return pl.pallas_call(
        _ffn_down_body,
        out_shape=jax.ShapeDtypeStruct((bsz, seq, d), F32),
        grid=(bsz, ns, d // tn),
        in_specs=[pl.BlockSpec((tm, k), lambda b, i, j: (b * ns + i, 0)),
                  pl.BlockSpec((k, tn), lambda b, i, j: (0, j)),
                  pl.BlockSpec((1, tm, tn), lambda b, i, j: (b, i, j)),
                  pl.BlockSpec((1, 1, tn), lambda b, i, j: (b * 6 + gate_idx, 0, j))],
        out_specs=pl.BlockSpec((1, tm, tn), lambda b, i, j: (b, i, j)),
        compiler_params=_cparams(("parallel", "parallel", "parallel")),
        name="ffn_down",
    )(a, w, x, mod3)


def _bias_body(tab_ref, o_ref, *, off, window, shift, key_major, scale):
    h = pl.program_id(0)
    rows, cols = o_ref.shape[1], o_ref.shape[2]
    last = tab_ref[REL_BUCKETS - 1, h]
    max_exact = REL_BUCKETS // 2

    def block(d0):
        qi = lax.broadcasted_iota(jnp.int32, (LANE, LANE), 1 if key_major else 0)
        kj = lax.broadcasted_iota(jnp.int32, (LANE, LANE), 0 if key_major else 1)
        dist = d0 + qi - kj
        n = jnp.maximum(dist, 0)
        nf = jnp.maximum(n, 1).astype(F32)
        large = max_exact + (jnp.log(nf / max_exact) / math.log(REL_MAX_DIST / max_exact)
                             * (REL_BUCKETS - max_exact)).astype(jnp.int32)
        large = jnp.minimum(large, REL_BUCKETS - 1)
        bucket = jnp.where(n < max_exact, n, large)
        bias = jnp.full((LANE, LANE), last, F32)
        for b in range(REL_BUCKETS - 1):
            bias = jnp.where(bucket == b, tab_ref[b, h], bias)
        if shift:
            bias = bias - last
        if scale != 1.0:
            bias = bias * scale
        valid = dist >= 0
        if window is not None:
            valid = jnp.logical_and(valid, dist < window)
        return jnp.where(valid, bias, NEG_INF)

    cache = {}
    for br in range(rows // LANE):
        for bc in range(cols // LANE):
            qb, kb = (bc, br) if key_major else (br, bc)
            d0 = off + LANE * (qb - kb)
            if d0 not in cache:
                if d0 + LANE <= 0 or (window is not None and d0 - LANE >= window):
                    cache[d0] = jnp.full((LANE, LANE), NEG_INF, F32)
                elif d0 - LANE >= REL_MAX_DIST and (window is None or d0 + LANE <= window):
                    const = (0.0 if shift else last) * scale
                    cache[d0] = jnp.full((LANE, LANE), const, F32)
                else:
                    cache[d0] = block(d0)
            o_ref[0, br * LANE:(br + 1) * LANE, bc * LANE:(bc + 1) * LANE] = cache[d0]


def _bias_tiles(table, rows, cols, off, window, shift, key_major=False, scale=1.0):
    nh = table.shape[1]
    return pl.pallas_call(
        functools.partial(_bias_body, off=off, window=window, shift=shift, key_major=key_major, scale=scale),
        out_shape=jax.ShapeDtypeStruct((nh, rows, cols), F32),
        grid=(nh,),
        in_specs=[pl.BlockSpec(memory_space=pltpu.SMEM)],
        out_specs=pl.BlockSpec((1, rows, cols), lambda h: (h, 0, 0)),
        compiler_params=_cparams(("parallel",)),
        name="rel_bias_tiles",
    )(table)


LOG2E = 1.4426950408889634
VT_ROWS = LANE + 16
Q_CHUNK = 512


def _prep_a_body(c_ref, qg_ref, kg_ref, qk_ref, vt_ref):
    ones = _seg_ones()
    scale = DIFF_QK ** -0.5 * LOG2E
    tm = c_ref.shape[0]
    for j in range(2 * DIFF_HEADS):
        x = c_ref[:, j * LANE:(j + 1) * LANE].astype(F32)
        ms = _segsum(x * x, ones) * (1.0 / DIFF_QK)
        gain = qg_ref[...] * scale if j < DIFF_HEADS else kg_ref[...]
        qk_ref[:, j * LANE:(j + 1) * LANE] = (x * lax.rsqrt(ms + NORM_EPS) * gain).astype(qk_ref.dtype)
    for h in range(DIFF_HEADS):
        v = c_ref[:, 2 * GROUP + h * LANE:2 * GROUP + (h + 1) * LANE].astype(F32)
        vt_ref[h * VT_ROWS:h * VT_ROWS + LANE, :] = v.T.astype(vt_ref.dtype)
        vt_ref[h * VT_ROWS + LANE:(h + 1) * VT_ROWS, :] = jnp.ones((VT_ROWS - LANE, tm), vt_ref.dtype)


def _prep_a(cols, q_gain, k_gain, tm=256):
    t = cols.shape[0]
    tm = min(tm, t)
    qg = jnp.tile(q_gain, 2).reshape(1, LANE)
    kg = jnp.tile(k_gain, 2).reshape(1, LANE)
    vec = pl.BlockSpec((1, LANE), lambda i: (0, 0))
    return pl.pallas_call(
        _prep_a_body,
        out_shape=[jax.ShapeDtypeStruct((t, 2 * GROUP), BF16),
                   jax.ShapeDtypeStruct((DIFF_HEADS * VT_ROWS, t), BF16)],
        grid=(t // tm,),
        in_specs=[pl.BlockSpec((tm, 3 * GROUP), lambda i: (i, OFF_A // (3 * GROUP))), vec, vec],
        out_specs=[pl.BlockSpec((tm, 2 * GROUP), lambda i: (i, 0)),
                   pl.BlockSpec((DIFF_HEADS * VT_ROWS, tm), lambda i: (0, i))],
        compiler_params=_cparams(("parallel",)),
        name="diff_prep",
    )(cols, qg, kg)


def _flash_body(it_ref, jt_ref, q_ref, k_ref, vt_ref, vtp_ref, bd_ref, bs_ref, lam_ref, sub_ref, o_ref,
                qlo, qhi, m1, a1, m2, a2, p_scr, al_scr, *, lam_init):
    t = pl.program_id(2)
    i = it_ref[t]
    j = jt_ref[t]

    @pl.when(j == 0)
    def _():
        q = q_ref[...]
        lane = lax.broadcasted_iota(jnp.int32, q.shape, 1)
        zero = jnp.zeros_like(q)
        qlo[...] = jnp.where(lane < HALF, q, zero)
        qhi[...] = jnp.where(lane >= HALF, q, zero)
        for m, a in ((m1, a1), (m2, a2)):
            m[...] = jnp.full(m.shape, NEG_INF, F32)
            a[...] = jnp.zeros(a.shape, F32)
        al_scr[...] = jnp.ones(al_scr.shape, F32)

    @pl.when(t == 0)
    def _():
        p_scr[...] = jnp.zeros(p_scr.shape, BF16)

    pending = (j > 0).astype(F32)

    def step(bias, last):
        k = k_ref[...]
        vt = vt_ref[...]
        vtp = vtp_ref[...]
        nq = q_ref.shape[0]
        qc = min(Q_CHUNK, nq)
        chains = [(mi, qq, m, a, slice(c * qc, (c + 1) * qc))
                  for mi, (qq, m, a) in enumerate(((qlo, m1, a1), (qhi, m2, a2))) for c in range(nq // qc)]
        ahead = 2
        kend = [qs.stop if last else k.shape[0] for _, _, _, _, qs in chains]
        if bias is None:
            scores = [_dot_nt(k, qq[qs, :]) for _, qq, _, _, qs in chains[:ahead]]
        elif last:
            scores = [_dot_nt(k[:ke, :], c[1][c[4], :]) + bias[:ke, c[4]] for c, ke in zip(chains, kend)]
        else:
            full = [_dot_nt(k, qq[...]) for qq in (qlo, qhi)]
            scores = [full[c[0]][:, c[4]] + bias[:, c[4]] for c in chains]
        for idx, (mi, _, m, a, qs) in enumerate(chains):
            if bias is None and idx + ahead < len(chains):
                _, qq_n, _, _, qs_n = chains[idx + ahead]
                scores.append(_dot_nt(k, qq_n[qs_n, :]))
            acc = al_scr[mi, :, qs] * a[:, qs] + pending * _dot(vtp, p_scr[mi, :, qs])
            s = scores[idx]
            m_old = m[:, qs]
            m_new = jnp.maximum(m_old, jnp.max(s, axis=0, keepdims=True))
            p = jnp.exp2(s - m_new).astype(BF16)
            alpha = jnp.exp2(m_old - m_new)
            if last:
                acc = alpha * acc + _dot(vt[:, :kend[idx]], p)
            else:
                p_scr[mi, :, qs] = p
                al_scr[mi, :, qs] = alpha
            a[:, qs] = acc
            m[:, qs] = m_new

    @pl.when(j < i - 1)
    def _():
        step(None, False)

    @pl.when(j == i - 1)
    def _():
        step(bs_ref[0], False)

    @pl.when(j == i)
    def _():
        step(bd_ref[0], True)
        lam = lam_ref[...]
        e1 = jnp.exp(jnp.sum(lam[0:1] * lam[1:2], axis=-1, keepdims=True))
        e2 = jnp.exp(jnp.sum(lam[2:3] * lam[3:4], axis=-1, keepdims=True))
        lam_full = e1 - e2 + lam_init
        o = a1[:LANE, :] / a1[LANE:LANE + 1, :] - lam_full * (a2[:LANE, :] / a2[LANE:LANE + 1, :])
        ms = jnp.mean(o * o, axis=0, keepdims=True)
        o = o * lax.rsqrt(ms + NORM_EPS) * sub_ref[...] * (1.0 - lam_init)
        o_ref[...] = o.T.astype(o_ref.dtype)


def _diff_attention(qk, vt, bias_diag, bias_sub, lam, subln, bsz, seq, layer_idx, tile):
    nq = seq // tile
    lam_init = 0.8 - 0.6 * math.exp(-0.3 * layer_idx)
    pairs = [(i, j) for i in range(nq) for j in range(i + 1)]
    i_tab = jnp.array([ij[0] for ij in pairs], jnp.int32)
    j_tab = jnp.array([ij[1] for ij in pairs], jnp.int32)
    grid_spec = pltpu.PrefetchScalarGridSpec(
        num_scalar_prefetch=2,
        grid=(bsz, DIFF_HEADS, len(pairs)),
        in_specs=[pl.BlockSpec((tile, LANE), lambda b, h, t, it, jt: (b * nq + it[t], h)),
                  pl.BlockSpec((tile, LANE), lambda b, h, t, it, jt: (b * nq + jt[t], DIFF_HEADS + h)),
                  pl.BlockSpec((VT_ROWS, tile), lambda b, h, t, it, jt: (h, b * nq + jt[t])),
                  pl.BlockSpec((VT_ROWS, tile), lambda b, h, t, it, jt: (h, b * nq + jnp.maximum(jt[t] - 1, 0))),
                  pl.BlockSpec((1, tile, tile), lambda b, h, t, it, jt: (h, 0, 0)),
                  pl.BlockSpec((1, tile, tile), lambda b, h, t, it, jt: (h, 0, 0)),
                  pl.BlockSpec((4, DIFF_QK), lambda b, h, t, it, jt: (0, 0)),
                  pl.BlockSpec((LANE, 1), lambda b, h, t, it, jt: (0, 0))],
        out_specs=pl.BlockSpec((tile, LANE), lambda b, h, t, it, jt: (b * nq + it[t], h)),
        scratch_shapes=[pltpu.VMEM((tile, LANE), BF16), pltpu.VMEM((tile, LANE), BF16),
                        pltpu.VMEM((1, tile), F32), pltpu.VMEM((VT_ROWS, tile), F32),
                        pltpu.VMEM((1, tile), F32), pltpu.VMEM((VT_ROWS, tile), F32),
                        pltpu.VMEM((2, tile, tile), BF16), pltpu.VMEM((2, 1, tile), F32)])
    return pl.pallas_call(
        functools.partial(_flash_body, lam_init=lam_init),
        out_shape=jax.ShapeDtypeStruct((bsz * seq, GROUP), BF16),
        grid_spec=grid_spec,
        compiler_params=_cparams(("parallel", "parallel", "arbitrary")),
        name="diff_attention",
    )(i_tab, j_tab, qk, qk, vt, vt, bias_diag, bias_sub, lam, subln.reshape(LANE, 1))


def _swa_body(q_ref, kp_ref, kc_ref, vp_ref, vc_ref, bias_ref, qg_ref, kg_ref, sink_ref, o_ref):
    n = pl.program_id(1)
    g = pl.program_id(2)
    ones = _seg_ones()
    lane = lax.broadcasted_iota(jnp.int32, (SWA_BLOCK, LANE), 1)
    lo = lane < HALF

    def norm(x, gain):
        ms = _segsum(x * x, ones) * (1.0 / HALF)
        return x * lax.rsqrt(ms + NORM_EPS) * gain

    k = jnp.concatenate([norm(kp_ref[...], kg_ref[...]), norm(kc_ref[...], kg_ref[...])], axis=0)
    v = jnp.concatenate([vp_ref[...], vc_ref[...]], axis=0)
    k_sw = pltpu.roll(k, HALF, axis=1)
    v_sw = pltpu.roll(v, HALF, axis=1)
    first_head = g == 0
    lo2 = lax.broadcasted_iota(jnp.int32, (2 * SWA_BLOCK, LANE), 1) < HALF
    k_a = jnp.where(first_head, k, k_sw).astype(BF16)
    k_b = jnp.where(first_head, k_sw, k).astype(BF16)
    v_mine_lo = jnp.where(first_head, v, v_sw)
    v_mine_hi = jnp.where(first_head, v_sw, v)
    v_a = jnp.where(lo2, v_mine_lo, 0.0).astype(BF16)
    v_b = jnp.where(lo2, 0.0, v_mine_hi).astype(BF16)
    kcol = lax.broadcasted_iota(jnp.int32, (SWA_BLOCK, 2 * SWA_BLOCK), 1)
    pad = jnp.logical_and(n == 0, kcol < SWA_BLOCK)
    scale = HALF ** -0.5
    grp = SWA_HEADS // SWA_KV_HEADS
    heads = range(grp)
    q = [norm(q_ref[:, pr * LANE:(pr + 1) * LANE].astype(F32), qg_ref[...] * scale) for pr in range(grp // 2)]
    qm = [jnp.where(lo if hh % 2 == 0 else jnp.logical_not(lo), q[hh // 2], 0.0).astype(BF16) for hh in heads]
    s = [_dot_nt(qm[hh], k_a if hh % 2 == 0 else k_b) for hh in heads]
    s = [jnp.where(pad, NEG_INF, s[hh] + bias_ref[hh]) for hh in heads]
    sink = [sink_ref[g * grp + hh] for hh in heads]
    m = [jnp.maximum(jnp.max(s[hh], axis=-1, keepdims=True), sink[hh]) for hh in heads]
    p = [jnp.exp(s[hh] - m[hh]) for hh in heads]
    den = [jnp.sum(p[hh], axis=-1, keepdims=True) + jnp.exp(sink[hh] - m[hh]) for hh in heads]
    pv = [_dot(p[hh].astype(BF16), v_a if hh % 2 == 0 else v_b) for hh in heads]
    for pr in range(grp // 2):
        o = pv[2 * pr] / den[2 * pr] + pv[2 * pr + 1] / den[2 * pr + 1]
        o_ref[:, pr * LANE:(pr + 1) * LANE] = o.astype(o_ref.dtype)


def _swa_attention(cols, tail, bias, q_gain, k_gain, sinks, bsz, seq):
    nb = seq // SWA_BLOCK
    grp = SWA_HEADS // SWA_KV_HEADS
    qw = grp * HALF
    kcol = (OFF_D + GROUP - OFF_T) // LANE
    vcol = kcol + 1
    prev = lambda b, n, g: b * nb + jnp.maximum(n - 1, 0)
    cur = lambda b, n, g: b * nb + n
    vec = pl.BlockSpec((1, LANE), lambda b, n, g: (0, 0))
    return pl.pallas_call(
        _swa_body,
        out_shape=jax.ShapeDtypeStruct((bsz * seq, GROUP), BF16),
        grid=(bsz, nb, SWA_KV_HEADS),
        in_specs=[pl.BlockSpec((SWA_BLOCK, qw), lambda b, n, g: (cur(b, n, g), OFF_D // qw + g)),
                  pl.BlockSpec((SWA_BLOCK, LANE), lambda b, n, g: (prev(b, n, g), kcol)),
                  pl.BlockSpec((SWA_BLOCK, LANE), lambda b, n, g: (cur(b, n, g), kcol)),
                  pl.BlockSpec((SWA_BLOCK, LANE), lambda b, n, g: (prev(b, n, g), vcol)),
                  pl.BlockSpec((SWA_BLOCK, LANE), lambda b, n, g: (cur(b, n, g), vcol)),
                  pl.BlockSpec((grp, SWA_BLOCK, 2 * SWA_BLOCK), lambda b, n, g: (g, 0, 0)),
                  vec, vec,
                  pl.BlockSpec(memory_space=pltpu.SMEM)],
        out_specs=pl.BlockSpec((SWA_BLOCK, qw), lambda b, n, g: (cur(b, n, g), g)),
        compiler_params=_cparams(("parallel", "parallel", "parallel")),
        name="swa_attention",
    )(cols, tail, tail, tail, tail, bias,
      jnp.tile(q_gain, 2).reshape(1, LANE), jnp.tile(k_gain, 2).reshape(1, LANE), sinks)


def _prep_c_body(*refs, seq, with_vres):
    (cm_ref, cs_ref, pm_ref, ps_ref, mum_ref, mus_ref, wup_ref, aup_ref, gup_ref, vup_ref, glaup_ref,
     w0_ref, a0_ref, v0_ref, kk_ref, ka_ref, glab_ref) = refs[:17]
    rest = refs[17:]
    if with_vres:
        vf_ref, rest = rest[0], rest[1:]
    r_ref, lw_ref, k_ref, v_ref, nk_ref, kb_ref, g_ref, la_ref = rest
    tm = cm_ref.shape[0]
    i = pl.program_id(0)
    seq_start = (i * tm) % seq == 0

    def shifted(cur, prev_rows, mu):
        row = lax.broadcasted_iota(jnp.int32, cur.shape, 0)
        before = jnp.where(seq_start, 0.0, prev_rows[PREV_ROWS - 1:PREV_ROWS, :].astype(F32))
        prev = jnp.where(row == 0, before, pltpu.roll(cur, 1, axis=0))
        return cur + (prev - cur) * mu

    def low_rank(x, w2_ref):
        hi, lo = _split(x)
        return _dot(hi, w2_ref[0]) + _dot(lo, w2_ref[0]) + _dot(hi, w2_ref[1])

    cs = cs_ref[...]
    la_ref[...] = -_softplus(-(low_rank(cs, glaup_ref) + glab_ref[...])) * (1.0 / GLA_NORMALIZER)
    sm = shifted(cm_ref[...].astype(F32), pm_ref, mum_ref[...])
    ss = shifted(cs, ps_ref, mus_ref[...])
    r = sm[:, :GROUP]
    k = sm[:, GROUP:2 * GROUP]
    v = sm[:, 2 * GROUP:]
    if with_vres:
        gate = _sigmoid(v0_ref[...] + low_rank(ss, vup_ref))
        v = v + (vf_ref[...] - v) * gate
    w_log = -_softplus(-(w0_ref[...] + low_rank(jnp.tanh(ss), wup_ref))) - 0.5
    a = _sigmoid(a0_ref[...] + low_rank(ss, aup_ref))
    r_ref[...] = r.astype(r_ref.dtype)
    lw_ref[...] = -jnp.exp(w_log)
    v_ref[...] = v
    g_ref[...] = low_rank(_sigmoid(ss), gup_ref).astype(g_ref.dtype)
    k_ref[...] = (k * (1.0 + (a - 1.0) * ka_ref[...])).astype(k_ref.dtype)
    ones = _seg_ones()
    kk = k * kk_ref[...]
    for j in range(GROUP // LANE):
        sl = slice(j * LANE, (j + 1) * LANE)
        x = kk[:, sl]
        nrm = jnp.maximum(jnp.sqrt(_segsum(x * x, ones)), 1e-12)
        x = x / nrm
        nk_ref[:, sl] = x.astype(nk_ref.dtype)
        kb_ref[:, sl] = (x * a[:, sl]).astype(kb_ref.dtype)


PREV_ROWS = 16


def _prep_c(cols, tail, seq, mu_main, mu_small, w_up, a_up, g_up, v_up, gla_up, w0, a0, v0, k_k, k_a, gla_bias,
            v_first, tm=256):
    t = cols.shape[0]
    tm = min(tm, seq)
    with_vres = v_first is not None
    cmain = OFF_C // (3 * GROUP)
    csmall = (OFF_S - OFF_T) // SMALL
    prev = lambda i: jnp.maximum(i * (tm // PREV_ROWS) - 1, 0)
    full = lambda r, c: pl.BlockSpec((r, c), lambda i: (0, 0))
    hilo = lambda c: pl.BlockSpec((2, SMALL, c), lambda i: (0, 0, 0))
    in_specs = [pl.BlockSpec((tm, 3 * GROUP), lambda i: (i, cmain)),
                pl.BlockSpec((tm, SMALL), lambda i: (i, csmall)),
                pl.BlockSpec((PREV_ROWS, 3 * GROUP), lambda i: (prev(i), cmain)),
                pl.BlockSpec((PREV_ROWS, SMALL), lambda i: (prev(i), csmall)),
                full(1, 3 * GROUP), full(1, SMALL),
                hilo(GROUP), hilo(GROUP), hilo(GROUP), hilo(GROUP), hilo(GLA_HEADS * GLA_DK),
                full(1, GROUP), full(1, GROUP), full(1, GROUP), full(1, GROUP), full(1, GROUP),
                full(1, GLA_HEADS * GLA_DK)]
    args = [cols, tail, cols, tail, mu_main, mu_small, w_up, a_up, g_up, v_up, gla_up,
            w0, a0, v0, k_k, k_a, gla_bias]
    if with_vres:
        in_specs.append(pl.BlockSpec((tm, GROUP), lambda i: (i, 0)))
        args.append(v_first)
    row = pl.BlockSpec((tm, GROUP), lambda i: (i, 0))
    out = [jax.ShapeDtypeStruct((t, GROUP), dt) for dt in (BF16, F32, BF16, F32, BF16, BF16, BF16)]
    return pl.pallas_call(
        functools.partial(_prep_c_body, seq=seq, with_vres=with_vres),
        out_shape=out + [jax.ShapeDtypeStruct((t, GLA_HEADS * GLA_DK), F32)],
        grid=(t // tm,),
        in_specs=in_specs,
        out_specs=[row] * 7 + [pl.BlockSpec((tm, GLA_HEADS * GLA_DK), lambda i: (i, 0))],
        compiler_params=_cparams(("parallel",)),
        name="rwkv_gla_prep",
    )(*args)


def _gla_body(q_ref, k_ref, v_ref, g_ref, la_ref, gain_ref, o_ref, st_ref):
    @pl.when(pl.program_id(2) == 0)
    def _():
        st_ref[...] = jnp.zeros(st_ref.shape, F32)

    row = lax.broadcasted_iota(jnp.int32, (CHUNK, CHUNK), 0)
    col = lax.broadcasted_iota(jnp.int32, (CHUNK, CHUNK), 1)
    causal = row >= col
    tri = causal.astype(BF16)
    nchunk = q_ref.shape[0] // CHUNK

    def local(sls):
        n = range(len(sls))
        k = [k_ref[sl, :].astype(F32) for sl in sls]
        v = [v_ref[sl, :].astype(BF16) for sl in sls]
        b = [_tri_cumsum(tri, la_ref[sl, :]) for sl in sls]
        b_last = [b[i][CHUNK - 1:CHUNK, :] for i in n]
        q_dec = [(q_ref[sl, :].astype(F32) * (GLA_DK ** -0.5) * jnp.exp(b[i])).astype(BF16)
                 for i, sl in enumerate(sls)]
        a_intra = [jnp.where(causal, _dot_nt(q_dec[i], (k[i] * jnp.exp(-b[i])).astype(BF16)), 0.0) for i in n]
        upd = [_dot_tn(v[i], (k[i] * jnp.exp(b_last[i] - b[i])).astype(BF16)) for i in n]
        o_intra = [_dot(a_intra[i].astype(BF16), v[i]) for i in n]
        return [(q_dec[i], o_intra[i], upd[i], jnp.exp(b_last[i])) for i in n]

    def advance(sl, q_dec, o_intra, upd, dec):
        state = st_ref[...]
        s_hi, s_lo = _split(state)
        o = o_intra + _dot_nt(q_dec, s_hi) + _dot_nt(q_dec, s_lo)
        st_ref[...] = state * dec + upd
        ms = jnp.mean(o * o, axis=-1, keepdims=True)
        o = o * lax.rsqrt(ms + NORM_EPS) * gain_ref[...]
        gate = g_ref[sl, :].astype(F32)
        o_ref[sl, :] = (o * (gate * _sigmoid(gate))).astype(o_ref.dtype)

    def group(gi, carry):
        sls = [pl.ds(pl.multiple_of((gi * SCAN_GROUP + g) * CHUNK, CHUNK), CHUNK) for g in range(SCAN_GROUP)]
        for sl, part in zip(sls, local(sls)):
            advance(sl, *part)
        return carry

    lax.fori_loop(0, nchunk // SCAN_GROUP, group, 0)


def _gla(cols, log_a, out_gain, bsz, seq, tc=512):
    tc = min(tc, seq)
    ns = seq // tc
    qc = OFF_B // GLA_DK
    kc = qc + GLA_HEADS
    vc = (OFF_B + 2 * GLA_HEADS * GLA_DK) // GLA_DV
    gc = vc + GLA_HEADS
    rows = lambda b, h, i: b * ns + i
    return pl.pallas_call(
        _gla_body,
        out_shape=jax.ShapeDtypeStruct((bsz * seq, GROUP), BF16),
        grid=(bsz, GLA_HEADS, ns),
        in_specs=[pl.BlockSpec((tc, GLA_DK), lambda b, h, i: (rows(b, h, i), qc + h)),
                  pl.BlockSpec((tc, GLA_DK), lambda b, h, i: (rows(b, h, i), kc + h)),
                  pl.BlockSpec((tc, GLA_DV), lambda b, h, i: (rows(b, h, i), vc + h)),
                  pl.BlockSpec((tc, GLA_DV), lambda b, h, i: (rows(b, h, i), gc + h)),
                  pl.BlockSpec((tc, GLA_DK), lambda b, h, i: (rows(b, h, i), h)),
                  pl.BlockSpec((1, GLA_DV), lambda b, h, i: (0, 0))],
        out_specs=pl.BlockSpec((tc, GLA_DV), lambda b, h, i: (rows(b, h, i), h)),
        scratch_shapes=[pltpu.VMEM((GLA_DV, GLA_DK), F32)],
        compiler_params=_cparams(("parallel", "parallel", "arbitrary")),
        name="gla_scan",
    )(cols, cols, cols, cols, log_a, out_gain.reshape(1, GLA_DV))


def _rwkv_body(r_ref, lw_ref, k_ref, v_ref, nk_ref, kb_ref, g_ref, rk_ref, lnw_ref, lnb_ref, o_ref, st_ref):
    @pl.when(pl.program_id(2) == 0)
    def _():
        st_ref[...] = jnp.zeros(st_ref.shape, F32)

    two = 2 * CHUNK
    row = lax.broadcasted_iota(jnp.int32, (two, two), 0)
    col = lax.broadcasted_iota(jnp.int32, (two, two), 1)
    same = (row // CHUNK) == (col // CHUNK)
    strict = jnp.logical_and(same, (row % CHUNK) > (col % CHUNK))
    incl = jnp.logical_and(same, (row % CHUNK) >= (col % CHUNK))
    eye = (row == col).astype(F32)
    crow = lax.broadcasted_iota(jnp.int32, (CHUNK, CHUNK), 0)
    ccol = lax.broadcasted_iota(jnp.int32, (CHUNK, CHUNK), 1)
    tri = (crow >= ccol).astype(BF16)
    lo = lax.broadcasted_iota(jnp.int32, (CHUNK, LANE), 1) < HALF
    ones = _seg_ones()
    nchunk = r_ref.shape[0] // CHUNK

    def stack(x):
        return jnp.concatenate([jnp.where(lo, x, 0.0), jnp.where(lo, 0.0, x)], axis=0)

    def fold(x):
        return x[:CHUNK, :] + x[CHUNK:, :]

    def transitions(items):
        n = range(len(items))
        r = [r_ref[sl, ls].astype(F32) for sl, ls in items]
        lw = [lw_ref[sl, ls] for sl, ls in items]
        k = [k_ref[sl, ls].astype(F32) for sl, ls in items]
        v = [v_ref[sl, ls] for sl, ls in items]
        cum = [_tri_cumsum(tri, lw[i]) for i in n]
        last = [cum[i][CHUNK - 1:CHUNK, :] for i in n]
        a_bf, r_st, bk_st, bkh_st, v_st = [], [], [], [], []
        for i, (sl, ls) in enumerate(items):
            kb = kb_ref[sl, ls].astype(F32)
            e_neg = jnp.exp(-cum[i])
            e_rem = jnp.exp(last[i] - cum[i])
            a_bf.append(stack(-nk_ref[sl, ls].astype(F32) * jnp.exp(cum[i] - lw[i])).astype(BF16))
            r_st.append(stack(r[i] * jnp.exp(cum[i])))
            bk_st.append(jnp.concatenate([stack(kb * e_neg), stack(k[i] * e_neg)], axis=0).astype(BF16))
            bkh_st.append(jnp.concatenate([stack(kb * e_rem), stack(k[i] * e_rem)], axis=0).astype(BF16))
            v_st.append(stack(v[i]).astype(BF16))
        sc = [_dot_nt(jnp.concatenate([a_bf[i], r_st[i].astype(BF16)], axis=0), bk_st[i]) for i in n]
        a_ab = [jnp.where(strict, sc[i][:two, :two], 0.0) for i in n]
        a_ak = [jnp.where(strict, sc[i][:two, two:], 0.0).astype(BF16) for i in n]
        r_b = [jnp.where(incl, sc[i][two:, :two], 0.0).astype(BF16) for i in n]
        r_k = [jnp.where(incl, sc[i][two:, two:], 0.0).astype(BF16) for i in n]
        akv = [_dot(a_ak[i], v_st[i]).astype(BF16) for i in n]
        rkv = [_dot(r_k[i], v_st[i]) for i in n]
        inv = [eye + a_ab[i] for i in n]
        pw = [a_ab[i].astype(BF16) for i in n]
        for _ in range(5):
            pw = [_dot(pw[i], pw[i]).astype(BF16) for i in n]
            inv = [inv[i] + _dot(inv[i].astype(BF16), pw[i]) for i in n]
        pu = [_dot(inv[i].astype(BF16), jnp.concatenate([a_bf[i], akv[i]], axis=1)).astype(BF16) for i in n]
        rb_pu = [_dot(r_b[i], pu[i]) for i in n]
        m = [_dot_tn(pu[i][:, :LANE], bkh_st[i][:two, :]).astype(BF16) for i in n]
        n0 = [_dot_tn(jnp.concatenate([pu[i][:, LANE:], v_st[i]], axis=0), bkh_st[i]) for i in n]
        rkr = [_segsum(r[i] * k[i] * rk_ref[:, items[i][1]], ones) for i in n]
        return [(fold(r_st[i] + rb_pu[i][:, :LANE]).astype(BF16), fold(rb_pu[i][:, LANE:] + rkv[i]),
                 m[i], n0[i], jnp.exp(last[i]), rkr[i] * v[i]) for i in n]

    def advance(tile, sl, ls, p2, y0, m, n0, dec, bonus):
        state = st_ref[tile]
        s_hi, s_lo = _split(state)
        y = _dot_nt(p2, s_hi) + y0
        st_ref[tile] = state * dec + _dot(s_hi, m) + _dot(s_lo, m) + n0
        d = y - _segsum(y, ones) * (1.0 / HALF)
        var = _segsum(d * d, ones) * (1.0 / HALF)
        y = d * lax.rsqrt(var + RWKV_LN_EPS) * lnw_ref[:, ls] + lnb_ref[:, ls]
        o_ref[sl, ls] = ((y + bonus) * g_ref[sl, ls]).astype(o_ref.dtype)

    gsz = min(RWKV_GROUP, nchunk)

    def group(gi, carry):
        sls = [pl.ds(pl.multiple_of((gi * gsz + g) * CHUNK, CHUNK), CHUNK) for g in range(gsz)]
        tiles = range(r_ref.shape[1] // LANE)
        items = [(sl, slice(q * LANE, (q + 1) * LANE)) for sl in sls for q in tiles]
        for idx, ((sl, ls), part) in enumerate(zip(items, transitions(items))):
            advance(idx % len(tiles), sl, ls, *part)
        return carry

    lax.fori_loop(0, nchunk // gsz, group, 0)


def _rwkv_scan(r, lw, k, v, nk, kb, g, r_k, ln_w, ln_b, bsz, seq, tc=512):
    tc = min(tc, seq)
    ns = seq // tc
    width = RWKV_TILES * LANE
    npair = GROUP // width
    blk = pl.BlockSpec((tc, width), lambda b, p, i: (b * ns + i, p))
    vec = pl.BlockSpec((1, width), lambda b, p, i: (0, p))
    return pl.pallas_call(
        _rwkv_body,
        out_shape=jax.ShapeDtypeStruct((bsz * seq, GROUP), BF16),
        grid=(bsz, npair, ns),
        in_specs=[blk] * 7 + [vec] * 3,
        out_specs=blk,
        scratch_shapes=[pltpu.VMEM((RWKV_TILES, LANE, LANE), F32)],
        compiler_params=_cparams(("parallel", "parallel", "arbitrary")),
        name="rwkv_scan",
    )(r, lw, k, v, nk, kb, g, r_k.reshape(1, GROUP), ln_w.reshape(1, GROUP), ln_b.reshape(1, GROUP))


def _pad_rows(w, start):
    full = jnp.zeros((SMALL, w.shape[1]), F32).at[start:start + w.shape[0]].set(w)
    hi = full.astype(BF16)
    return jnp.stack([hi, (full - hi.astype(F32)).astype(BF16)])


def _layer(x, c8, layer_idx, v_first, vres, bias_ad, bias_as, bias_d, p, attn_tile):
    bsz, seq, d = x.shape
    t = bsz * seq
    mod = _ada(c8, p["ada_w"], p["ada_b"], layer_idx)
    mod3 = mod[:bsz].reshape(bsz * 6, 1, d)
    h = _norm_mod(x, mod3, 1, 0)

    vres_cols = vres[0] if vres is not None else jnp.zeros((d, 32), F32)
    w_r = _regroup_w_in(p["w_in"], layer_idx, vres_cols)
    cols, tail = _in_proj(h, w_r)

    qk, vt = _prep_a(cols, p["diff_q_norm"], p["diff_k_norm"])
    o_a = _diff_attention(qk, vt, bias_ad, bias_as, p["diff_lambda"], p["diff_subln"], bsz, seq, layer_idx,
                          attn_tile)
    o_dd = _swa_attention(cols, tail, bias_d, p["swa_q_norm"], p["swa_k_norm"], p["swa_sinks"], bsz, seq)
    mu = p["rwkv_mu"]
    vres_mu = vres[1] if vres is not None else jnp.zeros((32,), F32)
    mu_small = jnp.concatenate([jnp.zeros((S_WD,), F32), mu[3 * GROUP:], vres_mu,
                                jnp.zeros((SMALL - S_VR - 32,), F32)]).reshape(1, SMALL)
    v_up = _pad_rows(vres[2], S_VR) if vres is not None else jnp.zeros((2, SMALL, GROUP), BF16)
    v0 = vres[3] if vres is not None else jnp.zeros((GROUP,), F32)
    r_, lw_, k_, v_, nk_, kb_, g_, la_ = _prep_c(
        cols, tail, seq, mu[:3 * GROUP].reshape(1, -1), mu_small,
        _pad_rows(p["rwkv_w_up"], S_WD), _pad_rows(p["rwkv_a_up"], S_AD), _pad_rows(p["rwkv_g_up"], S_GD),
        v_up, _pad_rows(p["gla_gate_up"], S_GLA),
        p["rwkv_w0"].reshape(1, -1), p["rwkv_a0"].reshape(1, -1), v0.reshape(1, -1),
        p["rwkv_k_k"].reshape(1, -1), p["rwkv_k_a"].reshape(1, -1), p["gla_gate_bias"].reshape(1, -1),
        v_first if vres is not None else None)
    if vres is None:
        v_first = v_
    o_bb = _gla(cols, la_, p["gla_out_norm"], bsz, seq)
    o_c = _rwkv_scan(r_, lw_, k_, v_, nk_, kb_, g_, p["rwkv_r_k"].reshape(-1), p["rwkv_ln_w"], p["rwkv_ln_b"],
                     bsz, seq)

    x = _out_proj([o_a, o_bb, o_c, o_dd], p["w_out"], x, mod3, 2)
    h2 = _norm_mod(x, mod3, 4, 3)
    act = _ffn_up(h2, p["ffn_w1"], p["ffn_w3"], layer_idx)
    x = _ffn_down(act, p["ffn_w2"], x, mod3, 5)
    return x, v_first


def kernel(x, c, rel_bias, ada_w, ada_b, w_in, w_out, diff_q_norm, diff_k_norm, diff_lambda, diff_subln,
           gla_gate_up, gla_gate_bias, gla_out_norm, rwkv_mu, rwkv_w_up, rwkv_w0, rwkv_a_up, rwkv_a0,
           rwkv_g_up, rwkv_k_k, rwkv_k_a, rwkv_r_k, rwkv_ln_w, rwkv_ln_b, rwkv_vres_down, rwkv_vres_mu,
           rwkv_vres_up, rwkv_v0, swa_q_norm, swa_k_norm, swa_sinks, ffn_w1, ffn_w3, ffn_w2):
    bsz, seq, _ = x.shape
    depth = ada_w.shape[0]
    attn_tile = min(1024, seq)
    bias_ad = _bias_tiles(rel_bias[:, :DIFF_HEADS], attn_tile, attn_tile, 0, None, True, True, LOG2E)
    bias_as = _bias_tiles(rel_bias[:, :DIFF_HEADS], attn_tile, attn_tile, attn_tile, None, True, True, LOG2E)
    bias_d = _bias_tiles(rel_bias[:, DIFF_HEADS:], SWA_BLOCK, 2 * SWA_BLOCK, SWA_BLOCK, SWA_BLOCK, False)
    c8 = jnp.zeros((8, c.shape[1]), F32).at[:bsz].set(c)
    v_first = None
    for l in range(depth):
        p = dict(ada_w=ada_w, ada_b=ada_b, w_in=w_in, w_out=_cast_bf16(w_out, l),
                 diff_q_norm=diff_q_norm[l], diff_k_norm=diff_k_norm[l], diff_lambda=diff_lambda[l],
                 diff_subln=diff_subln[l], gla_gate_up=gla_gate_up[l], gla_gate_bias=gla_gate_bias[l],
                 gla_out_norm=gla_out_norm[l], rwkv_mu=rwkv_mu[l], rwkv_w_up=rwkv_w_up[l],
                 rwkv_w0=rwkv_w0[l], rwkv_a_up=rwkv_a_up[l], rwkv_a0=rwkv_a0[l], rwkv_g_up=rwkv_g_up[l],
                 rwkv_k_k=rwkv_k_k[l], rwkv_k_a=rwkv_k_a[l], rwkv_r_k=rwkv_r_k[l], rwkv_ln_w=rwkv_ln_w[l],
                 rwkv_ln_b=rwkv_ln_b[l], swa_q_norm=swa_q_norm[l], swa_k_norm=swa_k_norm[l],
                 swa_sinks=swa_sinks[l], ffn_w1=ffn_w1, ffn_w3=ffn_w3, ffn_w2=_cast_bf16(ffn_w2, l))
        vres = None if l == 0 else (rwkv_vres_down[l - 1], rwkv_vres_mu[l - 1], rwkv_vres_up[l - 1],
                                    rwkv_v0[l - 1])
        x, v_first = _layer(x, c8, l, v_first, vres, bias_ad, bias_as, bias_d, p, attn_tile)
    return x
```

```python
import functools
import math

import jax
import jax.numpy as jnp
from jax import lax
from jax.experimental import pallas as pl
from jax.experimental.pallas import tpu as pltpu

F32 = jnp.float32
BF16 = jnp.bfloat16
HI = lax.Precision.HIGHEST

D_MODEL = 4096
GROUP = 1024
D_FF = 11008
NORM_EPS = 1e-6
NEG_INF = -1e30
LANE = 128
HALF = 64

DIFF_HEADS = 8
DIFF_QK = 64
GLA_HEADS = 4
GLA_DK = 128
GLA_DV = 256
GLA_RANK = 16
GLA_NORMALIZER = 16.0
CHUNK = 64
SCAN_GROUP = 16
RWKV_GROUP = 8
RWKV_TILES = 2
RWKV_LN_EPS = 64e-5
SWA_HEADS = 16
SWA_KV_HEADS = 2
SWA_BLOCK = 128
REL_BUCKETS = 32
REL_MAX_DIST = 128

OFF_A = 0
OFF_B = 3072
OFF_C = 6144
OFF_D = 9216
OFF_S = 10496
SMALL = 256
NCOLS = OFF_S + SMALL
S_GLA, S_WD, S_AD, S_GD, S_VR = 0, 16, 80, 144, 208
TAIL = 512
OFF_T = NCOLS - TAIL

VMEM_LIMIT = 56 * 1024 * 1024


def _cparams(sem, vmem=VMEM_LIMIT):
    return pltpu.CompilerParams(dimension_semantics=sem, vmem_limit_bytes=vmem)


def _dot(a, b, prec=None):
    return jnp.dot(a, b, preferred_element_type=F32, precision=prec)


def _dot_nt(a, b, prec=None):
    return lax.dot_general(a, b, (((1,), (1,)), ((), ())), preferred_element_type=F32, precision=prec)


def _dot_tn(a, b, prec=None):
    return lax.dot_general(a, b, (((0,), (0,)), ((), ())), preferred_element_type=F32, precision=prec)


def _sigmoid(z):
    return 1.0 / (1.0 + jnp.exp(-z))


def _softplus(z):
    return jnp.maximum(z, 0.0) + jnp.log(1.0 + jnp.exp(-jnp.abs(z)))


def _seg_ones():
    r = lax.broadcasted_iota(jnp.int32, (LANE, LANE), 0) // HALF
    c = lax.broadcasted_iota(jnp.int32, (LANE, LANE), 1) // HALF
    return (r == c).astype(BF16)


def _split(x):
    hi = x.astype(BF16)
    return hi, (x - hi.astype(F32)).astype(BF16)


def _dot_left2(x, w):
    hi, lo = _split(x)
    return _dot(hi, w) + _dot(lo, w)


def _tri_cumsum(tri, x):
    hi, lo = _split(x)
    return _dot(tri, hi) + _dot(tri, lo)


def _segsum(x, ones):
    return _dot_left2(x, ones)


def _ada_body(c_ref, w_ref, b_ref, o_ref):
    c = c_ref[...]
    s = (c * _sigmoid(c)).astype(BF16)
    o_ref[...] = _dot(s, w_ref[...].astype(BF16)) + b_ref[...]


def _ada(c8, ada_w, ada_b, layer, tn=512):
    _, d, n = ada_w.shape
    return pl.pallas_call(
        _ada_body,
        out_shape=jax.ShapeDtypeStruct((8, n), F32),
        grid=(n // tn,),
        in_specs=[pl.BlockSpec((8, d), lambda j: (0, 0)),
                  pl.BlockSpec((None, d, tn), lambda j: (layer, 0, j)),
                  pl.BlockSpec((1, tn), lambda j: (0, j))],
        out_specs=pl.BlockSpec((8, tn), lambda j: (0, j)),
        compiler_params=_cparams(("parallel",)),
        name="ada_mod",
    )(c8, ada_w, ada_b[layer].reshape(1, n))


def _cast_body(w_ref, o_ref):
    o_ref[...] = w_ref[...].astype(o_ref.dtype)


def _cast_bf16(w, layer, tr=256):
    _, rows, cols = w.shape
    return pl.pallas_call(
        _cast_body,
        out_shape=jax.ShapeDtypeStruct((rows, cols), BF16),
        grid=(rows // tr,),
        in_specs=[pl.BlockSpec((None, tr, cols), lambda i: (layer, i, 0))],
        out_specs=pl.BlockSpec((tr, cols), lambda i: (i, 0)),
        compiler_params=_cparams(("parallel",)),
        name="weight_cast",
    )(w)


P_IN = 10704
SRC_B, SRC_C, SRC_D = 3072, 6160, 9424
SRC_BS, SRC_CS = SRC_B + 3072, SRC_C + 3072


PIECE = 16
PIECES = LANE // PIECE


def _piece_index(j, s):
    n_ab, n_c, n_d = OFF_C // LANE, OFF_D // LANE, OFF_S // LANE
    small0 = SRC_BS // PIECE if s == 0 else SRC_CS // PIECE + s - 1
    small1 = SRC_CS // PIECE + PIECES - 1 + min(s, 4)
    return jnp.where(j < n_ab, j * PIECES + s,
           jnp.where(j < n_c, SRC_C // PIECE + (j - n_ab) * PIECES + s,
           jnp.where(j < n_d, SRC_D // PIECE + (j - n_c) * PIECES + s,
           jnp.where(j == n_d, small0, small1))))


def _regroup_body(*refs):
    w_refs, vr_ref, o_ref = refs[:PIECES], refs[PIECES], refs[PIECES + 1]
    last = pl.program_id(0) == pl.num_programs(0) - 1
    pieces = [r[...] for r in w_refs]
    pieces[5] = jnp.where(last, vr_ref[:PIECE, :], pieces[5])
    pieces[6] = jnp.where(last, vr_ref[PIECE:, :], pieces[6])
    pieces[7] = jnp.where(last, 0.0, pieces[7])
    o_ref[...] = jnp.concatenate(pieces, axis=0).T.astype(o_ref.dtype)


def _regroup_w_in(w_in, layer, vres_down):
    _, d, _ = w_in.shape
    w_t = jnp.swapaxes(w_in, 1, 2)
    piece = lambda s: pl.BlockSpec((None, PIECE, d), lambda j: (layer, _piece_index(j, s), 0))
    return pl.pallas_call(
        _regroup_body,
        out_shape=jax.ShapeDtypeStruct((d, NCOLS), BF16),
        grid=(NCOLS // LANE,),
        in_specs=[piece(s) for s in range(PIECES)] + [pl.BlockSpec((2 * PIECE, d), lambda j: (0, 0))],
        out_specs=pl.BlockSpec((d, LANE), lambda j: (0, j)),
        compiler_params=_cparams(("parallel",)),
        name="w_in_regroup",
    )(*([w_t] * PIECES), vres_down.T)


def _norm_body(x_ref, sc_ref, sh_ref, o_ref):
    x = x_ref[0]
    ms = jnp.mean(x * x, axis=-1, keepdims=True)
    h = x * lax.rsqrt(ms + NORM_EPS) * (1.0 + sc_ref[0]) + sh_ref[0]
    o_ref[...] = h.astype(o_ref.dtype)


def _norm_mod(x, mod3, sc_idx, sh_idx, ts=256):
    bsz, seq, d = x.shape
    ts = min(ts, seq)
    ns = seq // ts
    return pl.pallas_call(
        _norm_body,
        out_shape=jax.ShapeDtypeStruct((bsz * seq, d), BF16),
        grid=(bsz, ns),
        in_specs=[pl.BlockSpec((1, ts, d), lambda b, i: (b, i, 0)),
                  pl.BlockSpec((1, 1, d), lambda b, i: (b * 6 + sc_idx, 0, 0)),
                  pl.BlockSpec((1, 1, d), lambda b, i: (b * 6 + sh_idx, 0, 0))],
        out_specs=pl.BlockSpec((ts, d), lambda b, i: (b * ns + i, 0)),
        compiler_params=_cparams(("parallel", "parallel")),
        name="norm_mod",
    )(x, mod3, mod3)


def _in_proj_body(a_ref, b_ref, main_ref, tail_ref):
    j = pl.program_id(1)
    acc = _dot(a_ref[...], b_ref[...])

    @pl.when(j < pl.num_programs(1) - 1)
    def _():
        main_ref[...] = acc.astype(main_ref.dtype)

    @pl.when(j == pl.num_programs(1) - 1)
    def _():
        tail_ref[...] = acc


def _in_proj(a, b, tm=2048):
    m, k = a.shape
    tm = min(tm, m)
    n_main = OFF_T // TAIL
    return pl.pallas_call(
        _in_proj_body,
        out_shape=[jax.ShapeDtypeStruct((m, OFF_T), BF16), jax.ShapeDtypeStruct((m, TAIL), F32)],
        grid=(m // tm, n_main + 1),
        in_specs=[pl.BlockSpec((tm, k), lambda i, j: (i, 0)),
                  pl.BlockSpec((k, TAIL), lambda i, j: (0, j))],
        out_specs=[pl.BlockSpec((tm, TAIL), lambda i, j: (i, jnp.minimum(j, n_main - 1))),
                   pl.BlockSpec((tm, TAIL), lambda i, j: (i, 0))],
        compiler_params=_cparams(("parallel", "arbitrary")),
        name="in_proj",
    )(a, b)


def _outproj_body(a0, a1, a2, a3, w_ref, x_ref, g_ref, o_ref):
    acc = _dot(a0[...], w_ref[0 * GROUP:1 * GROUP, :])
    acc += _dot(a1[...], w_ref[1 * GROUP:2 * GROUP, :])
    acc += _dot(a2[...], w_ref[2 * GROUP:3 * GROUP, :])
    acc += _dot(a3[...], w_ref[3 * GROUP:4 * GROUP, :])
    o_ref[0] = x_ref[0] + g_ref[0] * acc


def _out_proj(parts, w, x, mod3, gate_idx, tm=1024, tn=512):
    bsz, seq, d = x.shape
    tm = min(tm, seq)
    ns = seq // tm
    a_spec = pl.BlockSpec((tm, GROUP), lambda b, i, j: (b * ns + i, 0))
    return pl.pallas_call(
        _outproj_body,
        out_shape=jax.ShapeDtypeStruct((bsz, seq, d), F32),
        grid=(bsz, ns, d // tn),
        in_specs=[a_spec, a_spec, a_spec, a_spec,
                  pl.BlockSpec((4 * GROUP, tn), lambda b, i, j: (0, j)),
                  pl.BlockSpec((1, tm, tn), lambda b, i, j: (b, i, j)),
                  pl.BlockSpec((1, 1, tn), lambda b, i, j: (b * 6 + gate_idx, 0, j))],
        out_specs=pl.BlockSpec((1, tm, tn), lambda b, i, j: (b, i, j)),
        compiler_params=_cparams(("parallel", "parallel", "parallel")),
        name="out_proj",
    )(*parts, w, x, mod3)


def _ffn_up_body(a_ref, w1_ref, w3_ref, o_ref):
    a = a_ref[...]
    u = _dot(a, w1_ref[...].astype(BF16))
    v = _dot(a, w3_ref[...].astype(BF16))
    o_ref[...] = (u * _sigmoid(u) * v).astype(o_ref.dtype)


def _ffn_up(h, w1, w3, layer, tm=1024, tn=256):
    m, k = h.shape
    n = w1.shape[2]
    tm = min(tm, m)
    return pl.pallas_call(
        _ffn_up_body,
        out_shape=jax.ShapeDtypeStruct((m, n), BF16),
        grid=(m // tm, n // tn),
        in_specs=[pl.BlockSpec((tm, k), lambda i, j: (i, 0)),
                  pl.BlockSpec((None, k, tn), lambda i, j: (layer, 0, j)),
                  pl.BlockSpec((None, k, tn), lambda i, j: (layer, 0, j))],
        out_specs=pl.BlockSpec((tm, tn), lambda i, j: (i, j)),
        compiler_params=_cparams(("parallel", "parallel")),
        name="ffn_up",
    )(h, w1, w3)


def _ffn_down_body(a_ref, w_ref, x_ref, g_ref, o_ref):
    o_ref[0] = x_ref[0] + g_ref[0] * _dot(a_ref[...], w_ref[...])


def _ffn_down(a, w, x, mod3, gate_idx, tm=512, tn=512):
    bsz, seq, d = x.shape
    k = a.shape[1]
    tm = min(tm, seq)
    ns = seq // tm
    return pl.pallas_call(
        _ffn_down_body,
        out_shape=jax.ShapeDtypeStruct((bsz, seq, d), F32),
        grid=(bsz, ns, d // tn),
        in_specs=[pl.BlockSpec((tm, k), lambda b, i, j: (b * ns + i, 0)),
                  pl.BlockSpec((k, tn), lambda b, i, j: (0, j)),
                  pl.BlockSpec((1, tm, tn), lambda b, i, j: (b, i, j)),
                  pl.BlockSpec((1, 1, tn), lambda b, i, j: (b * 6 + gate_idx, 0, j))],
        out_specs=pl.BlockSpec((1, tm, tn), lambda b, i, j: (b, i, j)),
        compiler_params=_cparams(("parallel", "parallel", "parallel")),
        name="ffn_down",
    )(a, w, x, mod3)


def _bias_body(tab_ref, o_ref, *, off, window, shift, key_major, scale):
    h = pl.program_id(0)
    rows, cols = o_ref.shape[1], o_ref.shape[2]
    last = tab_ref[REL_BUCKETS - 1, h]
    max_exact = REL_BUCKETS // 2

    def block(d0):
        qi = lax.broadcasted_iota(jnp.int32, (LANE, LANE), 1 if key_major else 0)
        kj = lax.broadcasted_iota(jnp.int32, (LANE, LANE), 0 if key_major else 1)
        dist = d0 + qi - kj
        n = jnp.maximum(dist, 0)
        nf = jnp.maximum(n, 1).astype(F32)
        large = max_exact + (jnp.log(nf / max_exact) / math.log(REL_MAX_DIST / max_exact)
                             * (REL_BUCKETS - max_exact)).astype(jnp.int32)
        large = jnp.minimum(large, REL_BUCKETS - 1)
        bucket = jnp.where(n < max_exact, n, large)
        bias = jnp.full((LANE, LANE), last, F32)
        for b in range(REL_BUCKETS - 1):
            bias = jnp.where(bucket == b, tab_ref[b, h], bias)
        if shift:
            bias = bias - last
        if scale != 1.0:
            bias = bias * scale
        valid = dist >= 0
        if window is not None:
            valid = jnp.logical_and(valid, dist < window)
        return jnp.where(valid, bias, NEG_INF)

    cache = {}
    for br in range(rows // LANE):
        for bc in range(cols // LANE):
            qb, kb = (bc, br) if key_major else (br, bc)
            d0 = off + LANE * (qb - kb)
            if d0 not in cache:
                if d0 + LANE <= 0 or (window is not None and d0 - LANE >= window):
                    cache[d0] = jnp.full((LANE, LANE), NEG_INF, F32)
                elif d0 - LANE >= REL_MAX_DIST and (window is None or d0 + LANE <= window):
                    const = (0.0 if shift else last) * scale
                    cache[d0] = jnp.full((LANE, LANE), const, F32)
                else:
                    cache[d0] = block(d0)
            o_ref[0, br * LANE:(br + 1) * LANE, bc * LANE:(bc + 1) * LANE] = cache[d0]


def _bias_tiles(table, rows, cols, off, window, shift, key_major=False, scale=1.0):
    nh = table.shape[1]
    return pl.pallas_call(
        functools.partial(_bias_body, off=off, window=window, shift=shift, key_major=key_major, scale=scale),
        out_shape=jax.ShapeDtypeStruct((nh, rows, cols), F32),
        grid=(nh,),
        in_specs=[pl.BlockSpec(memory_space=pltpu.SMEM)],
        out_specs=pl.BlockSpec((1, rows, cols), lambda h: (h, 0, 0)),
        compiler_params=_cparams(("parallel",)),
        name="rel_bias_tiles",
    )(table)


LOG2E = 1.4426950408889634
VT_ROWS = LANE + 16
Q_CHUNK = 512


def _prep_a_body(c_ref, qg_ref, kg_ref, qk_ref, vt_ref):
    ones = _seg_ones()
    scale = DIFF_QK ** -0.5 * LOG2E
    tm = c_ref.shape[0]
    for j in range(2 * DIFF_HEADS):
        x = c_ref[:, j * LANE:(j + 1) * LANE].astype(F32)
        ms = _segsum(x * x, ones) * (1.0 / DIFF_QK)
        gain = qg_ref[...] * scale if j < DIFF_HEADS else kg_ref[...]
        qk_ref[:, j * LANE:(j + 1) * LANE] = (x * lax.rsqrt(ms + NORM_EPS) * gain).astype(qk_ref.dtype)
    for h in range(DIFF_HEADS):
        v = c_ref[:, 2 * GROUP + h * LANE:2 * GROUP + (h + 1) * LANE].astype(F32)
        vt_ref[h * VT_ROWS:h * VT_ROWS + LANE, :] = v.T.astype(vt_ref.dtype)
        vt_ref[h * VT_ROWS + LANE:(h + 1) * VT_ROWS, :] = jnp.ones((VT_ROWS - LANE, tm), vt_ref.dtype)


def _prep_a(cols, q_gain, k_gain, tm=256):
    t = cols.shape[0]
    tm = min(tm, t)
    qg = jnp.tile(q_gain, 2).reshape(1, LANE)
    kg = jnp.tile(k_gain, 2).reshape(1, LANE)
    vec = pl.BlockSpec((1, LANE), lambda i: (0, 0))
    return pl.pallas_call(
        _prep_a_body,
        out_shape=[jax.ShapeDtypeStruct((t, 2 * GROUP), BF16),
                   jax.ShapeDtypeStruct((DIFF_HEADS * VT_ROWS, t), BF16)],
        grid=(t // tm,),
        in_specs=[pl.BlockSpec((tm, 3 * GROUP), lambda i: (i, OFF_A // (3 * GROUP))), vec, vec],
        out_specs=[pl.BlockSpec((tm, 2 * GROUP), lambda i: (i, 0)),
                   pl.BlockSpec((DIFF_HEADS * VT_ROWS, tm), lambda i: (0, i))],
        compiler_params=_cparams(("parallel",)),
        name="diff_prep",
    )(cols, qg, kg)


def _flash_body(it_ref, jt_ref, q_ref, k_ref, vt_ref, vtp_ref, bd_ref, bs_ref, lam_ref, sub_ref, o_ref,
                qlo, qhi, m1, a1, m2, a2, p_scr, al_scr, *, lam_init):
    t = pl.program_id(2)
    i = it_ref[t]
    j = jt_ref[t]

    @pl.when(j == 0)
    def _():
        q = q_ref[...]
        lane = lax.broadcasted_iota(jnp.int32, q.shape, 1)
        zero = jnp.zeros_like(q)
        qlo[...] = jnp.where(lane < HALF, q, zero)
        qhi[...] = jnp.where(lane >= HALF, q, zero)
        for m, a in ((m1, a1), (m2, a2)):
            m[...] = jnp.full(m.shape, NEG_INF, F32)
            a[...] = jnp.zeros(a.shape, F32)
        al_scr[...] = jnp.ones(al_scr.shape, F32)

    @pl.when(t == 0)
    def _():
        p_scr[...] = jnp.zeros(p_scr.shape, BF16)

    pending = (j > 0).astype(F32)

    def step(bias, last):
        k = k_ref[...]
        vt = vt_ref[...]
        vtp = vtp_ref[...]
        nq = q_ref.shape[0]
        qc = min(Q_CHUNK, nq)
        chains = [(mi, qq, m, a, slice(c * qc, (c + 1) * qc))
                  for mi, (qq, m, a) in enumerate(((qlo, m1, a1), (qhi, m2, a2))) for c in range(nq // qc)]
        ahead = 2
        kend = [qs.stop if last else k.shape[0] for _, _, _, _, qs in chains]
        if bias is None:
            scores = [_dot_nt(k, qq[qs, :]) for _, qq, _, _, qs in chains[:ahead]]
        elif last:
            scores = [_dot_nt(k[:ke, :], c[1][c[4], :]) + bias[:ke, c[4]] for c, ke in zip(chains, kend)]
        else:
            full = [_dot_nt(k, qq[...]) for qq in (qlo, qhi)]
            scores = [full[c[0]][:, c[4]] + bias[:, c[4]] for c in chains]
        for idx, (mi, _, m, a, qs) in enumerate(chains):
            if bias is None and idx + ahead < len(chains):
                _, qq_n, _, _, qs_n = chains[idx + ahead]
                scores.append(_dot_nt(k, qq_n[qs_n, :]))
            acc = al_scr[mi, :, qs] * a[:, qs] + pending * _dot(vtp, p_scr[mi, :, qs])
            s = scores[idx]
            m_old = m[:, qs]
            m_new = jnp.maximum(m_old, jnp.max(s, axis=0, keepdims=True))
            p = jnp.exp2(s - m_new).astype(BF16)
            alpha = jnp.exp2(m_old - m_new)
            if last:
                acc = alpha * acc + _dot(vt[:, :kend[idx]], p)
            else:
                p_scr[mi, :, qs] = p
                al_scr[mi, :, qs] = alpha
            a[:, qs] = acc
            m[:, qs] = m_new

    @pl.when(j < i - 1)
    def _():
        step(None, False)

    @pl.when(j == i - 1)
    def _():
        step(bs_ref[0], False)

    @pl.when(j == i)
    def _():
        step(bd_ref[0], True)
        lam = lam_ref[...]
        e1 = jnp.exp(jnp.sum(lam[0:1] * lam[1:2], axis=-1, keepdims=True))
        e2 = jnp.exp(jnp.sum(lam[2:3] * lam[3:4], axis=-1, keepdims=True))
        lam_full = e1 - e2 + lam_init
        o = a1[:LANE, :] / a1[LANE:LANE + 1, :] - lam_full * (a2[:LANE, :] / a2[LANE:LANE + 1, :])
        ms = jnp.mean(o * o, axis=0, keepdims=True)
        o = o * lax.rsqrt(ms + NORM_EPS) * sub_ref[...] * (1.0 - lam_init)
        o_ref[...] = o.T.astype(o_ref.dtype)


def _diff_attention(qk, vt, bias_diag, bias_sub, lam, subln, bsz, seq, layer_idx, tile):
    nq = seq // tile
    lam_init = 0.8 - 0.6 * math.exp(-0.3 * layer_idx)
    pairs = [(i, j) for i in range(nq) for j in range(i + 1)]
    i_tab = jnp.array([ij[0] for ij in pairs], jnp.int32)
    j_tab = jnp.array([ij[1] for ij in pairs], jnp.int32)
    grid_spec = pltpu.PrefetchScalarGridSpec(
        num_scalar_prefetch=2,
        grid=(bsz, DIFF_HEADS, len(pairs)),
        in_specs=[pl.BlockSpec((tile, LANE), lambda b, h, t, it, jt: (b * nq + it[t], h)),
                  pl.BlockSpec((tile, LANE), lambda b, h, t, it, jt: (b * nq + jt[t], DIFF_HEADS + h)),
                  pl.BlockSpec((VT_ROWS, tile), lambda b, h, t, it, jt: (h, b * nq + jt[t])),
                  pl.BlockSpec((VT_ROWS, tile), lambda b, h, t, it, jt: (h, b * nq + jnp.maximum(jt[t] - 1, 0))),
                  pl.BlockSpec((1, tile, tile), lambda b, h, t, it, jt: (h, 0, 0)),
                  pl.BlockSpec((1, tile, tile), lambda b, h, t, it, jt: (h, 0, 0)),
                  pl.BlockSpec((4, DIFF_QK), lambda b, h, t, it, jt: (0, 0)),
                  pl.BlockSpec((LANE, 1), lambda b, h, t, it, jt: (0, 0))],
        out_specs=pl.BlockSpec((tile, LANE), lambda b, h, t, it, jt: (b * nq + it[t], h)),
        scratch_shapes=[pltpu.VMEM((tile, LANE), BF16), pltpu.VMEM((tile, LANE), BF16),
                        pltpu.VMEM((1, tile), F32), pltpu.VMEM((VT_ROWS, tile), F32),
                        pltpu.VMEM((1, tile), F32), pltpu.VMEM((VT_ROWS, tile), F32),
                        pltpu.VMEM((2, tile, tile), BF16), pltpu.VMEM((2, 1, tile), F32)])
    return pl.pallas_call(
        functools.partial(_flash_body, lam_init=lam_init),
        out_shape=jax.ShapeDtypeStruct((bsz * seq, GROUP), BF16),
        grid_spec=grid_spec,
        compiler_params=_cparams(("parallel", "parallel", "arbitrary")),
        name="diff_attention",
    )(i_tab, j_tab, qk, qk, vt, vt, bias_diag, bias_sub, lam, subln.reshape(LANE, 1))


def _swa_body(q_ref, kp_ref, kc_ref, vp_ref, vc_ref, bias_ref, qg_ref, kg_ref, sink_ref, o_ref):
    n = pl.program_id(1)
    g = pl.program_id(2)
    ones = _seg_ones()
    lane = lax.broadcasted_iota(jnp.int32, (SWA_BLOCK, LANE), 1)
    lo = lane < HALF

    def norm(x, gain):
        ms = _segsum(x * x, ones) * (1.0 / HALF)
        return x * lax.rsqrt(ms + NORM_EPS) * gain

    k = jnp.concatenate([norm(kp_ref[...], kg_ref[...]), norm(kc_ref[...], kg_ref[...])], axis=0)
    v = jnp.concatenate([vp_ref[...], vc_ref[...]], axis=0)
    k_sw = pltpu.roll(k, HALF, axis=1)
    v_sw = pltpu.roll(v, HALF, axis=1)
    first_head = g == 0
    lo2 = lax.broadcasted_iota(jnp.int32, (2 * SWA_BLOCK, LANE), 1) < HALF
    k_a = jnp.where(first_head, k, k_sw).astype(BF16)
    k_b = jnp.where(first_head, k_sw, k).astype(BF16)
    v_mine_lo = jnp.where(first_head, v, v_sw)
    v_mine_hi = jnp.where(first_head, v_sw, v)
    v_a = jnp.where(lo2, v_mine_lo, 0.0).astype(BF16)
    v_b = jnp.where(lo2, 0.0, v_mine_hi).astype(BF16)
    kcol = lax.broadcasted_iota(jnp.int32, (SWA_BLOCK, 2 * SWA_BLOCK), 1)
    pad = jnp.logical_and(n == 0, kcol < SWA_BLOCK)
    scale = HALF ** -0.5
    grp = SWA_HEADS // SWA_KV_HEADS
    heads = range(grp)
    q = [norm(q_ref[:, pr * LANE:(pr + 1) * LANE].astype(F32), qg_ref[...] * scale) for pr in range(grp // 2)]
    qm = [jnp.where(lo if hh % 2 == 0 else jnp.logical_not(lo), q[hh // 2], 0.0).astype(BF16) for hh in heads]
    s = [_dot_nt(qm[hh], k_a if hh % 2 == 0 else k_b) for hh in heads]
    s = [jnp.where(pad, NEG_INF, s[hh] + bias_ref[hh]) for hh in heads]
    sink = [sink_ref[g * grp + hh] for hh in heads]
    m = [jnp.maximum(jnp.max(s[hh], axis=-1, keepdims=True), sink[hh]) for hh in heads]
    p = [jnp.exp(s[hh] - m[hh]) for hh in heads]
    den = [jnp.sum(p[hh], axis=-1, keepdims=True) + jnp.exp(sink[hh] - m[hh]) for hh in heads]
    pv = [_dot(p[hh].astype(BF16), v_a if hh % 2 == 0 else v_b) for hh in heads]
    for pr in range(grp // 2):
        o = pv[2 * pr] / den[2 * pr] + pv[2 * pr + 1] / den[2 * pr + 1]
        o_ref[:, pr * LANE:(pr + 1) * LANE] = o.astype(o_ref.dtype)


def _swa_attention(cols, tail, bias, q_gain, k_gain, sinks, bsz, seq):
    nb = seq // SWA_BLOCK
    grp = SWA_HEADS // SWA_KV_HEADS
    qw = grp * HALF
    kcol = (OFF_D + GROUP - OFF_T) // LANE
    vcol = kcol + 1
    prev = lambda b, n, g: b * nb + jnp.maximum(n - 1, 0)
    cur = lambda b, n, g: b * nb + n
    vec = pl.BlockSpec((1, LANE), lambda b, n, g: (0, 0))
    return pl.pallas_call(
        _swa_body,
        out_shape=jax.ShapeDtypeStruct((bsz * seq, GROUP), BF16),
        grid=(bsz, nb, SWA_KV_HEADS),
        in_specs=[pl.BlockSpec((SWA_BLOCK, qw), lambda b, n, g: (cur(b, n, g), OFF_D // qw + g)),
                  pl.BlockSpec((SWA_BLOCK, LANE), lambda b, n, g: (prev(b, n, g), kcol)),
                  pl.BlockSpec((SWA_BLOCK, LANE), lambda b, n, g: (cur(b, n, g), kcol)),
                  pl.BlockSpec((SWA_BLOCK, LANE), lambda b, n, g: (prev(b, n, g), vcol)),
                  pl.BlockSpec((SWA_BLOCK, LANE), lambda b, n, g: (cur(b, n, g), vcol)),
                  pl.BlockSpec((grp, SWA_BLOCK, 2 * SWA_BLOCK), lambda b, n, g: (g, 0, 0)),
                  vec, vec,
                  pl.BlockSpec(memory_space=pltpu.SMEM)],
        out_specs=pl.BlockSpec((SWA_BLOCK, qw), lambda b, n, g: (cur(b, n, g), g)),
        compiler_params=_cparams(("parallel", "parallel", "parallel")),
        name="swa_attention",
    )(cols, tail, tail, tail, tail, bias,
      jnp.tile(q_gain, 2).reshape(1, LANE), jnp.tile(k_gain, 2).reshape(1, LANE), sinks)


def _prep_c_body(*refs, seq, with_vres):
    (cm_ref, cs_ref, pm_ref, ps_ref, mum_ref, mus_ref, wup_ref, aup_ref, gup_ref, vup_ref, glaup_ref,
     w0_ref, a0_ref, v0_ref, kk_ref, ka_ref, glab_ref) = refs[:17]
    rest = refs[17:]
    if with_vres:
        vf_ref, rest = rest[0], rest[1:]
    r_ref, lw_ref, k_ref, v_ref, nk_ref, kb_ref, g_ref, la_ref = rest
    tm = cm_ref.shape[0]
    i = pl.program_id(0)
    seq_start = (i * tm) % seq == 0

    def shifted(cur, prev_rows, mu):
        row = lax.broadcasted_iota(jnp.int32, cur.shape, 0)
        before = jnp.where(seq_start, 0.0, prev_rows[PREV_ROWS - 1:PREV_ROWS, :].astype(F32))
        prev = jnp.where(row == 0, before, pltpu.roll(cur, 1, axis=0))
        return cur + (prev - cur) * mu

    def low_rank(x, w2_ref):
        hi, lo = _split(x)
        return _dot(hi, w2_ref[0]) + _dot(lo, w2_ref[0]) + _dot(hi, w2_ref[1])

    cs = cs_ref[...]
    la_ref[...] = -_softplus(-(low_rank(cs, glaup_ref) + glab_ref[...])) * (1.0 / GLA_NORMALIZER)
    sm = shifted(cm_ref[...].astype(F32), pm_ref, mum_ref[...])
    ss = shifted(cs, ps_ref, mus_ref[...])
    r = sm[:, :GROUP]
    k = sm[:, GROUP:2 * GROUP]
    v = sm[:, 2 * GROUP:]
    if with_vres:
        gate = _sigmoid(v0_ref[...] + low_rank(ss, vup_ref))
        v = v + (vf_ref[...] - v) * gate
    w_log = -_softplus(-(w0_ref[...] + low_rank(jnp.tanh(ss), wup_ref))) - 0.5
    a = _sigmoid(a0_ref[...] + low_rank(ss, aup_ref))
    r_ref[...] = r.astype(r_ref.dtype)
    lw_ref[...] = -jnp.exp(w_log)
    v_ref[...] = v
    g_ref[...] = low_rank(_sigmoid(ss), gup_ref).astype(g_ref.dtype)
    k_ref[...] = (k * (1.0 + (a - 1.0) * ka_ref[...])).astype(k_ref.dtype)
    ones = _seg_ones()
    kk = k * kk_ref[...]
    for j in range(GROUP // LANE):
        sl = slice(j * LANE, (j + 1) * LANE)
        x = kk[:, sl]
        nrm = jnp.maximum(jnp.sqrt(_segsum(x * x, ones)), 1e-12)
        x = x / nrm
        nk_ref[:, sl] = x.astype(nk_ref.dtype)
        kb_ref[:, sl] = (x * a[:, sl]).astype(kb_ref.dtype)


PREV_ROWS = 16


def _prep_c(cols, tail, seq, mu_main, mu_small, w_up, a_up, g_up, v_up, gla_up, w0, a0, v0, k_k, k_a, gla_bias,
            v_first, tm=256):
    t = cols.shape[0]
    tm = min(tm, seq)
    with_vres = v_first is not None
    cmain = OFF_C // (3 * GROUP)
    csmall = (OFF_S - OFF_T) // SMALL
    prev = lambda i: jnp.maximum(i * (tm // PREV_ROWS) - 1, 0)
    full = lambda r, c: pl.BlockSpec((r, c), lambda i: (0, 0))
    hilo = lambda c: pl.BlockSpec((2, SMALL, c), lambda i: (0, 0, 0))
    in_specs = [pl.BlockSpec((tm, 3 * GROUP), lambda i: (i, cmain)),
                pl.BlockSpec((tm, SMALL), lambda i: (i, csmall)),
                pl.BlockSpec((PREV_ROWS, 3 * GROUP), lambda i: (prev(i), cmain)),
                pl.BlockSpec((PREV_ROWS, SMALL), lambda i: (prev(i), csmall)),
                full(1, 3 * GROUP), full(1, SMALL),
                hilo(GROUP), hilo(GROUP), hilo(GROUP), hilo(GROUP), hilo(GLA_HEADS * GLA_DK),
                full(1, GROUP), full(1, GROUP), full(1, GROUP), full(1, GROUP), full(1, GROUP),
                full(1, GLA_HEADS * GLA_DK)]
    args = [cols, tail, cols, tail, mu_main, mu_small, w_up, a_up, g_up, v_up, gla_up,
            w0, a0, v0, k_k, k_a, gla_bias]
    if with_vres:
        in_specs.append(pl.BlockSpec((tm, GROUP), lambda i: (i, 0)))
        args.append(v_first)
    row = pl.BlockSpec((tm, GROUP), lambda i: (i, 0))
    out = [jax.ShapeDtypeStruct((t, GROUP), dt) for dt in (BF16, F32, BF16, F32, BF16, BF16, BF16)]
    return pl.pallas_call(
        functools.partial(_prep_c_body, seq=seq, with_vres=with_vres),
        out_shape=out + [jax.ShapeDtypeStruct((t, GLA_HEADS * GLA_DK), F32)],
        grid=(t // tm,),
        in_specs=in_specs,
        out_specs=[row] * 7 + [pl.BlockSpec((tm, GLA_HEADS * GLA_DK), lambda i: (i, 0))],
        compiler_params=_cparams(("parallel",)),
        name="rwkv_gla_prep",
    )(*args)


def _gla_body(q_ref, k_ref, v_ref, g_ref, la_ref, gain_ref, o_ref, st_ref):
    @pl.when(pl.program_id(2) == 0)
    def _():
        st_ref[...] = jnp.zeros(st_ref.shape, F32)

    row = lax.broadcasted_iota(jnp.int32, (CHUNK, CHUNK), 0)
    col = lax.broadcasted_iota(jnp.int32, (CHUNK, CHUNK), 1)
    causal = row >= col
    tri = causal.astype(BF16)
    nchunk = q_ref.shape[0] // CHUNK

    def local(sls):
        n = range(len(sls))
        k = [k_ref[sl, :].astype(F32) for sl in sls]
        v = [v_ref[sl, :].astype(BF16) for sl in sls]
        b = [_tri_cumsum(tri, la_ref[sl, :]) for sl in sls]
        b_last = [b[i][CHUNK - 1:CHUNK, :] for i in n]
        q_dec = [(q_ref[sl, :].astype(F32) * (GLA_DK ** -0.5) * jnp.exp(b[i])).astype(BF16)
                 for i, sl in enumerate(sls)]
        a_intra = [jnp.where(causal, _dot_nt(q_dec[i], (k[i] * jnp.exp(-b[i])).astype(BF16)), 0.0) for i in n]
        upd = [_dot_tn(v[i], (k[i] * jnp.exp(b_last[i] - b[i])).astype(BF16)) for i in n]
        o_intra = [_dot(a_intra[i].astype(BF16), v[i]) for i in n]
        return [(q_dec[i], o_intra[i], upd[i], jnp.exp(b_last[i])) for i in n]

    def advance(sl, q_dec, o_intra, upd, dec):
        state = st_ref[...]
        s_hi, s_lo = _split(state)
        o = o_intra + _dot_nt(q_dec, s_hi) + _dot_nt(q_dec, s_lo)
        st_ref[...] = state * dec + upd
        ms = jnp.mean(o * o, axis=-1, keepdims=True)
        o = o * lax.rsqrt(ms + NORM_EPS) * gain_ref[...]
        gate = g_ref[sl, :].astype(F32)
        o_ref[sl, :] = (o * (gate * _sigmoid(gate))).astype(o_ref.dtype)

    gsz = min(SCAN_GROUP, nchunk)

    def group(gi, carry):
        sls = [pl.ds(pl.multiple_of((gi * gsz + g) * CHUNK, CHUNK), CHUNK) for g in range(gsz)]
        for sl, part in zip(sls, local(sls)):
            advance(sl, *part)
        return carry

    lax.fori_loop(0, nchunk // gsz, group, 0)


def _gla(cols, log_a, out_gain, bsz, seq, tc=1024):
    tc = min(tc, seq)
    ns = seq // tc
    qc = OFF_B // GLA_DK
    kc = qc + GLA_HEADS
    vc = (OFF_B + 2 * GLA_HEADS * GLA_DK) // GLA_DV
    gc = vc + GLA_HEADS
    rows = lambda b, h, i: b * ns + i
    return pl.pallas_call(
        _gla_body,
        out_shape=jax.ShapeDtypeStruct((bsz * seq, GROUP), BF16),
        grid=(bsz, GLA_HEADS, ns),
        in_specs=[pl.BlockSpec((tc, GLA_DK), lambda b, h, i: (rows(b, h, i), qc + h)),
                  pl.BlockSpec((tc, GLA_DK), lambda b, h, i: (rows(b, h, i), kc + h)),
                  pl.BlockSpec((tc, GLA_DV), lambda b, h, i: (rows(b, h, i), vc + h)),
                  pl.BlockSpec((tc, GLA_DV), lambda b, h, i: (rows(b, h, i), gc + h)),
                  pl.BlockSpec((tc, GLA_DK), lambda b, h, i: (rows(b, h, i), h)),
                  pl.BlockSpec((1, GLA_DV), lambda b, h, i: (0, 0))],
        out_specs=pl.BlockSpec((tc, GLA_DV), lambda b, h, i: (rows(b, h, i), h)),
        scratch_shapes=[pltpu.VMEM((GLA_DV, GLA_DK), F32)],
        compiler_params=_cparams(("parallel", "parallel", "arbitrary")),
        name="gla_scan",
    )(cols, cols, cols, cols, log_a, out_gain.reshape(1, GLA_DV))


def _rwkv_body(r_ref, lw_ref, k_ref, v_ref, nk_ref, kb_ref, g_ref, rk_ref, lnw_ref, lnb_ref, o_ref, st_ref):
    @pl.when(pl.program_id(2) == 0)
    def _():
        st_ref[...] = jnp.zeros(st_ref.shape, F32)

    two = 2 * CHUNK
    row = lax.broadcasted_iota(jnp.int32, (two, two), 0)
    col = lax.broadcasted_iota(jnp.int32, (two, two), 1)
    same = (row // CHUNK) == (col // CHUNK)
    strict = jnp.logical_and(same, (row % CHUNK) > (col % CHUNK))
    incl = jnp.logical_and(same, (row % CHUNK) >= (col % CHUNK))
    eye = (row == col).astype(F32)
    crow = lax.broadcasted_iota(jnp.int32, (CHUNK, CHUNK), 0)
    ccol = lax.broadcasted_iota(jnp.int32, (CHUNK, CHUNK), 1)
    tri = (crow >= ccol).astype(BF16)
    lo = lax.broadcasted_iota(jnp.int32, (CHUNK, LANE), 1) < HALF
    ones = _seg_ones()
    nchunk = r_ref.shape[0] // CHUNK

    def stack(x):
        return jnp.concatenate([jnp.where(lo, x, 0.0), jnp.where(lo, 0.0, x)], axis=0)

    def fold(x):
        return x[:CHUNK, :] + x[CHUNK:, :]

    def transitions(items):
        n = range(len(items))
        r = [r_ref[sl, ls].astype(F32) for sl, ls in items]
        lw = [lw_ref[sl, ls] for sl, ls in items]
        k = [k_ref[sl, ls].astype(F32) for sl, ls in items]
        v = [v_ref[sl, ls] for sl, ls in items]
        cum = [_tri_cumsum(tri, lw[i]) for i in n]
        last = [cum[i][CHUNK - 1:CHUNK, :] for i in n]
        a_bf, r_st, bk_st, bkh_st, v_st = [], [], [], [], []
        for i, (sl, ls) in enumerate(items):
            kb = kb_ref[sl, ls].astype(F32)
            e_neg = jnp.exp(-cum[i])
            e_rem = jnp.exp(last[i] - cum[i])
            a_bf.append(stack(-nk_ref[sl, ls].astype(F32) * jnp.exp(cum[i] - lw[i])).astype(BF16))
            r_st.append(stack(r[i] * jnp.exp(cum[i])))
            bk_st.append(jnp.concatenate([stack(kb * e_neg), stack(k[i] * e_neg)], axis=0).astype(BF16))
            bkh_st.append(jnp.concatenate([stack(kb * e_rem), stack(k[i] * e_rem)], axis=0).astype(BF16))
            v_st.append(stack(v[i]).astype(BF16))
        sc = [_dot_nt(jnp.concatenate([a_bf[i], r_st[i].astype(BF16)], axis=0), bk_st[i]) for i in n]
        a_ab = [jnp.where(strict, sc[i][:two, :two], 0.0) for i in n]
        a_ak = [jnp.where(strict, sc[i][:two, two:], 0.0).astype(BF16) for i in n]
        r_b = [jnp.where(incl, sc[i][two:, :two], 0.0).astype(BF16) for i in n]
        r_k = [jnp.where(incl, sc[i][two:, two:], 0.0).astype(BF16) for i in n]
        akv = [_dot(a_ak[i], v_st[i]).astype(BF16) for i in n]
        rkv = [_dot(r_k[i], v_st[i]) for i in n]
        inv = [eye + a_ab[i] for i in n]
        pw = [a_ab[i].astype(BF16) for i in n]
        for _ in range(5):
            pw = [_dot(pw[i], pw[i]).astype(BF16) for i in n]
            inv = [inv[i] + _dot(inv[i].astype(BF16), pw[i]) for i in n]
        pu = [_dot(inv[i].astype(BF16), jnp.concatenate([a_bf[i], akv[i]], axis=1)).astype(BF16) for i in n]
        rb_pu = [_dot(r_b[i], pu[i]) for i in n]
        m = [_dot_tn(pu[i][:, :LANE], bkh_st[i][:two, :]).astype(BF16) for i in n]
        n0 = [_dot_tn(jnp.concatenate([pu[i][:, LANE:], v_st[i]], axis=0), bkh_st[i]) for i in n]
        rkr = [_segsum(r[i] * k[i] * rk_ref[:, items[i][1]], ones) for i in n]
        return [(fold(r_st[i] + rb_pu[i][:, :LANE]).astype(BF16), fold(rb_pu[i][:, LANE:] + rkv[i]),
                 m[i], n0[i], jnp.exp(last[i]), rkr[i] * v[i]) for i in n]

    def advance(tile, sl, ls, p2, y0, m, n0, dec, bonus):
        state = st_ref[tile]
        s_hi, s_lo = _split(state)
        y = _dot_nt(p2, s_hi) + y0
        st_ref[tile] = state * dec + _dot(s_hi, m) + _dot(s_lo, m) + n0
        d = y - _segsum(y, ones) * (1.0 / HALF)
        var = _segsum(d * d, ones) * (1.0 / HALF)
        y = d * lax.rsqrt(var + RWKV_LN_EPS) * lnw_ref[:, ls] + lnb_ref[:, ls]
        o_ref[sl, ls] = ((y + bonus) * g_ref[sl, ls]).astype(o_ref.dtype)

    gsz = min(RWKV_GROUP, nchunk)

    def group(gi, carry):
        sls = [pl.ds(pl.multiple_of((gi * gsz + g) * CHUNK, CHUNK), CHUNK) for g in range(gsz)]
        tiles = range(r_ref.shape[1] // LANE)
        items = [(sl, slice(q * LANE, (q + 1) * LANE)) for sl in sls for q in tiles]
        for idx, ((sl, ls), part) in enumerate(zip(items, transitions(items))):
            advance(idx % len(tiles), sl, ls, *part)
        return carry

    lax.fori_loop(0, nchunk // gsz, group, 0)


def _rwkv_scan(r, lw, k, v, nk, kb, g, r_k, ln_w, ln_b, bsz, seq, tc=512):
    tc = min(tc, seq)
    ns = seq // tc
    width = RWKV_TILES * LANE
    npair = GROUP // width
    blk = pl.BlockSpec((tc, width), lambda b, p, i: (b * ns + i, p))
    vec = pl.BlockSpec((1, width), lambda b, p, i: (0, p))
    return pl.pallas_call(
        _rwkv_body,
        out_shape=jax.ShapeDtypeStruct((bsz * seq, GROUP), BF16),
        grid=(bsz, npair, ns),
        in_specs=[blk] * 7 + [vec] * 3,
        out_specs=blk,
        scratch_shapes=[pltpu.VMEM((RWKV_TILES, LANE, LANE), F32)],
        compiler_params=_cparams(("parallel", "parallel", "arbitrary")),
        name="rwkv_scan",
    )(r, lw, k, v, nk, kb, g, r_k.reshape(1, GROUP), ln_w.reshape(1, GROUP), ln_b.reshape(1, GROUP))


def _pad_rows(w, start):
    full = jnp.zeros((SMALL, w.shape[1]), F32).at[start:start + w.shape[0]].set(w)
    hi = full.astype(BF16)
    return jnp.stack([hi, (full - hi.astype(F32)).astype(BF16)])


def _layer(x, c8, layer_idx, v_first, vres, bias_ad, bias_as, bias_d, p, attn_tile):
    bsz, seq, d = x.shape
    t = bsz * seq
    mod = _ada(c8, p["ada_w"], p["ada_b"], layer_idx)
    mod3 = mod[:bsz].reshape(bsz * 6, 1, d)
    h = _norm_mod(x, mod3, 1, 0)

    vres_cols = vres[0] if vres is not None else jnp.zeros((d, 32), F32)
    w_r = _regroup_w_in(p["w_in"], layer_idx, vres_cols)
    cols, tail = _in_proj(h, w_r)

    qk, vt = _prep_a(cols, p["diff_q_norm"], p["diff_k_norm"])
    o_a = _diff_attention(qk, vt, bias_ad, bias_as, p["diff_lambda"], p["diff_subln"], bsz, seq, layer_idx,
                          attn_tile)
    o_dd = _swa_attention(cols, tail, bias_d, p["swa_q_norm"], p["swa_k_norm"], p["swa_sinks"], bsz, seq)
    mu = p["rwkv_mu"]
    vres_mu = vres[1] if vres is not None else jnp.zeros((32,), F32)
    mu_small = jnp.concatenate([jnp.zeros((S_WD,), F32), mu[3 * GROUP:], vres_mu,
                                jnp.zeros((SMALL - S_VR - 32,), F32)]).reshape(1, SMALL)
    v_up = _pad_rows(vres[2], S_VR) if vres is not None else jnp.zeros((2, SMALL, GROUP), BF16)
    v0 = vres[3] if vres is not None else jnp.zeros((GROUP,), F32)
    r_, lw_, k_, v_, nk_, kb_, g_, la_ = _prep_c(
        cols, tail, seq, mu[:3 * GROUP].reshape(1, -1), mu_small,
        _pad_rows(p["rwkv_w_up"], S_WD), _pad_rows(p["rwkv_a_up"], S_AD), _pad_rows(p["rwkv_g_up"], S_GD),
        v_up, _pad_rows(p["gla_gate_up"], S_GLA),
        p["rwkv_w0"].reshape(1, -1), p["rwkv_a0"].reshape(1, -1), v0.reshape(1, -1),
        p["rwkv_k_k"].reshape(1, -1), p["rwkv_k_a"].reshape(1, -1), p["gla_gate_bias"].reshape(1, -1),
        v_first if vres is not None else None)
    if vres is None:
        v_first = v_
    o_bb = _gla(cols, la_, p["gla_out_norm"], bsz, seq)
    o_c = _rwkv_scan(r_, lw_, k_, v_, nk_, kb_, g_, p["rwkv_r_k"].reshape(-1), p["rwkv_ln_w"], p["rwkv_ln_b"],
                     bsz, seq)

    x = _out_proj([o_a, o_bb, o_c, o_dd], p["w_out"], x, mod3, 2)
    h2 = _norm_mod(x, mod3, 4, 3)
    act = _ffn_up(h2, p["ffn_w1"], p["ffn_w3"], layer_idx)
    x = _ffn_down(act, p["ffn_w2"], x, mod3, 5)
    return x, v_first


def kernel(x, c, rel_bias, ada_w, ada_b, w_in, w_out, diff_q_norm, diff_k_norm, diff_lambda, diff_subln,
           gla_gate_up, gla_gate_bias, gla_out_norm, rwkv_mu, rwkv_w_up, rwkv_w0, rwkv_a_up, rwkv_a0,
           rwkv_g_up, rwkv_k_k, rwkv_k_a, rwkv_r_k, rwkv_ln_w, rwkv_ln_b, rwkv_vres_down, rwkv_vres_mu,
           rwkv_vres_up, rwkv_v0, swa_q_norm, swa_k_norm, swa_sinks, ffn_w1, ffn_w3, ffn_w2):
    bsz, seq, _ = x.shape
    depth = ada_w.shape[0]
    attn_tile = min(1024, seq)
    bias_ad = _bias_tiles(rel_bias[:, :DIFF_HEADS], attn_tile, attn_tile, 0, None, True, True, LOG2E)
    bias_as = _bias_tiles(rel_bias[:, :DIFF_HEADS], attn_tile, attn_tile, attn_tile, None, True, True, LOG2E)
    bias_d = _bias_tiles(rel_bias[:, DIFF_HEADS:], SWA_BLOCK, 2 * SWA_BLOCK, SWA_BLOCK, SWA_BLOCK, False)
    c8 = jnp.zeros((8, c.shape[1]), F32).at[:bsz].set(c)
    v_first = None
    for l in range(depth):
        p = dict(ada_w=ada_w, ada_b=ada_b, w_in=w_in, w_out=_cast_bf16(w_out, l),
                 diff_q_norm=diff_q_norm[l], diff_k_norm=diff_k_norm[l], diff_lambda=diff_lambda[l],
                 diff_subln=diff_subln[l], gla_gate_up=gla_gate_up[l], gla_gate_bias=gla_gate_bias[l],
                 gla_out_norm=gla_out_norm[l], rwkv_mu=rwkv_mu[l], rwkv_w_up=rwkv_w_up[l],
                 rwkv_w0=rwkv_w0[l], rwkv_a_up=rwkv_a_up[l], rwkv_a0=rwkv_a0[l], rwkv_g_up=rwkv_g_up[l],
                 rwkv_k_k=rwkv_k_k[l], rwkv_k_a=rwkv_k_a[l], rwkv_r_k=rwkv_r_k[l], rwkv_ln_w=rwkv_ln_w[l],
                 rwkv_ln_b=rwkv_ln_b[l], swa_q_norm=swa_q_norm[l], swa_k_norm=swa_k_norm[l],
                 swa_sinks=swa_sinks[l], ffn_w1=ffn_w1, ffn_w3=ffn_w3, ffn_w2=_cast_bf16(ffn_w2, l))
        vres = None if l == 0 else (rwkv_vres_down[l - 1], rwkv_vres_mu[l - 1], rwkv_vres_up[l - 1],
                                    rwkv_v0[l - 1])
        x, v_first = _layer(x, c8, l, v_first, vres, bias_ad, bias_as, bias_d, p, attn_tile)
    return x
```

```python
import functools
import math

import jax
import jax.numpy as jnp
from jax import lax
from jax.experimental import pallas as pl
from jax.experimental.pallas import tpu as pltpu

F32 = jnp.float32
BF16 = jnp.bfloat16

GROUP = 1024
NORM_EPS = 1e-6
NEG_INF = -1e30
LANE = 128
HALF = 64

DIFF_HEADS = 8
DIFF_QK = 64
GLA_HEADS = 4
GLA_DK = 128
GLA_DV = 256
GLA_NORMALIZER = 16.0
CHUNK = 64
SCAN_GROUP = 16
RWKV_GROUP = 8
RWKV_TILES = 2
PREV_ROWS = 16
RWKV_LN_EPS = 64e-5
SWA_HEADS = 16
SWA_KV_HEADS = 2
SWA_BLOCK = 128
REL_BUCKETS = 32
REL_MAX_DIST = 128

OFF_A = 0
OFF_B = 3072
OFF_C = 6144
OFF_D = 9216
OFF_S = 10496
SMALL = 256
NCOLS = OFF_S + SMALL
S_GLA, S_WD, S_AD, S_GD, S_VR = 0, 16, 80, 144, 208
TAIL = 512
OFF_T = NCOLS - TAIL

VMEM_LIMIT = 56 * 1024 * 1024


def _cparams(sem, vmem=VMEM_LIMIT):
    return pltpu.CompilerParams(dimension_semantics=sem, vmem_limit_bytes=vmem)


def _dot(a, b, prec=None):
    return jnp.dot(a, b, preferred_element_type=F32, precision=prec)


def _dot_nt(a, b, prec=None):
    return lax.dot_general(a, b, (((1,), (1,)), ((), ())), preferred_element_type=F32, precision=prec)


def _dot_tn(a, b, prec=None):
    return lax.dot_general(a, b, (((0,), (0,)), ((), ())), preferred_element_type=F32, precision=prec)


def _sigmoid(z):
    return 1.0 / (1.0 + jnp.exp(-z))


def _softplus(z):
    return jnp.maximum(z, 0.0) + jnp.log(1.0 + jnp.exp(-jnp.abs(z)))


def _seg_ones():
    r = lax.broadcasted_iota(jnp.int32, (LANE, LANE), 0) // HALF
    c = lax.broadcasted_iota(jnp.int32, (LANE, LANE), 1) // HALF
    return (r == c).astype(BF16)


def _split(x):
    hi = x.astype(BF16)
    return hi, (x - hi.astype(F32)).astype(BF16)


def _dot_left2(x, w):
    hi, lo = _split(x)
    return _dot(hi, w) + _dot(lo, w)


def _tri_cumsum(tri, x):
    hi, lo = _split(x)
    return _dot(tri, hi) + _dot(tri, lo)


def _segsum(x, ones):
    return _dot_left2(x, ones)


def _ada_body(c_ref, w_ref, b_ref, o_ref):
    c = c_ref[...]
    s = (c * _sigmoid(c)).astype(BF16)
    o_ref[...] = _dot(s, w_ref[...].astype(BF16)) + b_ref[...]


def _ada(c8, ada_w, ada_b, layer, tn=512):
    _, d, n = ada_w.shape
    return pl.pallas_call(
        _ada_body,
        out_shape=jax.ShapeDtypeStruct((8, n), F32),
        grid=(n // tn,),
        in_specs=[pl.BlockSpec((8, d), lambda j: (0, 0)),
                  pl.BlockSpec((None, d, tn), lambda j: (layer, 0, j)),
                  pl.BlockSpec((1, tn), lambda j: (0, j))],
        out_specs=pl.BlockSpec((8, tn), lambda j: (0, j)),
        compiler_params=_cparams(("parallel",)),
        name="ada_mod",
    )(c8, ada_w, ada_b[layer].reshape(1, n))


def _cast_body(w_ref, o_ref):
    o_ref[...] = w_ref[...].astype(o_ref.dtype)


def _cast_bf16(w, layer, tr=256):
    _, rows, cols = w.shape
    return pl.pallas_call(
        _cast_body,
        out_shape=jax.ShapeDtypeStruct((rows, cols), BF16),
        grid=(rows // tr,),
        in_specs=[pl.BlockSpec((None, tr, cols), lambda i: (layer, i, 0))],
        out_specs=pl.BlockSpec((tr, cols), lambda i: (i, 0)),
        compiler_params=_cparams(("parallel",)),
        name="weight_cast",
    )(w)


P_IN = 10704
SRC_B, SRC_C, SRC_D = 3072, 6160, 9424
SRC_BS, SRC_CS = SRC_B + 3072, SRC_C + 3072


PIECE = 16
PIECES = LANE // PIECE


def _piece_index(j, s):
    n_ab, n_c, n_d = OFF_C // LANE, OFF_D // LANE, OFF_S // LANE
    small0 = SRC_BS // PIECE if s == 0 else SRC_CS // PIECE + s - 1
    small1 = SRC_CS // PIECE + PIECES - 1 + min(s, 4)
    return jnp.where(j < n_ab, j * PIECES + s,
           jnp.where(j < n_c, SRC_C // PIECE + (j - n_ab) * PIECES + s,
           jnp.where(j < n_d, SRC_D // PIECE + (j - n_c) * PIECES + s,
           jnp.where(j == n_d, small0, small1))))


def _regroup_body(*refs):
    w_refs, vr_ref, o_ref = refs[:PIECES], refs[PIECES], refs[PIECES + 1]
    last = pl.program_id(0) == pl.num_programs(0) - 1
    pieces = [r[...] for r in w_refs]
    pieces[5] = jnp.where(last, vr_ref[:PIECE, :], pieces[5])
    pieces[6] = jnp.where(last, vr_ref[PIECE:, :], pieces[6])
    pieces[7] = jnp.where(last, 0.0, pieces[7])
    o_ref[...] = jnp.concatenate(pieces, axis=0).T.astype(o_ref.dtype)


def _regroup_w_in(w_in, layer, vres_down):
    _, d, _ = w_in.shape
    w_t = jnp.swapaxes(w_in, 1, 2)
    piece = lambda s: pl.BlockSpec((None, PIECE, d), lambda j: (layer, _piece_index(j, s), 0))
    return pl.pallas_call(
        _regroup_body,
        out_shape=jax.ShapeDtypeStruct((d, NCOLS), BF16),
        grid=(NCOLS // LANE,),
        in_specs=[piece(s) for s in range(PIECES)] + [pl.BlockSpec((2 * PIECE, d), lambda j: (0, 0))],
        out_specs=pl.BlockSpec((d, LANE), lambda j: (0, j)),
        compiler_params=_cparams(("parallel",)),
        name="w_in_regroup",
    )(*([w_t] * PIECES), vres_down.T)


def _norm_body(x_ref, sc_ref, sh_ref, o_ref):
    x = x_ref[0]
    ms = jnp.mean(x * x, axis=-1, keepdims=True)
    h = x * lax.rsqrt(ms + NORM_EPS) * (1.0 + sc_ref[0]) + sh_ref[0]
    o_ref[...] = h.astype(o_ref.dtype)


def _norm_mod(x, mod3, sc_idx, sh_idx, ts=256):
    bsz, seq, d = x.shape
    ts = min(ts, seq)
    ns = seq // ts
    return pl.pallas_call(
        _norm_body,
        out_shape=jax.ShapeDtypeStruct((bsz * seq, d), BF16),
        grid=(bsz, ns),
        in_specs=[pl.BlockSpec((1, ts, d), lambda b, i: (b, i, 0)),
                  pl.BlockSpec((1, 1, d), lambda b, i: (b * 6 + sc_idx, 0, 0)),
                  pl.BlockSpec((1, 1, d), lambda b, i: (b * 6 + sh_idx, 0, 0))],
        out_specs=pl.BlockSpec((ts, d), lambda b, i: (b * ns + i, 0)),
        compiler_params=_cparams(("parallel", "parallel")),
        name="norm_mod",
    )(x, mod3, mod3)


def _in_proj_body(a_ref, b_ref, main_ref, tail_ref):
    j = pl.program_id(1)
    acc = _dot(a_ref[...], b_ref[...])

    @pl.when(j < pl.num_programs(1) - 1)
    def _():
        main_ref[...] = acc.astype(main_ref.dtype)

    @pl.when(j == pl.num_programs(1) - 1)
    def _():
        tail_ref[...] = acc


def _in_proj(a, b, tm=2048):
    m, k = a.shape
    tm = min(tm, m)
    n_main = OFF_T // TAIL
    return pl.pallas_call(
        _in_proj_body,
        out_shape=[jax.ShapeDtypeStruct((m, OFF_T), BF16), jax.ShapeDtypeStruct((m, TAIL), F32)],
        grid=(m // tm, n_main + 1),
        in_specs=[pl.BlockSpec((tm, k), lambda i, j: (i, 0)),
                  pl.BlockSpec((k, TAIL), lambda i, j: (0, j))],
        out_specs=[pl.BlockSpec((tm, TAIL), lambda i, j: (i, jnp.minimum(j, n_main - 1))),
                   pl.BlockSpec((tm, TAIL), lambda i, j: (i, 0))],
        compiler_params=_cparams(("parallel", "arbitrary")),
        name="in_proj",
    )(a, b)


def _outproj_body(a0, a1, a2, a3, w_ref, x_ref, g_ref, o_ref):
    acc = _dot(a0[...], w_ref[0 * GROUP:1 * GROUP, :])
    acc += _dot(a1[...], w_ref[1 * GROUP:2 * GROUP, :])
    acc += _dot(a2[...], w_ref[2 * GROUP:3 * GROUP, :])
    acc += _dot(a3[...], w_ref[3 * GROUP:4 * GROUP, :])
    o_ref[0] = x_ref[0] + g_ref[0] * acc


def _out_proj(parts, w, x, mod3, gate_idx, tm=1024, tn=512):
    bsz, seq, d = x.shape
    tm = min(tm, seq)
    ns = seq // tm
    a_spec = pl.BlockSpec((tm, GROUP), lambda b, i, j: (b * ns + i, 0))
    return pl.pallas_call(
        _outproj_body,
        out_shape=jax.ShapeDtypeStruct((bsz, seq, d), F32),
        grid=(bsz, ns, d // tn),
        in_specs=[a_spec, a_spec, a_spec, a_spec,
                  pl.BlockSpec((4 * GROUP, tn), lambda b, i, j: (0, j)),
                  pl.BlockSpec((1, tm, tn), lambda b, i, j: (b, i, j)),
                  pl.BlockSpec((1, 1, tn), lambda b, i, j: (b * 6 + gate_idx, 0, j))],
        out_specs=pl.BlockSpec((1, tm, tn), lambda b, i, j: (b, i, j)),
        compiler_params=_cparams(("parallel", "parallel", "parallel")),
        name="out_proj",
    )(*parts, w, x, mod3)


def _ffn_up_body(a_ref, w1_ref, w3_ref, o_ref):
    a = a_ref[...]
    u = _dot(a, w1_ref[...].astype(BF16))
    v = _dot(a, w3_ref[...].astype(BF16))
    o_ref[...] = (u * _sigmoid(u) * v).astype(o_ref.dtype)


def _ffn_up(h, w1, w3, layer, tm=1024, tn=256):
    m, k = h.shape
    n = w1.shape[2]
    tm = min(tm, m)
    return pl.pallas_call(
        _ffn_up_body,
        out_shape=jax.ShapeDtypeStruct((m, n), BF16),
        grid=(m // tm, n // tn),
        in_specs=[pl.BlockSpec((tm, k), lambda i, j: (i, 0)),
                  pl.BlockSpec((None, k, tn), lambda i, j: (layer, 0, j)),
                  pl.BlockSpec((None, k, tn), lambda i, j: (layer, 0, j))],
        out_specs=pl.BlockSpec((tm, tn), lambda i, j: (i, j)),
        compiler_params=_cparams(("parallel", "parallel")),
        name="ffn_up",
    )(h, w1, w3)


def _ffn_down_body(a_ref, w_ref, x_ref, g_ref, o_ref):
    o_ref[0] = x_ref[0] + g_ref[0] * _dot(a_ref[...], w_ref[...])


def _ffn_down(a, w, x, mod3, gate_idx, tm=512, tn=512):
    bsz, seq, d = x.shape
    k = a.shape[1]
    tm = min(tm, seq)
    ns = seq // tm
    return pl.pallas_call(
        _ffn_down_body,
        out_shape=jax.ShapeDtypeStruct((bsz, seq, d), F32),
        grid=(bsz, ns, d // tn),
        in_specs=[pl.BlockSpec((tm, k), lambda b, i, j: (b * ns + i, 0)),
                  pl.BlockSpec((k, tn), lambda b, i, j: (0, j)),
                  pl.BlockSpec((1, tm, tn), lambda b, i, j: (b, i, j)),
                  pl.BlockSpec((1, 1, tn), lambda b, i, j: (b * 6 + gate_idx, 0, j))],
        out_specs=pl.BlockSpec((1, tm, tn), lambda b, i, j: (b, i, j)),
        compiler_params=_cparams(("parallel", "parallel", "parallel")),
        name="ffn_down",
    )(a, w, x, mod3)


def _bias_body(tab_ref, o_ref, *, off, window, shift, key_major, scale):
    h = pl.program_id(0)
    rows, cols = o_ref.shape[1], o_ref.shape[2]
    last = tab_ref[REL_BUCKETS - 1, h]
    max_exact = REL_BUCKETS // 2

    def block(d0):
        qi = lax.broadcasted_iota(jnp.int32, (LANE, LANE), 1 if key_major else 0)
        kj = lax.broadcasted_iota(jnp.int32, (LANE, LANE), 0 if key_major else 1)
        dist = d0 + qi - kj
        n = jnp.maximum(dist, 0)
        nf = jnp.maximum(n, 1).astype(F32)
        large = max_exact + (jnp.log(nf / max_exact) / math.log(REL_MAX_DIST / max_exact)
                             * (REL_BUCKETS - max_exact)).astype(jnp.int32)
        large = jnp.minimum(large, REL_BUCKETS - 1)
        bucket = jnp.where(n < max_exact, n, large)
        bias = jnp.full((LANE, LANE), last, F32)
        for b in range(REL_BUCKETS - 1):
            bias = jnp.where(bucket == b, tab_ref[b, h], bias)
        if shift:
            bias = bias - last
        if scale != 1.0:
            bias = bias * scale
        valid = dist >= 0
        if window is not None:
            valid = jnp.logical_and(valid, dist < window)
        return jnp.where(valid, bias, NEG_INF)

    cache = {}
    for br in range(rows // LANE):
        for bc in range(cols // LANE):
            qb, kb = (bc, br) if key_major else (br, bc)
            d0 = off + LANE * (qb - kb)
            if d0 not in cache:
                if d0 + LANE <= 0 or (window is not None and d0 - LANE >= window):
                    cache[d0] = jnp.full((LANE, LANE), NEG_INF, F32)
                elif d0 - LANE >= REL_MAX_DIST and (window is None or d0 + LANE <= window):
                    const = (0.0 if shift else last) * scale
                    cache[d0] = jnp.full((LANE, LANE), const, F32)
                else:
                    cache[d0] = block(d0)
            o_ref[0, br * LANE:(br + 1) * LANE, bc * LANE:(bc + 1) * LANE] = cache[d0]


def _bias_tiles(table, rows, cols, off, window, shift, key_major=False, scale=1.0):
    nh = table.shape[1]
    return pl.pallas_call(
        functools.partial(_bias_body, off=off, window=window, shift=shift, key_major=key_major, scale=scale),
        out_shape=jax.ShapeDtypeStruct((nh, rows, cols), F32),
        grid=(nh,),
        in_specs=[pl.BlockSpec(memory_space=pltpu.SMEM)],
        out_specs=pl.BlockSpec((1, rows, cols), lambda h: (h, 0, 0)),
        compiler_params=_cparams(("parallel",)),
        name="rel_bias_tiles",
    )(table)


LOG2E = 1.4426950408889634
VT_ROWS = LANE + 16
Q_CHUNK = 512


def _prep_a_body(c_ref, qg_ref, kg_ref, qk_ref, vt_ref):
    ones = _seg_ones()
    scale = DIFF_QK ** -0.5 * LOG2E
    tm = c_ref.shape[0]
    for j in range(2 * DIFF_HEADS):
        x = c_ref[:, j * LANE:(j + 1) * LANE].astype(F32)
        ms = _segsum(x * x, ones) * (1.0 / DIFF_QK)
        gain = qg_ref[...] * scale if j < DIFF_HEADS else kg_ref[...]
        qk_ref[:, j * LANE:(j + 1) * LANE] = (x * lax.rsqrt(ms + NORM_EPS) * gain).astype(qk_ref.dtype)
    for h in range(DIFF_HEADS):
        v = c_ref[:, 2 * GROUP + h * LANE:2 * GROUP + (h + 1) * LANE].astype(F32)
        vt_ref[h * VT_ROWS:h * VT_ROWS + LANE, :] = v.T.astype(vt_ref.dtype)
        vt_ref[h * VT_ROWS + LANE:(h + 1) * VT_ROWS, :] = jnp.ones((VT_ROWS - LANE, tm), vt_ref.dtype)


def _prep_a(cols, q_gain, k_gain, tm=256):
    t = cols.shape[0]
    tm = min(tm, t)
    qg = jnp.tile(q_gain, 2).reshape(1, LANE)
    kg = jnp.tile(k_gain, 2).reshape(1, LANE)
    vec = pl.BlockSpec((1, LANE), lambda i: (0, 0))
    return pl.pallas_call(
        _prep_a_body,
        out_shape=[jax.ShapeDtypeStruct((t, 2 * GROUP), BF16),
                   jax.ShapeDtypeStruct((DIFF_HEADS * VT_ROWS, t), BF16)],
        grid=(t // tm,),
        in_specs=[pl.BlockSpec((tm, 3 * GROUP), lambda i: (i, OFF_A // (3 * GROUP))), vec, vec],
        out_specs=[pl.BlockSpec((tm, 2 * GROUP), lambda i: (i, 0)),
                   pl.BlockSpec((DIFF_HEADS * VT_ROWS, tm), lambda i: (0, i))],
        compiler_params=_cparams(("parallel",)),
        name="diff_prep",
    )(cols, qg, kg)


def _flash_body(it_ref, jt_ref, q_ref, k_ref, vt_ref, vtp_ref, bd_ref, bs_ref, lam_ref, sub_ref, o_ref,
                qlo, qhi, m1, a1, m2, a2, p_scr, al_scr, *, lam_init):
    t = pl.program_id(2)
    i = it_ref[t]
    j = jt_ref[t]

    @pl.when(j == 0)
    def _():
        q = q_ref[...]
        lane = lax.broadcasted_iota(jnp.int32, q.shape, 1)
        zero = jnp.zeros_like(q)
        qlo[...] = jnp.where(lane < HALF, q, zero)
        qhi[...] = jnp.where(lane >= HALF, q, zero)
        for m, a in ((m1, a1), (m2, a2)):
            m[...] = jnp.full(m.shape, NEG_INF, F32)
            a[...] = jnp.zeros(a.shape, F32)
        al_scr[...] = jnp.ones(al_scr.shape, F32)

    @pl.when(t == 0)
    def _():
        p_scr[...] = jnp.zeros(p_scr.shape, BF16)

    pending = (j > 0).astype(F32)

    def step(bias, last):
        k = k_ref[...]
        vt = vt_ref[...]
        vtp = vtp_ref[...]
        nq = q_ref.shape[0]
        qc = min(Q_CHUNK, nq)
        chains = [(mi, qq, m, a, slice(c * qc, (c + 1) * qc))
                  for mi, (qq, m, a) in enumerate(((qlo, m1, a1), (qhi, m2, a2))) for c in range(nq // qc)]
        ahead = 2
        kend = [qs.stop if last else k.shape[0] for _, _, _, _, qs in chains]
        if bias is None:
            scores = [_dot_nt(k, qq[qs, :]) for _, qq, _, _, qs in chains[:ahead]]
        elif last:
            scores = [_dot_nt(k[:ke, :], c[1][c[4], :]) + bias[:ke, c[4]] for c, ke in zip(chains, kend)]
        else:
            full = [_dot_nt(k, qq[...]) for qq in (qlo, qhi)]
            scores = [full[c[0]][:, c[4]] + bias[:, c[4]] for c in chains]
        for idx, (mi, _, m, a, qs) in enumerate(chains):
            if bias is None and idx + ahead < len(chains):
                _, qq_n, _, _, qs_n = chains[idx + ahead]
                scores.append(_dot_nt(k, qq_n[qs_n, :]))
            acc = al_scr[mi, :, qs] * a[:, qs] + pending * _dot(vtp, p_scr[mi, :, qs])
            s = scores[idx]
            m_old = m[:, qs]
            m_new = jnp.maximum(m_old, jnp.max(s, axis=0, keepdims=True))
            p = jnp.exp2(s - m_new).astype(BF16)
            alpha = jnp.exp2(m_old - m_new)
            if last:
                acc = alpha * acc + _dot(vt[:, :kend[idx]], p)
            else:
                p_scr[mi, :, qs] = p
                al_scr[mi, :, qs] = alpha
            a[:, qs] = acc
            m[:, qs] = m_new

    @pl.when(j < i - 1)
    def _():
        step(None, False)

    @pl.when(j == i - 1)
    def _():
        step(bs_ref[0], False)

    @pl.when(j == i)
    def _():
        step(bd_ref[0], True)
        lam = lam_ref[...]
        e1 = jnp.exp(jnp.sum(lam[0:1] * lam[1:2], axis=-1, keepdims=True))
        e2 = jnp.exp(jnp.sum(lam[2:3] * lam[3:4], axis=-1, keepdims=True))
        lam_full = e1 - e2 + lam_init
        o = a1[:LANE, :] / a1[LANE:LANE + 1, :] - lam_full * (a2[:LANE, :] / a2[LANE:LANE + 1, :])
        ms = jnp.mean(o * o, axis=0, keepdims=True)
        o = o * lax.rsqrt(ms + NORM_EPS) * sub_ref[...] * (1.0 - lam_init)
        o_ref[...] = o.T.astype(o_ref.dtype)


def _diff_attention(qk, vt, bias_diag, bias_sub, lam, subln, bsz, seq, layer_idx, tile):
    nq = seq // tile
    lam_init = 0.8 - 0.6 * math.exp(-0.3 * layer_idx)
    pairs = [(i, j) for i in range(nq) for j in range(i + 1)]
    i_tab = jnp.array([ij[0] for ij in pairs], jnp.int32)
    j_tab = jnp.array([ij[1] for ij in pairs], jnp.int32)
    grid_spec = pltpu.PrefetchScalarGridSpec(
        num_scalar_prefetch=2,
        grid=(bsz, DIFF_HEADS, len(pairs)),
        in_specs=[pl.BlockSpec((tile, LANE), lambda b, h, t, it, jt: (b * nq + it[t], h)),
                  pl.BlockSpec((tile, LANE), lambda b, h, t, it, jt: (b * nq + jt[t], DIFF_HEADS + h)),
                  pl.BlockSpec((VT_ROWS, tile), lambda b, h, t, it, jt: (h, b * nq + jt[t])),
                  pl.BlockSpec((VT_ROWS, tile), lambda b, h, t, it, jt: (h, b * nq + jnp.maximum(jt[t] - 1, 0))),
                  pl.BlockSpec((1, tile, tile), lambda b, h, t, it, jt: (h, 0, 0)),
                  pl.BlockSpec((1, tile, tile), lambda b, h, t, it, jt: (h, 0, 0)),
                  pl.BlockSpec((4, DIFF_QK), lambda b, h, t, it, jt: (0, 0)),
                  pl.BlockSpec((LANE, 1), lambda b, h, t, it, jt: (0, 0))],
        out_specs=pl.BlockSpec((tile, LANE), lambda b, h, t, it, jt: (b * nq + it[t], h)),
        scratch_shapes=[pltpu.VMEM((tile, LANE), BF16), pltpu.VMEM((tile, LANE), BF16),
                        pltpu.VMEM((1, tile), F32), pltpu.VMEM((VT_ROWS, tile), F32),
                        pltpu.VMEM((1, tile), F32), pltpu.VMEM((VT_ROWS, tile), F32),
                        pltpu.VMEM((2, tile, tile), BF16), pltpu.VMEM((2, 1, tile), F32)])
    return pl.pallas_call(
        functools.partial(_flash_body, lam_init=lam_init),
        out_shape=jax.ShapeDtypeStruct((bsz * seq, GROUP), BF16),
        grid_spec=grid_spec,
        compiler_params=_cparams(("parallel", "parallel", "arbitrary")),
        name="diff_attention",
    )(i_tab, j_tab, qk, qk, vt, vt, bias_diag, bias_sub, lam, subln.reshape(LANE, 1))


def _swa_body(q_ref, kp_ref, kc_ref, vp_ref, vc_ref, bias_ref, qg_ref, kg_ref, sink_ref, o_ref):
    n = pl.program_id(1)
    g = pl.program_id(2)
    ones = _seg_ones()
    lane = lax.broadcasted_iota(jnp.int32, (SWA_BLOCK, LANE), 1)
    lo = lane < HALF

    def norm(x, gain):
        ms = _segsum(x * x, ones) * (1.0 / HALF)
        return x * lax.rsqrt(ms + NORM_EPS) * gain

    k = jnp.concatenate([norm(kp_ref[...], kg_ref[...]), norm(kc_ref[...], kg_ref[...])], axis=0)
    v = jnp.concatenate([vp_ref[...], vc_ref[...]], axis=0)
    k_sw = pltpu.roll(k, HALF, axis=1)
    v_sw = pltpu.roll(v, HALF, axis=1)
    first_head = g == 0
    lo2 = lax.broadcasted_iota(jnp.int32, (2 * SWA_BLOCK, LANE), 1) < HALF
    k_a = jnp.where(first_head, k, k_sw).astype(BF16)
    k_b = jnp.where(first_head, k_sw, k).astype(BF16)
    v_mine_lo = jnp.where(first_head, v, v_sw)
    v_mine_hi = jnp.where(first_head, v_sw, v)
    v_a = jnp.where(lo2, v_mine_lo, 0.0).astype(BF16)
    v_b = jnp.where(lo2, 0.0, v_mine_hi).astype(BF16)
    kcol = lax.broadcasted_iota(jnp.int32, (SWA_BLOCK, 2 * SWA_BLOCK), 1)
    pad = jnp.logical_and(n == 0, kcol < SWA_BLOCK)
    scale = HALF ** -0.5
    grp = SWA_HEADS // SWA_KV_HEADS
    heads = range(grp)
    q = [norm(q_ref[:, pr * LANE:(pr + 1) * LANE].astype(F32), qg_ref[...] * scale) for pr in range(grp // 2)]
    qm = [jnp.where(lo if hh % 2 == 0 else jnp.logical_not(lo), q[hh // 2], 0.0).astype(BF16) for hh in heads]
    s = [_dot_nt(qm[hh], k_a if hh % 2 == 0 else k_b) for hh in heads]
    s = [jnp.where(pad, NEG_INF, s[hh] + bias_ref[hh]) for hh in heads]
    sink = [sink_ref[g * grp + hh] for hh in heads]
    m = [jnp.maximum(jnp.max(s[hh], axis=-1, keepdims=True), sink[hh]) for hh in heads]
    p = [jnp.exp(s[hh] - m[hh]) for hh in heads]
    den = [jnp.sum(p[hh], axis=-1, keepdims=True) + jnp.exp(sink[hh] - m[hh]) for hh in heads]
    pv = [_dot(p[hh].astype(BF16), v_a if hh % 2 == 0 else v_b) for hh in heads]
    for pr in range(grp // 2):
        o = pv[2 * pr] / den[2 * pr] + pv[2 * pr + 1] / den[2 * pr + 1]
        o_ref[:, pr * LANE:(pr + 1) * LANE] = o.astype(o_ref.dtype)


def _swa_attention(cols, tail, bias, q_gain, k_gain, sinks, bsz, seq):
    nb = seq // SWA_BLOCK
    grp = SWA_HEADS // SWA_KV_HEADS
    qw = grp * HALF
    kcol = (OFF_D + GROUP - OFF_T) // LANE
    vcol = kcol + 1
    prev = lambda b, n, g: b * nb + jnp.maximum(n - 1, 0)
    cur = lambda b, n, g: b * nb + n
    vec = pl.BlockSpec((1, LANE), lambda b, n, g: (0, 0))
    return pl.pallas_call(
        _swa_body,
        out_shape=jax.ShapeDtypeStruct((bsz * seq, GROUP), BF16),
        grid=(bsz, nb, SWA_KV_HEADS),
        in_specs=[pl.BlockSpec((SWA_BLOCK, qw), lambda b, n, g: (cur(b, n, g), OFF_D // qw + g)),
                  pl.BlockSpec((SWA_BLOCK, LANE), lambda b, n, g: (prev(b, n, g), kcol)),
                  pl.BlockSpec((SWA_BLOCK, LANE), lambda b, n, g: (cur(b, n, g), kcol)),
                  pl.BlockSpec((SWA_BLOCK, LANE), lambda b, n, g: (prev(b, n, g), vcol)),
                  pl.BlockSpec((SWA_BLOCK, LANE), lambda b, n, g: (cur(b, n, g), vcol)),
                  pl.BlockSpec((grp, SWA_BLOCK, 2 * SWA_BLOCK), lambda b, n, g: (g, 0, 0)),
                  vec, vec,
                  pl.BlockSpec(memory_space=pltpu.SMEM)],
        out_specs=pl.BlockSpec((SWA_BLOCK, qw), lambda b, n, g: (cur(b, n, g), g)),
        compiler_params=_cparams(("parallel", "parallel", "parallel")),
        name="swa_attention",
    )(cols, tail, tail, tail, tail, bias,
      jnp.tile(q_gain, 2).reshape(1, LANE), jnp.tile(k_gain, 2).reshape(1, LANE), sinks)


def _prep_c_body(*refs, seq, with_vres):
    (cm_ref, cs_ref, pm_ref, ps_ref, mum_ref, mus_ref, wup_ref, aup_ref, gup_ref, vup_ref, glaup_ref,
     w0_ref, a0_ref, v0_ref, kk_ref, ka_ref, glab_ref) = refs[:17]
    rest = refs[17:]
    if with_vres:
        vf_ref, rest = rest[0], rest[1:]
    r_ref, lw_ref, k_ref, v_ref, nk_ref, kb_ref, g_ref, la_ref = rest
    tm = cm_ref.shape[0]
    i = pl.program_id(0)
    seq_start = (i * tm) % seq == 0

    def shifted(cur, prev_rows, mu):
        row = lax.broadcasted_iota(jnp.int32, cur.shape, 0)
        before = jnp.where(seq_start, 0.0, prev_rows[PREV_ROWS - 1:PREV_ROWS, :].astype(F32))
        prev = jnp.where(row == 0, before, pltpu.roll(cur, 1, axis=0))
        return cur + (prev - cur) * mu

    def low_rank(x, w2_ref):
        hi, lo = _split(x)
        return _dot(hi, w2_ref[0]) + _dot(lo, w2_ref[0]) + _dot(hi, w2_ref[1])

    cs = cs_ref[...]
    la_ref[...] = -_softplus(-(low_rank(cs, glaup_ref) + glab_ref[...])) * (1.0 / GLA_NORMALIZER)
    sm = shifted(cm_ref[...].astype(F32), pm_ref, mum_ref[...])
    ss = shifted(cs, ps_ref, mus_ref[...])
    r = sm[:, :GROUP]
    k = sm[:, GROUP:2 * GROUP]
    v = sm[:, 2 * GROUP:]
    if with_vres:
        gate = _sigmoid(v0_ref[...] + low_rank(ss, vup_ref))
        v = v + (vf_ref[...] - v) * gate
    w_log = -_softplus(-(w0_ref[...] + low_rank(jnp.tanh(ss), wup_ref))) - 0.5
    a = _sigmoid(a0_ref[...] + low_rank(ss, aup_ref))
    r_ref[...] = r.astype(r_ref.dtype)
    lw_ref[...] = -jnp.exp(w_log)
    v_ref[...] = v
    g_ref[...] = low_rank(_sigmoid(ss), gup_ref).astype(g_ref.dtype)
    k_ref[...] = (k * (1.0 + (a - 1.0) * ka_ref[...])).astype(k_ref.dtype)
    ones = _seg_ones()
    kk = k * kk_ref[...]
    for j in range(GROUP // LANE):
        sl = slice(j * LANE, (j + 1) * LANE)
        x = kk[:, sl]
        nrm = jnp.maximum(jnp.sqrt(_segsum(x * x, ones)), 1e-12)
        x = x / nrm
        nk_ref[:, sl] = x.astype(nk_ref.dtype)
        kb_ref[:, sl] = (x * a[:, sl]).astype(kb_ref.dtype)


def _prep_c(cols, tail, seq, mu_main, mu_small, w_up, a_up, g_up, v_up, gla_up, w0, a0, v0, k_k, k_a, gla_bias,
            v_first, tm=256):
    t = cols.shape[0]
    tm = min(tm, seq)
    with_vres = v_first is not None
    cmain = OFF_C // (3 * GROUP)
    csmall = (OFF_S - OFF_T) // SMALL
    prev = lambda i: jnp.maximum(i * (tm // PREV_ROWS) - 1, 0)
    full = lambda r, c: pl.BlockSpec((r, c), lambda i: (0, 0))
    hilo = lambda c: pl.BlockSpec((2, SMALL, c), lambda i: (0, 0, 0))
    in_specs = [pl.BlockSpec((tm, 3 * GROUP), lambda i: (i, cmain)),
                pl.BlockSpec((tm, SMALL), lambda i: (i, csmall)),
                pl.BlockSpec((PREV_ROWS, 3 * GROUP), lambda i: (prev(i), cmain)),
                pl.BlockSpec((PREV_ROWS, SMALL), lambda i: (prev(i), csmall)),
                full(1, 3 * GROUP), full(1, SMALL),
                hilo(GROUP), hilo(GROUP), hilo(GROUP), hilo(GROUP), hilo(GLA_HEADS * GLA_DK),
                full(1, GROUP), full(1, GROUP), full(1, GROUP), full(1, GROUP), full(1, GROUP),
                full(1, GLA_HEADS * GLA_DK)]
    args = [cols, tail, cols, tail, mu_main, mu_small, w_up, a_up, g_up, v_up, gla_up,
            w0, a0, v0, k_k, k_a, gla_bias]
    if with_vres:
        in_specs.append(pl.BlockSpec((tm, GROUP), lambda i: (i, 0)))
        args.append(v_first)
    row = pl.BlockSpec((tm, GROUP), lambda i: (i, 0))
    out = [jax.ShapeDtypeStruct((t, GROUP), dt) for dt in (BF16, F32, BF16, F32, BF16, BF16, BF16)]
    return pl.pallas_call(
        functools.partial(_prep_c_body, seq=seq, with_vres=with_vres),
        out_shape=out + [jax.ShapeDtypeStruct((t, GLA_HEADS * GLA_DK), F32)],
        grid=(t // tm,),
        in_specs=in_specs,
        out_specs=[row] * 7 + [pl.BlockSpec((tm, GLA_HEADS * GLA_DK), lambda i: (i, 0))],
        compiler_params=_cparams(("parallel",)),
        name="rwkv_gla_prep",
    )(*args)


def _gla_body(q_ref, k_ref, v_ref, g_ref, la_ref, gain_ref, o_ref, st_ref):
    @pl.when(pl.program_id(2) == 0)
    def _():
        st_ref[...] = jnp.zeros(st_ref.shape, F32)

    row = lax.broadcasted_iota(jnp.int32, (CHUNK, CHUNK), 0)
    col = lax.broadcasted_iota(jnp.int32, (CHUNK, CHUNK), 1)
    causal = row >= col
    tri = causal.astype(BF16)
    nchunk = q_ref.shape[0] // CHUNK

    def local(sls):
        n = range(len(sls))
        k = [k_ref[sl, :].astype(F32) for sl in sls]
        v = [v_ref[sl, :].astype(BF16) for sl in sls]
        b = [_tri_cumsum(tri, la_ref[sl, :]) for sl in sls]
        b_last = [b[i][CHUNK - 1:CHUNK, :] for i in n]
        q_dec = [(q_ref[sl, :].astype(F32) * (GLA_DK ** -0.5) * jnp.exp(b[i])).astype(BF16)
                 for i, sl in enumerate(sls)]
        a_intra = [jnp.where(causal, _dot_nt(q_dec[i], (k[i] * jnp.exp(-b[i])).astype(BF16)), 0.0) for i in n]
        upd = [_dot_tn(v[i], (k[i] * jnp.exp(b_last[i] - b[i])).astype(BF16)) for i in n]
        o_intra = [_dot(a_intra[i].astype(BF16), v[i]) for i in n]
        return [(q_dec[i], o_intra[i], upd[i], jnp.exp(b_last[i])) for i in n]

    def advance(sl, q_dec, o_intra, upd, dec):
        state = st_ref[...]
        s_hi, s_lo = _split(state)
        o = o_intra + _dot_nt(q_dec, s_hi) + _dot_nt(q_dec, s_lo)
        st_ref[...] = state * dec + upd
        ms = jnp.mean(o * o, axis=-1, keepdims=True)
        o = o * lax.rsqrt(ms + NORM_EPS) * gain_ref[...]
        gate = g_ref[sl, :].astype(F32)
        o_ref[sl, :] = (o * (gate * _sigmoid(gate))).astype(o_ref.dtype)

    gsz = min(SCAN_GROUP, nchunk)

    def group(gi, carry):
        sls = [pl.ds(pl.multiple_of((gi * gsz + g) * CHUNK, CHUNK), CHUNK) for g in range(gsz)]
        for sl, part in zip(sls, local(sls)):
            advance(sl, *part)
        return carry

    lax.fori_loop(0, nchunk // gsz, group, 0)


def _gla(cols, log_a, out_gain, bsz, seq, tc=1024):
    tc = min(tc, seq)
    ns = seq // tc
    qc = OFF_B // GLA_DK
    kc = qc + GLA_HEADS
    vc = (OFF_B + 2 * GLA_HEADS * GLA_DK) // GLA_DV
    gc = vc + GLA_HEADS
    rows = lambda b, h, i: b * ns + i
    return pl.pallas_call(
        _gla_body,
        out_shape=jax.ShapeDtypeStruct((bsz * seq, GROUP), BF16),
        grid=(bsz, GLA_HEADS, ns),
        in_specs=[pl.BlockSpec((tc, GLA_DK), lambda b, h, i: (rows(b, h, i), qc + h)),
                  pl.BlockSpec((tc, GLA_DK), lambda b, h, i: (rows(b, h, i), kc + h)),
                  pl.BlockSpec((tc, GLA_DV), lambda b, h, i: (rows(b, h, i), vc + h)),
                  pl.BlockSpec((tc, GLA_DV), lambda b, h, i: (rows(b, h, i), gc + h)),
                  pl.BlockSpec((tc, GLA_DK), lambda b, h, i: (rows(b, h, i), h)),
                  pl.BlockSpec((1, GLA_DV), lambda b, h, i: (0, 0))],
        out_specs=pl.BlockSpec((tc, GLA_DV), lambda b, h, i: (rows(b, h, i), h)),
        scratch_shapes=[pltpu.VMEM((GLA_DV, GLA_DK), F32)],
        compiler_params=_cparams(("parallel", "parallel", "arbitrary")),
        name="gla_scan",
    )(cols, cols, cols, cols, log_a, out_gain.reshape(1, GLA_DV))


def _rwkv_body(r_ref, lw_ref, k_ref, v_ref, nk_ref, kb_ref, g_ref, rk_ref, lnw_ref, lnb_ref, o_ref, st_ref):
    @pl.when(pl.program_id(2) == 0)
    def _():
        st_ref[...] = jnp.zeros(st_ref.shape, F32)

    two = 2 * CHUNK
    row = lax.broadcasted_iota(jnp.int32, (two, two), 0)
    col = lax.broadcasted_iota(jnp.int32, (two, two), 1)
    same = (row // CHUNK) == (col // CHUNK)
    strict = jnp.logical_and(same, (row % CHUNK) > (col % CHUNK))
    incl = jnp.logical_and(same, (row % CHUNK) >= (col % CHUNK))
    eye = (row == col).astype(F32)
    crow = lax.broadcasted_iota(jnp.int32, (CHUNK, CHUNK), 0)
    ccol = lax.broadcasted_iota(jnp.int32, (CHUNK, CHUNK), 1)
    tri = (crow >= ccol).astype(BF16)
    lo = lax.broadcasted_iota(jnp.int32, (CHUNK, LANE), 1) < HALF
    ones = _seg_ones()
    nchunk = r_ref.shape[0] // CHUNK

    def stack(x):
        return jnp.concatenate([jnp.where(lo, x, 0.0), jnp.where(lo, 0.0, x)], axis=0)

    def fold(x):
        return x[:CHUNK, :] + x[CHUNK:, :]

    def transitions(items):
        n = range(len(items))
        r = [r_ref[sl, ls].astype(F32) for sl, ls in items]
        lw = [lw_ref[sl, ls] for sl, ls in items]
        k = [k_ref[sl, ls].astype(F32) for sl, ls in items]
        v = [v_ref[sl, ls] for sl, ls in items]
        cum = [_tri_cumsum(tri, lw[i]) for i in n]
        last = [cum[i][CHUNK - 1:CHUNK, :] for i in n]
        a_bf, r_st, bk_st, bkh_st, v_st = [], [], [], [], []
        for i, (sl, ls) in enumerate(items):
            kb = kb_ref[sl, ls].astype(F32)
            e_neg = jnp.exp(-cum[i])
            e_rem = jnp.exp(last[i] - cum[i])
            a_bf.append(stack(-nk_ref[sl, ls].astype(F32) * jnp.exp(cum[i] - lw[i])).astype(BF16))
            r_st.append(stack(r[i] * jnp.exp(cum[i])))
            bk_st.append(jnp.concatenate([stack(kb * e_neg), stack(k[i] * e_neg)], axis=0).astype(BF16))
            bkh_st.append(jnp.concatenate([stack(kb * e_rem), stack(k[i] * e_rem)], axis=0).astype(BF16))
            v_st.append(stack(v[i]).astype(BF16))
        sc = [_dot_nt(jnp.concatenate([a_bf[i], r_st[i].astype(BF16)], axis=0), bk_st[i]) for i in n]
        a_ab = [jnp.where(strict, sc[i][:two, :two], 0.0) for i in n]
        a_ak = [jnp.where(strict, sc[i][:two, two:], 0.0).astype(BF16) for i in n]
        r_b = [jnp.where(incl, sc[i][two:, :two], 0.0).astype(BF16) for i in n]
        r_k = [jnp.where(incl, sc[i][two:, two:], 0.0).astype(BF16) for i in n]
        akv = [_dot(a_ak[i], v_st[i]).astype(BF16) for i in n]
        rkv = [_dot(r_k[i], v_st[i]) for i in n]
        inv = [eye + a_ab[i] for i in n]
        pw = [a_ab[i].astype(BF16) for i in n]
        for _ in range(5):
            pw = [_dot(pw[i], pw[i]).astype(BF16) for i in n]
            inv = [inv[i] + _dot(inv[i].astype(BF16), pw[i]) for i in n]
        pu = [_dot(inv[i].astype(BF16), jnp.concatenate([a_bf[i], akv[i]], axis=1)).astype(BF16) for i in n]
        rb_pu = [_dot(r_b[i], pu[i]) for i in n]
        m = [_dot_tn(pu[i][:, :LANE], bkh_st[i][:two, :]).astype(BF16) for i in n]
        n0 = [_dot_tn(jnp.concatenate([pu[i][:, LANE:], v_st[i]], axis=0), bkh_st[i]) for i in n]
        rkr = [_segsum(r[i] * k[i] * rk_ref[:, items[i][1]], ones) for i in n]
        return [(fold(r_st[i] + rb_pu[i][:, :LANE]).astype(BF16), fold(rb_pu[i][:, LANE:] + rkv[i]),
                 m[i], n0[i], jnp.exp(last[i]), rkr[i] * v[i]) for i in n]

    def advance(tile, sl, ls, p2, y0, m, n0, dec, bonus):
        state = st_ref[tile]
        s_hi, s_lo = _split(state)
        y = _dot_nt(p2, s_hi) + y0
        st_ref[tile] = state * dec + _dot(s_hi, m) + _dot(s_lo, m) + n0
        d = y - _segsum(y, ones) * (1.0 / HALF)
        var = _segsum(d * d, ones) * (1.0 / HALF)
        y = d * lax.rsqrt(var + RWKV_LN_EPS) * lnw_ref[:, ls] + lnb_ref[:, ls]
        o_ref[sl, ls] = ((y + bonus) * g_ref[sl, ls]).astype(o_ref.dtype)

    gsz = min(RWKV_GROUP, nchunk)

    def group(gi, carry):
        sls = [pl.ds(pl.multiple_of((gi * gsz + g) * CHUNK, CHUNK), CHUNK) for g in range(gsz)]
        tiles = range(r_ref.shape[1] // LANE)
        items = [(sl, slice(q * LANE, (q + 1) * LANE)) for sl in sls for q in tiles]
        for idx, ((sl, ls), part) in enumerate(zip(items, transitions(items))):
            advance(idx % len(tiles), sl, ls, *part)
        return carry

    lax.fori_loop(0, nchunk // gsz, group, 0)


def _rwkv_scan(r, lw, k, v, nk, kb, g, r_k, ln_w, ln_b, bsz, seq, tc=512):
    tc = min(tc, seq)
    ns = seq // tc
    width = RWKV_TILES * LANE
    npair = GROUP // width
    blk = pl.BlockSpec((tc, width), lambda b, p, i: (b * ns + i, p))
    vec = pl.BlockSpec((1, width), lambda b, p, i: (0, p))
    return pl.pallas_call(
        _rwkv_body,
        out_shape=jax.ShapeDtypeStruct((bsz * seq, GROUP), BF16),
        grid=(bsz, npair, ns),
        in_specs=[blk] * 7 + [vec] * 3,
        out_specs=blk,
        scratch_shapes=[pltpu.VMEM((RWKV_TILES, LANE, LANE), F32)],
        compiler_params=_cparams(("parallel", "parallel", "arbitrary")),
        name="rwkv_scan",
    )(r, lw, k, v, nk, kb, g, r_k.reshape(1, GROUP), ln_w.reshape(1, GROUP), ln_b.reshape(1, GROUP))


def _pad_rows(w, start):
    full = jnp.zeros((SMALL, w.shape[1]), F32).at[start:start + w.shape[0]].set(w)
    hi = full.astype(BF16)
    return jnp.stack([hi, (full - hi.astype(F32)).astype(BF16)])


def _layer(x, c8, layer_idx, v_first, vres, bias_ad, bias_as, bias_d, p, attn_tile):
    bsz, seq, d = x.shape
    mod = _ada(c8, p["ada_w"], p["ada_b"], layer_idx)
    mod3 = mod[:bsz].reshape(bsz * 6, 1, d)
    h = _norm_mod(x, mod3, 1, 0)

    vres_cols = vres[0] if vres is not None else jnp.zeros((d, 32), F32)
    w_r = _regroup_w_in(p["w_in"], layer_idx, vres_cols)
    cols, tail = _in_proj(h, w_r)

    qk, vt = _prep_a(cols, p["diff_q_norm"], p["diff_k_norm"])
    o_a = _diff_attention(qk, vt, bias_ad, bias_as, p["diff_lambda"], p["diff_subln"], bsz, seq, layer_idx,
                          attn_tile)
    o_dd = _swa_attention(cols, tail, bias_d, p["swa_q_norm"], p["swa_k_norm"], p["swa_sinks"], bsz, seq)
    mu = p["rwkv_mu"]
    vres_mu = vres[1] if vres is not None else jnp.zeros((32,), F32)
    mu_small = jnp.concatenate([jnp.zeros((S_WD,), F32), mu[3 * GROUP:], vres_mu,
                                jnp.zeros((SMALL - S_VR - 32,), F32)]).reshape(1, SMALL)
    v_up = _pad_rows(vres[2], S_VR) if vres is not None else jnp.zeros((2, SMALL, GROUP), BF16)
    v0 = vres[3] if vres is not None else jnp.zeros((GROUP,), F32)
    r_, lw_, k_, v_, nk_, kb_, g_, la_ = _prep_c(
        cols, tail, seq, mu[:3 * GROUP].reshape(1, -1), mu_small,
        _pad_rows(p["rwkv_w_up"], S_WD), _pad_rows(p["rwkv_a_up"], S_AD), _pad_rows(p["rwkv_g_up"], S_GD),
        v_up, _pad_rows(p["gla_gate_up"], S_GLA),
        p["rwkv_w0"].reshape(1, -1), p["rwkv_a0"].reshape(1, -1), v0.reshape(1, -1),
        p["rwkv_k_k"].reshape(1, -1), p["rwkv_k_a"].reshape(1, -1), p["gla_gate_bias"].reshape(1, -1),
        v_first if vres is not None else None)
    if vres is None:
        v_first = v_
    o_bb = _gla(cols, la_, p["gla_out_norm"], bsz, seq)
    o_c = _rwkv_scan(r_, lw_, k_, v_, nk_, kb_, g_, p["rwkv_r_k"].reshape(-1), p["rwkv_ln_w"], p["rwkv_ln_b"],
                     bsz, seq)

    x = _out_proj([o_a, o_bb, o_c, o_dd], p["w_out"], x, mod3, 2)
    h2 = _norm_mod(x, mod3, 4, 3)
    act = _ffn_up(h2, p["ffn_w1"], p["ffn_w3"], layer_idx)
    x = _ffn_down(act, p["ffn_w2"], x, mod3, 5)
    return x, v_first


def kernel(x, c, rel_bias, ada_w, ada_b, w_in, w_out, diff_q_norm, diff_k_norm, diff_lambda, diff_subln,
           gla_gate_up, gla_gate_bias, gla_out_norm, rwkv_mu, rwkv_w_up, rwkv_w0, rwkv_a_up, rwkv_a0,
           rwkv_g_up, rwkv_k_k, rwkv_k_a, rwkv_r_k, rwkv_ln_w, rwkv_ln_b, rwkv_vres_down, rwkv_vres_mu,
           rwkv_vres_up, rwkv_v0, swa_q_norm, swa_k_norm, swa_sinks, ffn_w1, ffn_w3, ffn_w2):
    bsz, seq, _ = x.shape
    depth = ada_w.shape[0]
    attn_tile = min(1024, seq)
    bias_ad = _bias_tiles(rel_bias[:, :DIFF_HEADS], attn_tile, attn_tile, 0, None, True, True, LOG2E)
    bias_as = _bias_tiles(rel_bias[:, :DIFF_HEADS], attn_tile, attn_tile, attn_tile, None, True, True, LOG2E)
    bias_d = _bias_tiles(rel_bias[:, DIFF_HEADS:], SWA_BLOCK, 2 * SWA_BLOCK, SWA_BLOCK, SWA_BLOCK, False)
    c8 = jnp.zeros((8, c.shape[1]), F32).at[:bsz].set(c)
    v_first = None
    for l in range(depth):
        p = dict(ada_w=ada_w, ada_b=ada_b, w_in=w_in, w_out=_cast_bf16(w_out, l),
                 diff_q_norm=diff_q_norm[l], diff_k_norm=diff_k_norm[l], diff_lambda=diff_lambda[l],
                 diff_subln=diff_subln[l], gla_gate_up=gla_gate_up[l], gla_gate_bias=gla_gate_bias[l],
                 gla_out_norm=gla_out_norm[l], rwkv_mu=rwkv_mu[l], rwkv_w_up=rwkv_w_up[l],
                 rwkv_w0=rwkv_w0[l], rwkv_a_up=rwkv_a_up[l], rwkv_a0=rwkv_a0[l], rwkv_g_up=rwkv_g_up[l],
                 rwkv_k_k=rwkv_k_k[l], rwkv_k_a=rwkv_k_a[l], rwkv_r_k=rwkv_r_k[l], rwkv_ln_w=rwkv_ln_w[l],
                 rwkv_ln_b=rwkv_ln_b[l], swa_q_norm=swa_q_norm[l], swa_k_norm=swa_k_norm[l],
                 swa_sinks=swa_sinks[l], ffn_w1=ffn_w1, ffn_w3=ffn_w3, ffn_w2=_cast_bf16(ffn_w2, l))
        vres = None if l == 0 else (rwkv_vres_down[l - 1], rwkv_vres_mu[l - 1], rwkv_vres_up[l - 1],
                                    rwkv_v0[l - 1])
        x, v_first = _layer(x, c8, l, v_first, vres, bias_ad, bias_as, bias_d, p, attn_tile)
    return x
```

```python
import functools
import math

import jax
import jax.numpy as jnp
from jax import lax
from jax.experimental import pallas as pl
from jax.experimental.pallas import tpu as pltpu

F32 = jnp.float32
BF16 = jnp.bfloat16

GROUP = 1024
NORM_EPS = 1e-6
NEG_INF = -1e30
LANE = 128
HALF = 64

DIFF_HEADS = 8
DIFF_QK = 64
GLA_HEADS = 4
GLA_DK = 128
GLA_DV = 256
GLA_NORMALIZER = 16.0
CHUNK = 64
SCAN_GROUP = 16
RWKV_GROUP = 8
RWKV_TILES = 2
PREV_ROWS = 16
RWKV_LN_EPS = 64e-5
SWA_HEADS = 16
SWA_KV_HEADS = 2
SWA_BLOCK = 128
REL_BUCKETS = 32
REL_MAX_DIST = 128

OFF_A = 0
OFF_B = 3072
OFF_C = 6144
OFF_D = 9216
OFF_S = 10496
SMALL = 256
NCOLS = OFF_S + SMALL
S_GLA, S_WD, S_AD, S_GD, S_VR = 0, 16, 80, 144, 208
TAIL = 512
OFF_T = NCOLS - TAIL

VMEM_LIMIT = 56 * 1024 * 1024


def _cparams(sem, vmem=VMEM_LIMIT):
    return pltpu.CompilerParams(dimension_semantics=sem, vmem_limit_bytes=vmem)


def _dot(a, b, prec=None):
    return jnp.dot(a, b, preferred_element_type=F32, precision=prec)


def _dot_nt(a, b, prec=None):
    return lax.dot_general(a, b, (((1,), (1,)), ((), ())), preferred_element_type=F32, precision=prec)


def _dot_tn(a, b, prec=None):
    return lax.dot_general(a, b, (((0,), (0,)), ((), ())), preferred_element_type=F32, precision=prec)


def _sigmoid(z):
    return 1.0 / (1.0 + jnp.exp(-z))


def _softplus(z):
    return jnp.maximum(z, 0.0) + jnp.log(1.0 + jnp.exp(-jnp.abs(z)))


def _seg_ones():
    r = lax.broadcasted_iota(jnp.int32, (LANE, LANE), 0) // HALF
    c = lax.broadcasted_iota(jnp.int32, (LANE, LANE), 1) // HALF
    return (r == c).astype(BF16)


def _split(x):
    hi = x.astype(BF16)
    return hi, (x - hi.astype(F32)).astype(BF16)


def _dot_left2(x, w):
    hi, lo = _split(x)
    return _dot(hi, w) + _dot(lo, w)


def _tri_cumsum(tri, x):
    hi, lo = _split(x)
    return _dot(tri, hi) + _dot(tri, lo)


def _segsum(x, ones):
    return _dot_left2(x, ones)


def _ada_body(c_ref, w_ref, b_ref, o_ref):
    c = c_ref[...]
    s = (c * _sigmoid(c)).astype(BF16)
    o_ref[...] = _dot(s, w_ref[...].astype(BF16)) + b_ref[...]


def _ada(c8, ada_w, ada_b, tn=512):
    depth, d, n = ada_w.shape
    return pl.pallas_call(
        _ada_body,
        out_shape=jax.ShapeDtypeStruct((depth, 8, n), F32),
        grid=(depth, n // tn),
        in_specs=[pl.BlockSpec((8, d), lambda l, j: (0, 0)),
                  pl.BlockSpec((None, d, tn), lambda l, j: (l, 0, j)),
                  pl.BlockSpec((None, 1, tn), lambda l, j: (l, 0, j))],
        out_specs=pl.BlockSpec((None, 8, tn), lambda l, j: (l, 0, j)),
        compiler_params=_cparams(("parallel", "parallel")),
        name="ada_mod",
    )(c8, ada_w, ada_b.reshape(depth, 1, n))


def _cast_body(w_ref, o_ref):
    o_ref[...] = w_ref[...].astype(o_ref.dtype)


def _cast_bf16(w, layer, tr=256):
    _, rows, cols = w.shape
    return pl.pallas_call(
        _cast_body,
        out_shape=jax.ShapeDtypeStruct((rows, cols), BF16),
        grid=(rows // tr,),
        in_specs=[pl.BlockSpec((None, tr, cols), lambda i: (layer, i, 0))],
        out_specs=pl.BlockSpec((tr, cols), lambda i: (i, 0)),
        compiler_params=_cparams(("parallel",)),
        name="weight_cast",
    )(w)


P_IN = 10704
SRC_B, SRC_C, SRC_D = 3072, 6160, 9424
SRC_BS, SRC_CS = SRC_B + 3072, SRC_C + 3072


PIECE = 16
PIECES = LANE // PIECE


def _piece_index(j, s):
    n_ab, n_c, n_d = OFF_C // LANE, OFF_D // LANE, OFF_S // LANE
    small0 = SRC_BS // PIECE if s == 0 else SRC_CS // PIECE + s - 1
    small1 = SRC_CS // PIECE + PIECES - 1 + min(s, 4)
    return jnp.where(j < n_ab, j * PIECES + s,
           jnp.where(j < n_c, SRC_C // PIECE + (j - n_ab) * PIECES + s,
           jnp.where(j < n_d, SRC_D // PIECE + (j - n_c) * PIECES + s,
           jnp.where(j == n_d, small0, small1))))


def _regroup_body(*refs):
    w_refs, vr_ref, o_ref = refs[:PIECES], refs[PIECES], refs[PIECES + 1]
    last = pl.program_id(0) == pl.num_programs(0) - 1
    pieces = [r[...] for r in w_refs]
    pieces[5] = jnp.where(last, vr_ref[:PIECE, :], pieces[5])
    pieces[6] = jnp.where(last, vr_ref[PIECE:, :], pieces[6])
    pieces[7] = jnp.where(last, 0.0, pieces[7])
    o_ref[...] = jnp.concatenate(pieces, axis=0).T.astype(o_ref.dtype)


def _regroup_w_in(w_in, layer, vres_down):
    _, d, _ = w_in.shape
    w_t = jnp.swapaxes(w_in, 1, 2)
    piece = lambda s: pl.BlockSpec((None, PIECE, d), lambda j: (layer, _piece_index(j, s), 0))
    return pl.pallas_call(
        _regroup_body,
        out_shape=jax.ShapeDtypeStruct((d, NCOLS), BF16),
        grid=(NCOLS // LANE,),
        in_specs=[piece(s) for s in range(PIECES)] + [pl.BlockSpec((2 * PIECE, d), lambda j: (0, 0))],
        out_specs=pl.BlockSpec((d, LANE), lambda j: (0, j)),
        compiler_params=_cparams(("parallel",)),
        name="w_in_regroup",
    )(*([w_t] * PIECES), vres_down.T)


def _norm_body(x_ref, sc_ref, sh_ref, o_ref):
    x = x_ref[0]
    ms = jnp.mean(x * x, axis=-1, keepdims=True)
    h = x * lax.rsqrt(ms + NORM_EPS) * (1.0 + sc_ref[0]) + sh_ref[0]
    o_ref[...] = h.astype(o_ref.dtype)


def _norm_mod(x, mod3, sc_idx, sh_idx, ts=256):
    bsz, seq, d = x.shape
    ts = min(ts, seq)
    ns = seq // ts
    return pl.pallas_call(
        _norm_body,
        out_shape=jax.ShapeDtypeStruct((bsz * seq, d), BF16),
        grid=(bsz, ns),
        in_specs=[pl.BlockSpec((1, ts, d), lambda b, i: (b, i, 0)),
                  pl.BlockSpec((1, 1, d), lambda b, i: (b * 6 + sc_idx, 0, 0)),
                  pl.BlockSpec((1, 1, d), lambda b, i: (b * 6 + sh_idx, 0, 0))],
        out_specs=pl.BlockSpec((ts, d), lambda b, i: (b * ns + i, 0)),
        compiler_params=_cparams(("parallel", "parallel")),
        name="norm_mod",
    )(x, mod3, mod3)


def _in_proj_body(a_ref, b_ref, main_ref, tail_ref):
    j = pl.program_id(1)
    acc = _dot(a_ref[...], b_ref[...])

    @pl.when(j < pl.num_programs(1) - 1)
    def _():
        main_ref[...] = acc.astype(main_ref.dtype)

    @pl.when(j == pl.num_programs(1) - 1)
    def _():
        tail_ref[...] = acc


def _in_proj(a, b, tm=2048):
    m, k = a.shape
    tm = min(tm, m)
    n_main = OFF_T // TAIL
    return pl.pallas_call(
        _in_proj_body,
        out_shape=[jax.ShapeDtypeStruct((m, OFF_T), BF16), jax.ShapeDtypeStruct((m, TAIL), F32)],
        grid=(m // tm, n_main + 1),
        in_specs=[pl.BlockSpec((tm, k), lambda i, j: (i, 0)),
                  pl.BlockSpec((k, TAIL), lambda i, j: (0, j))],
        out_specs=[pl.BlockSpec((tm, TAIL), lambda i, j: (i, jnp.minimum(j, n_main - 1))),
                   pl.BlockSpec((tm, TAIL), lambda i, j: (i, 0))],
        compiler_params=_cparams(("parallel", "arbitrary")),
        name="in_proj",
    )(a, b)


def _outproj_body(a0, a1, a2, a3, w_ref, x_ref, g_ref, o_ref):
    acc = _dot(a0[...], w_ref[0 * GROUP:1 * GROUP, :])
    acc += _dot(a1[...], w_ref[1 * GROUP:2 * GROUP, :])
    acc += _dot(a2[...], w_ref[2 * GROUP:3 * GROUP, :])
    acc += _dot(a3[...], w_ref[3 * GROUP:4 * GROUP, :])
    o_ref[0] = x_ref[0] + g_ref[0] * acc


def _out_proj(parts, w, x, mod3, gate_idx, tm=1024, tn=512):
    bsz, seq, d = x.shape
    tm = min(tm, seq)
    ns = seq // tm
    a_spec = pl.BlockSpec((tm, GROUP), lambda b, i, j: (b * ns + i, 0))
    return pl.pallas_call(
        _outproj_body,
        out_shape=jax.ShapeDtypeStruct((bsz, seq, d), F32),
        grid=(bsz, ns, d // tn),
        in_specs=[a_spec, a_spec, a_spec, a_spec,
                  pl.BlockSpec((4 * GROUP, tn), lambda b, i, j: (0, j)),
                  pl.BlockSpec((1, tm, tn), lambda b, i, j: (b, i, j)),
                  pl.BlockSpec((1, 1, tn), lambda b, i, j: (b * 6 + gate_idx, 0, j))],
        out_specs=pl.BlockSpec((1, tm, tn), lambda b, i, j: (b, i, j)),
        compiler_params=_cparams(("parallel", "parallel", "parallel")),
        name="out_proj",
    )(*parts, w, x, mod3)


def _ffn_up_body(a_ref, w1_ref, w3_ref, o_ref):
    a = a_ref[...]
    u = _dot(a, w1_ref[...].astype(BF16))
    v = _dot(a, w3_ref[...].astype(BF16))
    o_ref[...] = (u * _sigmoid(u) * v).astype(o_ref.dtype)


def _ffn_up(h, w1, w3, layer, tm=1024, tn=256):
    m, k = h.shape
    n = w1.shape[2]
    tm = min(tm, m)
    return pl.pallas_call(
        _ffn_up_body,
        out_shape=jax.ShapeDtypeStruct((m, n), BF16),
        grid=(m // tm, n // tn),
        in_specs=[pl.BlockSpec((tm, k), lambda i, j: (i, 0)),
                  pl.BlockSpec((None, k, tn), lambda i, j: (layer, 0, j)),
                  pl.BlockSpec((None, k, tn), lambda i, j: (layer, 0, j))],
        out_specs=pl.BlockSpec((tm, tn), lambda i, j: (i, j)),
        compiler_params=_cparams(("parallel", "parallel")),
        name="ffn_up",
    )(h, w1, w3)


def _ffn_down_body(a_ref, w_ref, x_ref, g_ref, o_ref):
    o_ref[0] = x_ref[0] + g_ref[0] * _dot(a_ref[...], w_ref[...])


def _ffn_down(a, w, x, mod3, gate_idx, tm=512, tn=512):
    bsz, seq, d = x.shape
    k = a.shape[1]
    tm = min(tm, seq)
    ns = seq // tm
    return pl.pallas_call(
        _ffn_down_body,
        out_shape=jax.ShapeDtypeStruct((bsz, seq, d), F32),
        grid=(bsz, ns, d // tn),
        in_specs=[pl.BlockSpec((tm, k), lambda b, i, j: (b * ns + i, 0)),
                  pl.BlockSpec((k, tn), lambda b, i, j: (0, j)),
                  pl.BlockSpec((1, tm, tn), lambda b, i, j: (b, i, j)),
                  pl.BlockSpec((1, 1, tn), lambda b, i, j: (b * 6 + gate_idx, 0, j))],
        out_specs=pl.BlockSpec((1, tm, tn), lambda b, i, j: (b, i, j)),
        compiler_params=_cparams(("parallel", "parallel", "parallel")),
        name="ffn_down",
    )(a, w, x, mod3)


def _bias_body(tab_ref, o_ref, *, off, window, shift, key_major, scale):
    h = pl.program_id(0)
    rows, cols = o_ref.shape[1], o_ref.shape[2]
    last = tab_ref[REL_BUCKETS - 1, h]
    max_exact = REL_BUCKETS // 2

    def block(d0):
        qi = lax.broadcasted_iota(jnp.int32, (LANE, LANE), 1 if key_major else 0)
        kj = lax.broadcasted_iota(jnp.int32, (LANE, LANE), 0 if key_major else 1)
        dist = d0 + qi - kj
        n = jnp.maximum(dist, 0)
        nf = jnp.maximum(n, 1).astype(F32)
        large = max_exact + (jnp.log(nf / max_exact) / math.log(REL_MAX_DIST / max_exact)
                             * (REL_BUCKETS - max_exact)).astype(jnp.int32)
        large = jnp.minimum(large, REL_BUCKETS - 1)
        bucket = jnp.where(n < max_exact, n, large)
        bias = jnp.full((LANE, LANE), last, F32)
        for b in range(REL_BUCKETS - 1):
            bias = jnp.where(bucket == b, tab_ref[b, h], bias)
        if shift:
            bias = bias - last
        if scale != 1.0:
            bias = bias * scale
        valid = dist >= 0
        if window is not None:
            valid = jnp.logical_and(valid, dist < window)
        return jnp.where(valid, bias, NEG_INF)

    cache = {}
    for br in range(rows // LANE):
        for bc in range(cols // LANE):
            qb, kb = (bc, br) if key_major else (br, bc)
            d0 = off + LANE * (qb - kb)
            if d0 not in cache:
                if d0 + LANE <= 0 or (window is not None and d0 - LANE >= window):
                    cache[d0] = jnp.full((LANE, LANE), NEG_INF, F32)
                elif d0 - LANE >= REL_MAX_DIST and (window is None or d0 + LANE <= window):
                    const = (0.0 if shift else last) * scale
                    cache[d0] = jnp.full((LANE, LANE), const, F32)
                else:
                    cache[d0] = block(d0)
            o_ref[0, br * LANE:(br + 1) * LANE, bc * LANE:(bc + 1) * LANE] = cache[d0]


def _bias_tiles(table, rows, cols, off, window, shift, key_major=False, scale=1.0):
    nh = table.shape[1]
    return pl.pallas_call(
        functools.partial(_bias_body, off=off, window=window, shift=shift, key_major=key_major, scale=scale),
        out_shape=jax.ShapeDtypeStruct((nh, rows, cols), F32),
        grid=(nh,),
        in_specs=[pl.BlockSpec(memory_space=pltpu.SMEM)],
        out_specs=pl.BlockSpec((1, rows, cols), lambda h: (h, 0, 0)),
        compiler_params=_cparams(("parallel",)),
        name="rel_bias_tiles",
    )(table)


LOG2E = 1.4426950408889634
VT_ROWS = LANE + 16
Q_CHUNK = 512


def _prep_a_body(c_ref, qg_ref, kg_ref, qk_ref, vt_ref):
    ones = _seg_ones()
    scale = DIFF_QK ** -0.5 * LOG2E
    tm = c_ref.shape[0]
    for j in range(2 * DIFF_HEADS):
        x = c_ref[:, j * LANE:(j + 1) * LANE].astype(F32)
        ms = _segsum(x * x, ones) * (1.0 / DIFF_QK)
        gain = qg_ref[...] * scale if j < DIFF_HEADS else kg_ref[...]
        qk_ref[:, j * LANE:(j + 1) * LANE] = (x * lax.rsqrt(ms + NORM_EPS) * gain).astype(qk_ref.dtype)
    for h in range(DIFF_HEADS):
        v = c_ref[:, 2 * GROUP + h * LANE:2 * GROUP + (h + 1) * LANE].astype(F32)
        vt_ref[h * VT_ROWS:h * VT_ROWS + LANE, :] = v.T.astype(vt_ref.dtype)
        vt_ref[h * VT_ROWS + LANE:(h + 1) * VT_ROWS, :] = jnp.ones((VT_ROWS - LANE, tm), vt_ref.dtype)


def _prep_a(cols, q_gain, k_gain, tm=256):
    t = cols.shape[0]
    tm = min(tm, t)
    qg = jnp.tile(q_gain, 2).reshape(1, LANE)
    kg = jnp.tile(k_gain, 2).reshape(1, LANE)
    vec = pl.BlockSpec((1, LANE), lambda i: (0, 0))
    return pl.pallas_call(
        _prep_a_body,
        out_shape=[jax.ShapeDtypeStruct((t, 2 * GROUP), BF16),
                   jax.ShapeDtypeStruct((DIFF_HEADS * VT_ROWS, t), BF16)],
        grid=(t // tm,),
        in_specs=[pl.BlockSpec((tm, 3 * GROUP), lambda i: (i, OFF_A // (3 * GROUP))), vec, vec],
        out_specs=[pl.BlockSpec((tm, 2 * GROUP), lambda i: (i, 0)),
                   pl.BlockSpec((DIFF_HEADS * VT_ROWS, tm), lambda i: (0, i))],
        compiler_params=_cparams(("parallel",)),
        name="diff_prep",
    )(cols, qg, kg)


def _flash_body(it_ref, jt_ref, q_ref, k_ref, vt_ref, vtp_ref, bd_ref, bs_ref, lam_ref, sub_ref, o_ref,
                qlo, qhi, m1, a1, m2, a2, p_scr, al_scr, *, lam_init):
    t = pl.program_id(2)
    i = it_ref[t]
    j = jt_ref[t]

    @pl.when(j == 0)
    def _():
        q = q_ref[...]
        lane = lax.broadcasted_iota(jnp.int32, q.shape, 1)
        zero = jnp.zeros_like(q)
        qlo[...] = jnp.where(lane < HALF, q, zero)
        qhi[...] = jnp.where(lane >= HALF, q, zero)
        for m, a in ((m1, a1), (m2, a2)):
            m[...] = jnp.full(m.shape, NEG_INF, F32)
            a[...] = jnp.zeros(a.shape, F32)
        al_scr[...] = jnp.ones(al_scr.shape, F32)

    @pl.when(t == 0)
    def _():
        p_scr[...] = jnp.zeros(p_scr.shape, BF16)

    pending = (j > 0).astype(F32)

    def step(bias, last):
        k = k_ref[...]
        vt = vt_ref[...]
        vtp = vtp_ref[...]
        nq = q_ref.shape[0]
        qc = min(Q_CHUNK, nq)
        chains = [(mi, qq, m, a, slice(c * qc, (c + 1) * qc))
                  for mi, (qq, m, a) in enumerate(((qlo, m1, a1), (qhi, m2, a2))) for c in range(nq // qc)]
        ahead = 2
        kend = [qs.stop if last else k.shape[0] for _, _, _, _, qs in chains]
        if bias is None:
            scores = [_dot_nt(k, qq[qs, :]) for _, qq, _, _, qs in chains[:ahead]]
        elif last:
            scores = [_dot_nt(k[:ke, :], c[1][c[4], :]) + bias[:ke, c[4]] for c, ke in zip(chains, kend)]
        else:
            full = [_dot_nt(k, qq[...]) for qq in (qlo, qhi)]
            scores = [full[c[0]][:, c[4]] + bias[:, c[4]] for c in chains]
        for idx, (mi, _, m, a, qs) in enumerate(chains):
            if bias is None and idx + ahead < len(chains):
                _, qq_n, _, _, qs_n = chains[idx + ahead]
                scores.append(_dot_nt(k, qq_n[qs_n, :]))
            acc = al_scr[mi, :, qs] * a[:, qs] + pending * _dot(vtp, p_scr[mi, :, qs])
            s = scores[idx]
            m_old = m[:, qs]
            m_new = jnp.maximum(m_old, jnp.max(s, axis=0, keepdims=True))
            p = jnp.exp2(s - m_new).astype(BF16)
            alpha = jnp.exp2(m_old - m_new)
            if last:
                acc = alpha * acc + _dot(vt[:, :kend[idx]], p)
            else:
                p_scr[mi, :, qs] = p
                al_scr[mi, :, qs] = alpha
            a[:, qs] = acc
            m[:, qs] = m_new

    @pl.when(j < i - 1)
    def _():
        step(None, False)

    @pl.when(j == i - 1)
    def _():
        step(bs_ref[0], False)

    @pl.when(j == i)
    def _():
        step(bd_ref[0], True)
        lam = lam_ref[...]
        e1 = jnp.exp(jnp.sum(lam[0:1] * lam[1:2], axis=-1, keepdims=True))
        e2 = jnp.exp(jnp.sum(lam[2:3] * lam[3:4], axis=-1, keepdims=True))
        lam_full = e1 - e2 + lam_init
        o = a1[:LANE, :] / a1[LANE:LANE + 1, :] - lam_full * (a2[:LANE, :] / a2[LANE:LANE + 1, :])
        ms = jnp.mean(o * o, axis=0, keepdims=True)
        o = o * lax.rsqrt(ms + NORM_EPS) * sub_ref[...] * (1.0 - lam_init)
        o_ref[...] = o.T.astype(o_ref.dtype)


def _diff_attention(qk, vt, bias_diag, bias_sub, lam, subln, bsz, seq, layer_idx, tile):
    nq = seq // tile
    lam_init = 0.8 - 0.6 * math.exp(-0.3 * layer_idx)
    pairs = [(i, j) for i in range(nq) for j in range(i + 1)]
    i_tab = jnp.array([ij[0] for ij in pairs], jnp.int32)
    j_tab = jnp.array([ij[1] for ij in pairs], jnp.int32)
    grid_spec = pltpu.PrefetchScalarGridSpec(
        num_scalar_prefetch=2,
        grid=(bsz, DIFF_HEADS, len(pairs)),
        in_specs=[pl.BlockSpec((tile, LANE), lambda b, h, t, it, jt: (b * nq + it[t], h)),
                  pl.BlockSpec((tile, LANE), lambda b, h, t, it, jt: (b * nq + jt[t], DIFF_HEADS + h)),
                  pl.BlockSpec((VT_ROWS, tile), lambda b, h, t, it, jt: (h, b * nq + jt[t])),
                  pl.BlockSpec((VT_ROWS, tile), lambda b, h, t, it, jt: (h, b * nq + jnp.maximum(jt[t] - 1, 0))),
                  pl.BlockSpec((1, tile, tile), lambda b, h, t, it, jt: (h, 0, 0)),
                  pl.BlockSpec((1, tile, tile), lambda b, h, t, it, jt: (h, 0, 0)),
                  pl.BlockSpec((4, DIFF_QK), lambda b, h, t, it, jt: (0, 0)),
                  pl.BlockSpec((LANE, 1), lambda b, h, t, it, jt: (0, 0))],
        out_specs=pl.BlockSpec((tile, LANE), lambda b, h, t, it, jt: (b * nq + it[t], h)),
        scratch_shapes=[pltpu.VMEM((tile, LANE), BF16), pltpu.VMEM((tile, LANE), BF16),
                        pltpu.VMEM((1, tile), F32), pltpu.VMEM((VT_ROWS, tile), F32),
                        pltpu.VMEM((1, tile), F32), pltpu.VMEM((VT_ROWS, tile), F32),
                        pltpu.VMEM((2, tile, tile), BF16), pltpu.VMEM((2, 1, tile), F32)])
    return pl.pallas_call(
        functools.partial(_flash_body, lam_init=lam_init),
        out_shape=jax.ShapeDtypeStruct((bsz * seq, GROUP), BF16),
        grid_spec=grid_spec,
        compiler_params=_cparams(("parallel", "parallel", "arbitrary")),
        name="diff_attention",
    )(i_tab, j_tab, qk, qk, vt, vt, bias_diag, bias_sub, lam, subln.reshape(LANE, 1))


def _swa_body(q_ref, kp_ref, kc_ref, vp_ref, vc_ref, bias_ref, qg_ref, kg_ref, sink_ref, o_ref):
    n = pl.program_id(1)
    g = pl.program_id(2)
    ones = _seg_ones()
    lane = lax.broadcasted_iota(jnp.int32, (SWA_BLOCK, LANE), 1)
    lo = lane < HALF

    def norm(x, gain):
        ms = _segsum(x * x, ones) * (1.0 / HALF)
        return x * lax.rsqrt(ms + NORM_EPS) * gain

    k = jnp.concatenate([norm(kp_ref[...], kg_ref[...]), norm(kc_ref[...], kg_ref[...])], axis=0)
    v = jnp.concatenate([vp_ref[...], vc_ref[...]], axis=0)
    k_sw = pltpu.roll(k, HALF, axis=1)
    v_sw = pltpu.roll(v, HALF, axis=1)
    first_head = g == 0
    lo2 = lax.broadcasted_iota(jnp.int32, (2 * SWA_BLOCK, LANE), 1) < HALF
    k_a = jnp.where(first_head, k, k_sw).astype(BF16)
    k_b = jnp.where(first_head, k_sw, k).astype(BF16)
    v_mine_lo = jnp.where(first_head, v, v_sw)
    v_mine_hi = jnp.where(first_head, v_sw, v)
    v_a = jnp.where(lo2, v_mine_lo, 0.0).astype(BF16)
    v_b = jnp.where(lo2, 0.0, v_mine_hi).astype(BF16)
    kcol = lax.broadcasted_iota(jnp.int32, (SWA_BLOCK, 2 * SWA_BLOCK), 1)
    pad = jnp.logical_and(n == 0, kcol < SWA_BLOCK)
    scale = HALF ** -0.5
    grp = SWA_HEADS // SWA_KV_HEADS
    heads = range(grp)
    q = [norm(q_ref[:, pr * LANE:(pr + 1) * LANE].astype(F32), qg_ref[...] * scale) for pr in range(grp // 2)]
    qm = [jnp.where(lo if hh % 2 == 0 else jnp.logical_not(lo), q[hh // 2], 0.0).astype(BF16) for hh in heads]
    s = [_dot_nt(qm[hh], k_a if hh % 2 == 0 else k_b) for hh in heads]
    s = [jnp.where(pad, NEG_INF, s[hh] + bias_ref[hh]) for hh in heads]
    sink = [sink_ref[g * grp + hh] for hh in heads]
    m = [jnp.maximum(jnp.max(s[hh], axis=-1, keepdims=True), sink[hh]) for hh in heads]
    p = [jnp.exp(s[hh] - m[hh]) for hh in heads]
    den = [jnp.sum(p[hh], axis=-1, keepdims=True) + jnp.exp(sink[hh] - m[hh]) for hh in heads]
    pv = [_dot(p[hh].astype(BF16), v_a if hh % 2 == 0 else v_b) for hh in heads]
    for pr in range(grp // 2):
        o = pv[2 * pr] / den[2 * pr] + pv[2 * pr + 1] / den[2 * pr + 1]
        o_ref[:, pr * LANE:(pr + 1) * LANE] = o.astype(o_ref.dtype)


def _swa_attention(cols, tail, bias, q_gain, k_gain, sinks, bsz, seq):
    nb = seq // SWA_BLOCK
    grp = SWA_HEADS // SWA_KV_HEADS
    qw = grp * HALF
    kcol = (OFF_D + GROUP - OFF_T) // LANE
    vcol = kcol + 1
    prev = lambda b, n, g: b * nb + jnp.maximum(n - 1, 0)
    cur = lambda b, n, g: b * nb + n
    vec = pl.BlockSpec((1, LANE), lambda b, n, g: (0, 0))
    return pl.pallas_call(
        _swa_body,
        out_shape=jax.ShapeDtypeStruct((bsz * seq, GROUP), BF16),
        grid=(bsz, nb, SWA_KV_HEADS),
        in_specs=[pl.BlockSpec((SWA_BLOCK, qw), lambda b, n, g: (cur(b, n, g), OFF_D // qw + g)),
                  pl.BlockSpec((SWA_BLOCK, LANE), lambda b, n, g: (prev(b, n, g), kcol)),
                  pl.BlockSpec((SWA_BLOCK, LANE), lambda b, n, g: (cur(b, n, g), kcol)),
                  pl.BlockSpec((SWA_BLOCK, LANE), lambda b, n, g: (prev(b, n, g), vcol)),
                  pl.BlockSpec((SWA_BLOCK, LANE), lambda b, n, g: (cur(b, n, g), vcol)),
                  pl.BlockSpec((grp, SWA_BLOCK, 2 * SWA_BLOCK), lambda b, n, g: (g, 0, 0)),
                  vec, vec,
                  pl.BlockSpec(memory_space=pltpu.SMEM)],
        out_specs=pl.BlockSpec((SWA_BLOCK, qw), lambda b, n, g: (cur(b, n, g), g)),
        compiler_params=_cparams(("parallel", "parallel", "parallel")),
        name="swa_attention",
    )(cols, tail, tail, tail, tail, bias,
      jnp.tile(q_gain, 2).reshape(1, LANE), jnp.tile(k_gain, 2).reshape(1, LANE), sinks)


def _prep_c_body(*refs, seq, with_vres):
    (cm_ref, cs_ref, pm_ref, ps_ref, mum_ref, mus_ref, wup_ref, aup_ref, gup_ref, vup_ref, glaup_ref,
     w0_ref, a0_ref, v0_ref, kk_ref, ka_ref, glab_ref) = refs[:17]
    rest = refs[17:]
    if with_vres:
        vf_ref, rest = rest[0], rest[1:]
    r_ref, lw_ref, k_ref, v_ref, nk_ref, kb_ref, g_ref, la_ref = rest
    tm = cm_ref.shape[0]
    i = pl.program_id(0)
    seq_start = (i * tm) % seq == 0

    def shifted(cur, prev_rows, mu):
        row = lax.broadcasted_iota(jnp.int32, cur.shape, 0)
        before = jnp.where(seq_start, 0.0, prev_rows[PREV_ROWS - 1:PREV_ROWS, :].astype(F32))
        prev = jnp.where(row == 0, before, pltpu.roll(cur, 1, axis=0))
        return cur + (prev - cur) * mu

    def low_rank(x, w2_ref):
        hi, lo = _split(x)
        return _dot(hi, w2_ref[0]) + _dot(lo, w2_ref[0]) + _dot(hi, w2_ref[1])

    cs = cs_ref[...]
    la_ref[...] = -_softplus(-(low_rank(cs, glaup_ref) + glab_ref[...])) * (1.0 / GLA_NORMALIZER)
    sm = shifted(cm_ref[...].astype(F32), pm_ref, mum_ref[...])
    ss = shifted(cs, ps_ref, mus_ref[...])
    r = sm[:, :GROUP]
    k = sm[:, GROUP:2 * GROUP]
    v = sm[:, 2 * GROUP:]
    if with_vres:
        gate = _sigmoid(v0_ref[...] + low_rank(ss, vup_ref))
        v = v + (vf_ref[...] - v) * gate
    w_log = -_softplus(-(w0_ref[...] + low_rank(jnp.tanh(ss), wup_ref))) - 0.5
    a = _sigmoid(a0_ref[...] + low_rank(ss, aup_ref))
    r_ref[...] = r.astype(r_ref.dtype)
    lw_ref[...] = -jnp.exp(w_log)
    v_ref[...] = v
    g_ref[...] = low_rank(_sigmoid(ss), gup_ref).astype(g_ref.dtype)
    k_ref[...] = (k * (1.0 + (a - 1.0) * ka_ref[...])).astype(k_ref.dtype)
    ones = _seg_ones()
    kk = k * kk_ref[...]
    for j in range(GROUP // LANE):
        sl = slice(j * LANE, (j + 1) * LANE)
        x = kk[:, sl]
        nrm = jnp.maximum(jnp.sqrt(_segsum(x * x, ones)), 1e-12)
        x = x / nrm
        nk_ref[:, sl] = x.astype(nk_ref.dtype)
        kb_ref[:, sl] = (x * a[:, sl]).astype(kb_ref.dtype)


def _prep_c(cols, tail, seq, mu_main, mu_small, w_up, a_up, g_up, v_up, gla_up, w0, a0, v0, k_k, k_a, gla_bias,
            v_first, tm=256):
    t = cols.shape[0]
    tm = min(tm, seq)
    with_vres = v_first is not None
    cmain = OFF_C // (3 * GROUP)
    csmall = (OFF_S - OFF_T) // SMALL
    prev = lambda i: jnp.maximum(i * (tm // PREV_ROWS) - 1, 0)
    full = lambda r, c: pl.BlockSpec((r, c), lambda i: (0, 0))
    hilo = lambda c: pl.BlockSpec((2, SMALL, c), lambda i: (0, 0, 0))
    in_specs = [pl.BlockSpec((tm, 3 * GROUP), lambda i: (i, cmain)),
                pl.BlockSpec((tm, SMALL), lambda i: (i, csmall)),
                pl.BlockSpec((PREV_ROWS, 3 * GROUP), lambda i: (prev(i), cmain)),
                pl.BlockSpec((PREV_ROWS, SMALL), lambda i: (prev(i), csmall)),
                full(1, 3 * GROUP), full(1, SMALL),
                hilo(GROUP), hilo(GROUP), hilo(GROUP), hilo(GROUP), hilo(GLA_HEADS * GLA_DK),
                full(1, GROUP), full(1, GROUP), full(1, GROUP), full(1, GROUP), full(1, GROUP),
                full(1, GLA_HEADS * GLA_DK)]
    args = [cols, tail, cols, tail, mu_main, mu_small, w_up, a_up, g_up, v_up, gla_up,
            w0, a0, v0, k_k, k_a, gla_bias]
    if with_vres:
        in_specs.append(pl.BlockSpec((tm, GROUP), lambda i: (i, 0)))
        args.append(v_first)
    row = pl.BlockSpec((tm, GROUP), lambda i: (i, 0))
    out = [jax.ShapeDtypeStruct((t, GROUP), dt) for dt in (BF16, F32, BF16, F32, BF16, BF16, BF16)]
    return pl.pallas_call(
        functools.partial(_prep_c_body, seq=seq, with_vres=with_vres),
        out_shape=out + [jax.ShapeDtypeStruct((t, GLA_HEADS * GLA_DK), F32)],
        grid=(t // tm,),
        in_specs=in_specs,
        out_specs=[row] * 7 + [pl.BlockSpec((tm, GLA_HEADS * GLA_DK), lambda i: (i, 0))],
        compiler_params=_cparams(("parallel",)),
        name="rwkv_gla_prep",
    )(*args)


def _gla_body(q_ref, k_ref, v_ref, g_ref, la_ref, gain_ref, o_ref, st_ref):
    @pl.when(pl.program_id(2) == 0)
    def _():
        st_ref[...] = jnp.zeros(st_ref.shape, F32)

    row = lax.broadcasted_iota(jnp.int32, (CHUNK, CHUNK), 0)
    col = lax.broadcasted_iota(jnp.int32, (CHUNK, CHUNK), 1)
    causal = row >= col
    tri = causal.astype(BF16)
    nchunk = q_ref.shape[0] // CHUNK

    def local(sls):
        n = range(len(sls))
        k = [k_ref[sl, :].astype(F32) for sl in sls]
        v = [v_ref[sl, :].astype(BF16) for sl in sls]
        b = [_tri_cumsum(tri, la_ref[sl, :]) for sl in sls]
        b_last = [b[i][CHUNK - 1:CHUNK, :] for i in n]
        q_dec = [(q_ref[sl, :].astype(F32) * (GLA_DK ** -0.5) * jnp.exp(b[i])).astype(BF16)
                 for i, sl in enumerate(sls)]
        a_intra = [jnp.where(causal, _dot_nt(q_dec[i], (k[i] * jnp.exp(-b[i])).astype(BF16)), 0.0) for i in n]
        upd = [_dot_tn(v[i], (k[i] * jnp.exp(b_last[i] - b[i])).astype(BF16)) for i in n]
        o_intra = [_dot(a_intra[i].astype(BF16), v[i]) for i in n]
        return [(q_dec[i], o_intra[i], upd[i], jnp.exp(b_last[i])) for i in n]

    def advance(sl, q_dec, o_intra, upd, dec):
        state = st_ref[...]
        s_hi, s_lo = _split(state)
        o = o_intra + _dot_nt(q_dec, s_hi) + _dot_nt(q_dec, s_lo)
        st_ref[...] = state * dec + upd
        ms = jnp.mean(o * o, axis=-1, keepdims=True)
        o = o * lax.rsqrt(ms + NORM_EPS) * gain_ref[...]
        gate = g_ref[sl, :].astype(F32)
        o_ref[sl, :] = (o * (gate * _sigmoid(gate))).astype(o_ref.dtype)

    gsz = min(SCAN_GROUP, nchunk)

    def group(gi, carry):
        sls = [pl.ds(pl.multiple_of((gi * gsz + g) * CHUNK, CHUNK), CHUNK) for g in range(gsz)]
        for sl, part in zip(sls, local(sls)):
            advance(sl, *part)
        return carry

    lax.fori_loop(0, nchunk // gsz, group, 0)


def _gla(cols, log_a, out_gain, bsz, seq, tc=1024):
    tc = min(tc, seq)
    ns = seq // tc
    qc = OFF_B // GLA_DK
    kc = qc + GLA_HEADS
    vc = (OFF_B + 2 * GLA_HEADS * GLA_DK) // GLA_DV
    gc = vc + GLA_HEADS
    rows = lambda b, h, i: b * ns + i
    return pl.pallas_call(
        _gla_body,
        out_shape=jax.ShapeDtypeStruct((bsz * seq, GROUP), BF16),
        grid=(bsz, GLA_HEADS, ns),
        in_specs=[pl.BlockSpec((tc, GLA_DK), lambda b, h, i: (rows(b, h, i), qc + h)),
                  pl.BlockSpec((tc, GLA_DK), lambda b, h, i: (rows(b, h, i), kc + h)),
                  pl.BlockSpec((tc, GLA_DV), lambda b, h, i: (rows(b, h, i), vc + h)),
                  pl.BlockSpec((tc, GLA_DV), lambda b, h, i: (rows(b, h, i), gc + h)),
                  pl.BlockSpec((tc, GLA_DK), lambda b, h, i: (rows(b, h, i), h)),
                  pl.BlockSpec((1, GLA_DV), lambda b, h, i: (0, 0))],
        out_specs=pl.BlockSpec((tc, GLA_DV), lambda b, h, i: (rows(b, h, i), h)),
        scratch_shapes=[pltpu.VMEM((GLA_DV, GLA_DK), F32)],
        compiler_params=_cparams(("parallel", "parallel", "arbitrary")),
        name="gla_scan",
    )(cols, cols, cols, cols, log_a, out_gain.reshape(1, GLA_DV))


def _rwkv_body(r_ref, lw_ref, k_ref, v_ref, nk_ref, kb_ref, g_ref, rk_ref, lnw_ref, lnb_ref, o_ref, st_ref):
    @pl.when(pl.program_id(2) == 0)
    def _():
        st_ref[...] = jnp.zeros(st_ref.shape, F32)

    two = 2 * CHUNK
    row = lax.broadcasted_iota(jnp.int32, (two, two), 0)
    col = lax.broadcasted_iota(jnp.int32, (two, two), 1)
    same = (row // CHUNK) == (col // CHUNK)
    strict = jnp.logical_and(same, (row % CHUNK) > (col % CHUNK))
    incl = jnp.logical_and(same, (row % CHUNK) >= (col % CHUNK))
    eye = (row == col).astype(F32)
    crow = lax.broadcasted_iota(jnp.int32, (CHUNK, CHUNK), 0)
    ccol = lax.broadcasted_iota(jnp.int32, (CHUNK, CHUNK), 1)
    tri = (crow >= ccol).astype(BF16)
    lo = lax.broadcasted_iota(jnp.int32, (CHUNK, LANE), 1) < HALF
    ones = _seg_ones()
    nchunk = r_ref.shape[0] // CHUNK

    def stack(x):
        return jnp.concatenate([jnp.where(lo, x, 0.0), jnp.where(lo, 0.0, x)], axis=0)

    def fold(x):
        return x[:CHUNK, :] + x[CHUNK:, :]

    def transitions(items):
        n = range(len(items))
        r = [r_ref[sl, ls].astype(F32) for sl, ls in items]
        lw = [lw_ref[sl, ls] for sl, ls in items]
        k = [k_ref[sl, ls].astype(F32) for sl, ls in items]
        v = [v_ref[sl, ls] for sl, ls in items]
        cum = [_tri_cumsum(tri, lw[i]) for i in n]
        last = [cum[i][CHUNK - 1:CHUNK, :] for i in n]
        a_bf, r_st, bk_st, bkh_st, v_st = [], [], [], [], []
        for i, (sl, ls) in enumerate(items):
            kb = kb_ref[sl, ls].astype(F32)
            e_neg = jnp.exp(-cum[i])
            e_rem = jnp.exp(last[i] - cum[i])
            a_bf.append(stack(-nk_ref[sl, ls].astype(F32) * jnp.exp(cum[i] - lw[i])).astype(BF16))
            r_st.append(stack(r[i] * jnp.exp(cum[i])))
            bk_st.append(jnp.concatenate([stack(kb * e_neg), stack(k[i] * e_neg)], axis=0).astype(BF16))
            bkh_st.append(jnp.concatenate([stack(kb * e_rem), stack(k[i] * e_rem)], axis=0).astype(BF16))
            v_st.append(stack(v[i]).astype(BF16))
        sc = [_dot_nt(jnp.concatenate([a_bf[i], r_st[i].astype(BF16)], axis=0), bk_st[i]) for i in n]
        a_ab = [jnp.where(strict, sc[i][:two, :two], 0.0) for i in n]
        a_ak = [jnp.where(strict, sc[i][:two, two:], 0.0).astype(BF16) for i in n]
        r_b = [jnp.where(incl, sc[i][two:, :two], 0.0).astype(BF16) for i in n]
        r_k = [jnp.where(incl, sc[i][two:, two:], 0.0).astype(BF16) for i in n]
        akv = [_dot(a_ak[i], v_st[i]).astype(BF16) for i in n]
        rkv = [_dot(r_k[i], v_st[i]) for i in n]
        inv = [eye + a_ab[i] for i in n]
        pw = [a_ab[i].astype(BF16) for i in n]
        for _ in range(5):
            pw = [_dot(pw[i], pw[i]).astype(BF16) for i in n]
            inv = [inv[i] + _dot(inv[i].astype(BF16), pw[i]) for i in n]
        pu = [_dot(inv[i].astype(BF16), jnp.concatenate([a_bf[i], akv[i]], axis=1)).astype(BF16) for i in n]
        rb_pu = [_dot(r_b[i], pu[i]) for i in n]
        m = [_dot_tn(pu[i][:, :LANE], bkh_st[i][:two, :]).astype(BF16) for i in n]
        n0 = [_dot_tn(jnp.concatenate([pu[i][:, LANE:], v_st[i]], axis=0), bkh_st[i]) for i in n]
        rkr = [_segsum(r[i] * k[i] * rk_ref[:, items[i][1]], ones) for i in n]
        return [(fold(r_st[i] + rb_pu[i][:, :LANE]).astype(BF16), fold(rb_pu[i][:, LANE:] + rkv[i]),
                 m[i], n0[i], jnp.exp(last[i]), rkr[i] * v[i]) for i in n]

    def advance(tile, sl, ls, p2, y0, m, n0, dec, bonus):
        state = st_ref[tile]
        s_hi, s_lo = _split(state)
        y = _dot_nt(p2, s_hi) + y0
        st_ref[tile] = state * dec + _dot(s_hi, m) + _dot(s_lo, m) + n0
        d = y - _segsum(y, ones) * (1.0 / HALF)
        var = _segsum(d * d, ones) * (1.0 / HALF)
        y = d * lax.rsqrt(var + RWKV_LN_EPS) * lnw_ref[:, ls] + lnb_ref[:, ls]
        o_ref[sl, ls] = ((y + bonus) * g_ref[sl, ls]).astype(o_ref.dtype)

    gsz = min(RWKV_GROUP, nchunk)

    def group(gi, carry):
        sls = [pl.ds(pl.multiple_of((gi * gsz + g) * CHUNK, CHUNK), CHUNK) for g in range(gsz)]
        tiles = range(r_ref.shape[1] // LANE)
        items = [(sl, slice(q * LANE, (q + 1) * LANE)) for sl in sls for q in tiles]
        for idx, ((sl, ls), part) in enumerate(zip(items, transitions(items))):
            advance(idx % len(tiles), sl, ls, *part)
        return carry

    lax.fori_loop(0, nchunk // gsz, group, 0)


def _rwkv_scan(r, lw, k, v, nk, kb, g, r_k, ln_w, ln_b, bsz, seq, tc=512):
    tc = min(tc, seq)
    ns = seq // tc
    width = RWKV_TILES * LANE
    npair = GROUP // width
    blk = pl.BlockSpec((tc, width), lambda b, p, i: (b * ns + i, p))
    vec = pl.BlockSpec((1, width), lambda b, p, i: (0, p))
    return pl.pallas_call(
        _rwkv_body,
        out_shape=jax.ShapeDtypeStruct((bsz * seq, GROUP), BF16),
        grid=(bsz, npair, ns),
        in_specs=[blk] * 7 + [vec] * 3,
        out_specs=blk,
        scratch_shapes=[pltpu.VMEM((RWKV_TILES, LANE, LANE), F32)],
        compiler_params=_cparams(("parallel", "parallel", "arbitrary")),
        name="rwkv_scan",
    )(r, lw, k, v, nk, kb, g, r_k.reshape(1, GROUP), ln_w.reshape(1, GROUP), ln_b.reshape(1, GROUP))


def _pad_rows(w, start):
    full = jnp.zeros((SMALL, w.shape[1]), F32).at[start:start + w.shape[0]].set(w)
    hi = full.astype(BF16)
    return jnp.stack([hi, (full - hi.astype(F32)).astype(BF16)])


def _layer(x, c8, layer_idx, v_first, vres, bias_ad, bias_as, bias_d, p, attn_tile):
    bsz, seq, d = x.shape
    mod = p["mod"]
    mod3 = mod[:bsz].reshape(bsz * 6, 1, d)
    h = _norm_mod(x, mod3, 1, 0)

    vres_cols = vres[0] if vres is not None else jnp.zeros((d, 32), F32)
    w_r = _regroup_w_in(p["w_in"], layer_idx, vres_cols)
    cols, tail = _in_proj(h, w_r)

    qk, vt = _prep_a(cols, p["diff_q_norm"], p["diff_k_norm"])
    o_a = _diff_attention(qk, vt, bias_ad, bias_as, p["diff_lambda"], p["diff_subln"], bsz, seq, layer_idx,
                          attn_tile)
    o_dd = _swa_attention(cols, tail, bias_d, p["swa_q_norm"], p["swa_k_norm"], p["swa_sinks"], bsz, seq)
    mu = p["rwkv_mu"]
    vres_mu = vres[1] if vres is not None else jnp.zeros((32,), F32)
    mu_small = jnp.concatenate([jnp.zeros((S_WD,), F32), mu[3 * GROUP:], vres_mu,
                                jnp.zeros((SMALL - S_VR - 32,), F32)]).reshape(1, SMALL)
    v_up = _pad_rows(vres[2], S_VR) if vres is not None else jnp.zeros((2, SMALL, GROUP), BF16)
    v0 = vres[3] if vres is not None else jnp.zeros((GROUP,), F32)
    r_, lw_, k_, v_, nk_, kb_, g_, la_ = _prep_c(
        cols, tail, seq, mu[:3 * GROUP].reshape(1, -1), mu_small,
        _pad_rows(p["rwkv_w_up"], S_WD), _pad_rows(p["rwkv_a_up"], S_AD), _pad_rows(p["rwkv_g_up"], S_GD),
        v_up, _pad_rows(p["gla_gate_up"], S_GLA),
        p["rwkv_w0"].reshape(1, -1), p["rwkv_a0"].reshape(1, -1), v0.reshape(1, -1),
        p["rwkv_k_k"].reshape(1, -1), p["rwkv_k_a"].reshape(1, -1), p["gla_gate_bias"].reshape(1, -1),
        v_first if vres is not None else None)
    if vres is None:
        v_first = v_
    o_bb = _gla(cols, la_, p["gla_out_norm"], bsz, seq)
    o_c = _rwkv_scan(r_, lw_, k_, v_, nk_, kb_, g_, p["rwkv_r_k"].reshape(-1), p["rwkv_ln_w"], p["rwkv_ln_b"],
                     bsz, seq)

    x = _out_proj([o_a, o_bb, o_c, o_dd], p["w_out"], x, mod3, 2)
    h2 = _norm_mod(x, mod3, 4, 3)
    act = _ffn_up(h2, p["ffn_w1"], p["ffn_w3"], layer_idx)
    x = _ffn_down(act, p["ffn_w2"], x, mod3, 5)
    return x, v_first


def kernel(x, c, rel_bias, ada_w, ada_b, w_in, w_out, diff_q_norm, diff_k_norm, diff_lambda, diff_subln,
           gla_gate_up, gla_gate_bias, gla_out_norm, rwkv_mu, rwkv_w_up, rwkv_w0, rwkv_a_up, rwkv_a0,
           rwkv_g_up, rwkv_k_k, rwkv_k_a, rwkv_r_k, rwkv_ln_w, rwkv_ln_b, rwkv_vres_down, rwkv_vres_mu,
           rwkv_vres_up, rwkv_v0, swa_q_norm, swa_k_norm, swa_sinks, ffn_w1, ffn_w3, ffn_w2):
    bsz, seq, _ = x.shape
    depth = ada_w.shape[0]
    attn_tile = min(1024, seq)
    bias_ad = _bias_tiles(rel_bias[:, :DIFF_HEADS], attn_tile, attn_tile, 0, None, True, True, LOG2E)
    bias_as = _bias_tiles(rel_bias[:, :DIFF_HEADS], attn_tile, attn_tile, attn_tile, None, True, True, LOG2E)
    bias_d = _bias_tiles(rel_bias[:, DIFF_HEADS:], SWA_BLOCK, 2 * SWA_BLOCK, SWA_BLOCK, SWA_BLOCK, False)
    c8 = jnp.zeros((8, c.shape[1]), F32).at[:bsz].set(c)
    mod_all = _ada(c8, ada_w, ada_b)
    v_first = None
    for l in range(depth):
        p = dict(mod=mod_all[l], w_in=w_in, w_out=_cast_bf16(w_out, l),
                 diff_q_norm=diff_q_norm[l], diff_k_norm=diff_k_norm[l], diff_lambda=diff_lambda[l],
                 diff_subln=diff_subln[l], gla_gate_up=gla_gate_up[l], gla_gate_bias=gla_gate_bias[l],
                 gla_out_norm=gla_out_norm[l], rwkv_mu=rwkv_mu[l], rwkv_w_up=rwkv_w_up[l],
                 rwkv_w0=rwkv_w0[l], rwkv_a_up=rwkv_a_up[l], rwkv_a0=rwkv_a0[l], rwkv_g_up=rwkv_g_up[l],
                 rwkv_k_k=rwkv_k_k[l], rwkv_k_a=rwkv_k_a[l], rwkv_r_k=rwkv_r_k[l], rwkv_ln_w=rwkv_ln_w[l],
                 rwkv_ln_b=rwkv_ln_b[l], swa_q_norm=swa_q_norm[l], swa_k_norm=swa_k_norm[l],
                 swa_sinks=swa_sinks[l], ffn_w1=ffn_w1, ffn_w3=ffn_w3, ffn_w2=_cast_bf16(ffn_w2, l))
        vres = None if l == 0 else (rwkv_vres_down[l - 1], rwkv_vres_mu[l - 1], rwkv_vres_up[l - 1],
                                    rwkv_v0[l - 1])
        x, v_first = _layer(x, c8, l, v_first, vres, bias_ad, bias_as, bias_d, p, attn_tile)
    return x
```
